```python
import functools
import jax, jax.numpy as jnp
from jax import lax
import numpy as np

D_MODEL = 1024
BATCH = 16
SEQ = 2048
DEPTH = 2
DEC_BATCH = 32
DEC_SEQ = 8
PAST_LEN = 16384
PAGE_SIZE = 128

POOL_GROUPS = 4
POOL_GC = D_MODEL // 8
POOL_W = POOL_GROUPS * POOL_GC
POOL_WINDOWS = (2, 4, 8, 16)
POOL_HIST = max(POOL_WINDOWS) - 1
MB_HEADS = 8
MB_HD = 64
MB_W = MB_HEADS * MB_HD
MB_BLOCK = 256
MB_TOPK = 3
MB_QCHUNK = 128
ROPE_THETA = 500000.0
ROPE_DIM = MB_HD // 4
GLA_HEADS = 4
GLA_DK = 64
GLA_DV = 128
GLA_KW = GLA_HEADS * GLA_DK
GLA_VW = GLA_HEADS * GLA_DV
GLA_GATE_RANK = 16
GLA_NORMALIZER = 16.0
GLA_CHUNK = 64
D_FF = 4 * D_MODEL
EPS = 1e-6
IN_SIZES = (POOL_W, MB_W, MB_W, MB_W, GLA_KW, GLA_KW, GLA_VW, GLA_VW, GLA_GATE_RANK,
            D_MODEL, D_MODEL, D_MODEL)
D_IN = sum(IN_SIZES)

kernel_name = "hybrid_pool_moba_gla_decoder_step"


def rmsnorm(x, g):
    xf = x.astype(jnp.float32)
    y = xf * lax.rsqrt(jnp.mean(xf * xf, axis=-1, keepdims=True) + EPS)
    return (y * g.astype(jnp.float32)).astype(x.dtype)


def split_in(proj):
    offs = []
    acc = 0
    for s in IN_SIZES[:-1]:
        acc += s
        offs.append(acc)
    return jnp.split(proj, offs, axis=-1)


def partial_rope(x, pos):
    half = ROPE_DIM // 2
    inv = ROPE_THETA ** (-jnp.arange(half, dtype=jnp.float32) / half)
    ang = pos.astype(jnp.float32)[:, None] * inv[None, :]
    cos = jnp.cos(ang)[None, :, None, :]
    sin = jnp.sin(ang)[None, :, None, :]
    xr = x[..., :ROPE_DIM].astype(jnp.float32)
    x1, x2 = xr[..., :half], xr[..., half:]
    rot = jnp.concatenate([x1 * cos - x2 * sin, x2 * cos + x1 * sin], axis=-1).astype(x.dtype)
    return jnp.concatenate([rot, x[..., ROPE_DIM:]], axis=-1)


def pool_mix(u, hist, pos0, w_grp, scale):
    B, T, _ = u.shape
    ext = jnp.concatenate([hist, u], axis=1)
    extf = ext.astype(jnp.float32)
    cs = jnp.cumsum(extf, axis=1)
    cs = jnp.concatenate([jnp.zeros_like(cs[:, :1]), cs], axis=1)
    pos = pos0 + jnp.arange(T)
    means = []
    for g, w in enumerate(POOL_WINDOWS):
        sl = slice(g * POOL_GC, (g + 1) * POOL_GC)
        end = cs[:, POOL_HIST + 1: POOL_HIST + 1 + T, sl]
        start = cs[:, POOL_HIST + 1 - w: POOL_HIST + 1 - w + T, sl]
        cnt = jnp.minimum(pos + 1, w).astype(jnp.float32)[None, :, None]
        means.append((end - start) / cnt)
    d = jnp.concatenate(means, axis=-1) - u.astype(jnp.float32)
    d = d.reshape(B, T, POOL_GROUPS, POOL_GC)
    y = jnp.einsum('btgc,gcf->btgf', d, w_grp).reshape(B, T, POOL_W) * scale
    return y.astype(u.dtype), ext[:, -POOL_HIST:]


def gla_chunked(q, k, v, logg, s0):
    B, T, H, _ = q.shape
    C = min(GLA_CHUNK, T)
    n = -(-T // C)
    pad = n * C - T
    if pad:
        pw = ((0, 0), (0, pad), (0, 0), (0, 0))
        q, k, v, logg = (jnp.pad(a, pw) for a in (q, k, v, logg))
    to_chunks = lambda a: jnp.moveaxis(a.reshape(B, n, C, H, a.shape[-1]), 1, 0)
    tri = jnp.tril(jnp.ones((C, C), dtype=bool))

    def step(S, inp):
        qc, kc, vc, gc = inp
        b = jnp.cumsum(gc, axis=1)
        o_inter = jnp.einsum('bchk,bhkv->bchv', qc * jnp.exp(b), S)
        diff = b[:, :, None] - b[:, None, :]
        diff = jnp.where(tri[None, :, :, None, None], diff, -jnp.inf)
        A = jnp.einsum('bthk,btshk,bshk->bths', qc, jnp.exp(diff), kc)
        o_intra = jnp.einsum('bths,bshv->bthv', A, vc)
        bl = b[:, -1]
        S_new = S * jnp.exp(bl)[..., None] + jnp.einsum(
            'bshk,bshv->bhkv', kc * jnp.exp(bl[:, None] - b), vc)
        return S_new, o_inter + o_intra

    s_fin, o = lax.scan(step, s0, (to_chunks(q), to_chunks(k), to_chunks(v), to_chunks(logg)))
    o = jnp.moveaxis(o, 0, 1).reshape(B, n * C, H, v.shape[-1])[:, :T]
    return o, s_fin


def moba_core(q, ks, vs, sel_mask, own_k, own_v, own_mask):
    qf = q.astype(jnp.float32) * (MB_HD ** -0.5)
    s_sel = jnp.einsum('qhd,qhld->qhl', qf, ks.astype(jnp.float32))
    s_sel = jnp.where(sel_mask, s_sel, -jnp.inf)
    s_own = jnp.einsum('qhd,khd->qhk', qf, own_k.astype(jnp.float32))
    s_own = jnp.where(own_mask[:, None, :], s_own, -jnp.inf)
    L = s_sel.shape[-1]
    p = jax.nn.softmax(jnp.concatenate([s_sel, s_own], axis=-1), axis=-1)
    o = (jnp.einsum('qhl,qhld->qhd', p[..., :L], vs.astype(jnp.float32))
         + jnp.einsum('qhk,khd->qhd', p[..., L:], own_v.astype(jnp.float32)))
    return o.astype(q.dtype)


def moba_prompt(q, k, v):
    B, T, H, D = q.shape
    nb = -(-T // MB_BLOCK)
    padw = ((0, 0), (0, nb * MB_BLOCK - T), (0, 0), (0, 0))
    kb = jnp.pad(k, padw).reshape(B, nb, MB_BLOCK, H, D)
    vb = jnp.pad(v, padw).reshape(B, nb, MB_BLOCK, H, D)
    kmean = jnp.mean(kb.astype(jnp.float32), axis=2)
    qblk = jnp.arange(T) // MB_BLOCK
    gate = jnp.einsum('bthd,bnhd->bthn', q.astype(jnp.float32), kmean)
    past = jnp.arange(nb)[None, :] < qblk[:, None]
    gate = jnp.where(past[None, :, None, :], gate, -jnp.inf)
    n_sel = min(MB_TOPK, nb)
    _, sel = lax.top_k(gate, n_sel)
    valid = sel < qblk[None, :, None, None]
    nc = T // MB_QCHUNK
    qc = q.reshape(B * nc, MB_QCHUNK, H, D)
    selc = sel.reshape(B * nc, MB_QCHUNK, H, n_sel)
    validc = valid.reshape(B * nc, MB_QCHUNK, H, n_sel)
    bidx = jnp.repeat(jnp.arange(B), nc)
    cidx = jnp.tile(jnp.arange(nc), B)
    hidx = jnp.arange(H)[None, :, None]

    def one_chunk(args):
        qq, ss, vv, b, c = args
        kbb = kb[b]
        vbb = vb[b]
        ks = kbb[ss, :, hidx, :].reshape(MB_QCHUNK, H, n_sel * MB_BLOCK, D)
        vs = vbb[ss, :, hidx, :].reshape(MB_QCHUNK, H, n_sel * MB_BLOCK, D)
        qpos = c * MB_QCHUNK + jnp.arange(MB_QCHUNK)
        ob = (c * MB_QCHUNK) // MB_BLOCK
        own_k = lax.dynamic_index_in_dim(kbb, ob, axis=0, keepdims=False)
        own_v = lax.dynamic_index_in_dim(vbb, ob, axis=0, keepdims=False)
        own_pos = ob * MB_BLOCK + jnp.arange(MB_BLOCK)
        own_mask = own_pos[None, :] <= qpos[:, None]
        sel_mask = jnp.repeat(vv, MB_BLOCK, axis=-1)
        return moba_core(qq, ks, vs, sel_mask, own_k, own_v, own_mask)

    o = lax.map(one_chunk, (qc, selc, validc, bidx, cidx))
    return o.reshape(B, T, H, D)


def moba_sample(q, k, v, pool_k, pool_v, page_table):
    B, T, H, D = q.shape
    ppb = MB_BLOCK // PAGE_SIZE
    n_full = PAST_LEN // MB_BLOCK
    own_start = n_full * MB_BLOCK
    n_tail = PAST_LEN - own_start
    qpos = PAST_LEN + jnp.arange(T)
    qblk = qpos // MB_BLOCK
    if n_full > 0:
        kpast = pool_k[page_table[:, :n_full * ppb]]
        kmean = jnp.mean(kpast.astype(jnp.float32).reshape(B, n_full, MB_BLOCK, H, D), axis=2)
        gate = jnp.einsum('bthd,bnhd->bthn', q.astype(jnp.float32), kmean)
        n_sel = min(MB_TOPK, n_full)
        _, sel = lax.top_k(gate, n_sel)
        valid = sel < qblk[None, :, None, None]
        logical = sel[..., None] * ppb + jnp.arange(ppb)
        phys = jax.vmap(lambda pt, lg: pt[lg])(page_table, logical)
        hidx = jnp.arange(H)[None, None, :, None, None]
        ks = pool_k[phys, :, hidx, :].reshape(B, T, H, n_sel * MB_BLOCK, D)
        vs = pool_v[phys, :, hidx, :].reshape(B, T, H, n_sel * MB_BLOCK, D)
        sel_mask = jnp.repeat(valid, MB_BLOCK, axis=-1)
    else:
        ks = jnp.zeros((B, T, H, 0, D), k.dtype)
        vs = jnp.zeros((B, T, H, 0, D), v.dtype)
        sel_mask = jnp.zeros((B, T, H, 0), dtype=bool)
    tail_pages = page_table[:, own_start // PAGE_SIZE: PAST_LEN // PAGE_SIZE]
    own_k = jnp.concatenate([pool_k[tail_pages].reshape(B, n_tail, H, D), k], axis=1)
    own_v = jnp.concatenate([pool_v[tail_pages].reshape(B, n_tail, H, D), v], axis=1)
    own_pos = own_start + jnp.arange(n_tail + T)
    own_mask = ((own_pos[None, :] <= qpos[:, None])
                & (own_pos[None, :] >= (qblk * MB_BLOCK)[:, None]))
    return jax.vmap(moba_core, in_axes=(0, 0, 0, 0, 0, 0, None))(
        q, ks, vs, sel_mask, own_k, own_v, own_mask)


def hybrid_layer(x, pos0, pool_hist, gla_s0, attend, w_in, w_gla_a2, b_gla_a, gla_norm,
                 w_pool_grp, pool_scale, w_pool_out, w_mb_out, w_gla_out, w_o,
                 norm_mix_pre, norm_mix_post, norm_ffn_pre, norm_ffn_post, w_ff1, w_ff2):
    B, T, _ = x.shape
    f32 = jnp.float32
    h = rmsnorm(x, norm_mix_pre)
    (u_pool, q_mb, k_mb, v_mb, q_g, k_g, v_g, r_g, a_g,
     g_pool, g_mb, g_gla) = split_in(h @ w_in)
    y_pool, new_hist = pool_mix(u_pool, pool_hist, pos0, w_pool_grp, pool_scale)
    pos = pos0 + jnp.arange(T)
    qh = partial_rope(q_mb.reshape(B, T, MB_HEADS, MB_HD), pos)
    kh = partial_rope(k_mb.reshape(B, T, MB_HEADS, MB_HD), pos)
    vh = v_mb.reshape(B, T, MB_HEADS, MB_HD)
    y_mb = attend(qh, kh, vh).reshape(B, T, MB_W)
    qg = q_g.reshape(B, T, GLA_HEADS, GLA_DK).astype(f32) * (GLA_DK ** -0.5)
    kg = k_g.reshape(B, T, GLA_HEADS, GLA_DK).astype(f32)
    vg = v_g.reshape(B, T, GLA_HEADS, GLA_DV).astype(f32)
    logg = jax.nn.log_sigmoid((a_g @ w_gla_a2 + b_gla_a).astype(f32)).reshape(
        B, T, GLA_HEADS, GLA_DK) / GLA_NORMALIZER
    o_g, new_s = gla_chunked(qg, kg, vg, logg, gla_s0.astype(f32))
    o_g = rmsnorm(o_g, gla_norm.reshape(GLA_HEADS, GLA_DV)).reshape(B, T, GLA_VW)
    y_gla = (o_g * jax.nn.silu(r_g.astype(f32))).astype(x.dtype)
    sg = lambda g: jax.nn.sigmoid(g.astype(f32)).astype(x.dtype)
    merged = (sg(g_pool) * (y_pool @ w_pool_out) + sg(g_mb) * (y_mb @ w_mb_out)
              + sg(g_gla) * (y_gla @ w_gla_out))
    x = x + rmsnorm(merged @ w_o, norm_mix_post)
    f = jnp.square(jax.nn.relu(rmsnorm(x, norm_ffn_pre) @ w_ff1)) @ w_ff2
    x = x + rmsnorm(f, norm_ffn_post)
    return x, kh, vh, new_hist, new_s.astype(x.dtype)


def setup_inputs(seed: int = 0) -> dict:
    key = jax.random.key(seed)
    ks = jax.random.split(key, 26)
    f32 = jnp.float32
    n_pages = PAST_LEN // PAGE_SIZE
    n_used = DEC_BATCH * n_pages
    n_phys = n_used + n_used // 4

    def nrm(k, shape, scale=1.0):
        return jax.random.normal(k, shape, f32) * scale

    page_table = jax.random.permutation(ks[0], n_phys)[:n_used].reshape(
        DEC_BATCH, n_pages).astype(jnp.int32)
    return {
        "x_prompt": nrm(ks[1], (BATCH, SEQ, D_MODEL)),
        "x_sample": nrm(ks[2], (DEC_BATCH, DEC_SEQ, D_MODEL)),
        "cache_k": nrm(ks[3], (DEPTH, n_phys, PAGE_SIZE, MB_HEADS, MB_HD)),
        "cache_v": nrm(ks[4], (DEPTH, n_phys, PAGE_SIZE, MB_HEADS, MB_HD)),
        "state_pool": nrm(ks[5], (DEPTH, DEC_BATCH, POOL_HIST, POOL_W)),
        "state_gla": nrm(ks[6], (DEPTH, DEC_BATCH, GLA_HEADS, GLA_DK, GLA_DV), 0.5),
        "page_table": page_table,
        "w_in": nrm(ks[7], (DEPTH, D_MODEL, D_IN), D_MODEL ** -0.5),
        "w_gla_a2": nrm(ks[8], (DEPTH, GLA_GATE_RANK, GLA_KW), GLA_GATE_RANK ** -0.5),
        "b_gla_a": nrm(ks[9], (DEPTH, GLA_KW), 0.1),
        "gla_norm": 1.0 + nrm(ks[10], (DEPTH, GLA_VW), 0.02),
        "w_pool_grp": nrm(ks[11], (DEPTH, POOL_GROUPS, POOL_GC, POOL_GC), POOL_GC ** -0.5),
        "pool_scale": 1.0 + nrm(ks[12], (DEPTH, POOL_W), 0.02),
        "w_pool_out": nrm(ks[13], (DEPTH, POOL_W, D_MODEL), POOL_W ** -0.5),
        "w_mb_out": nrm(ks[14], (DEPTH, MB_W, D_MODEL), MB_W ** -0.5),
        "w_gla_out": nrm(ks[15], (DEPTH, GLA_VW, D_MODEL), GLA_VW ** -0.5),
        "w_o": nrm(ks[16], (DEPTH, D_MODEL, D_MODEL), D_MODEL ** -0.5),
        "norm_mix_pre": 1.0 + nrm(ks[17], (DEPTH, D_MODEL), 0.02),
        "norm_mix_post": 1.0 + nrm(ks[18], (DEPTH, D_MODEL), 0.02),
        "norm_ffn_pre": 1.0 + nrm(ks[19], (DEPTH, D_MODEL), 0.02),
        "norm_ffn_post": 1.0 + nrm(ks[20], (DEPTH, D_MODEL), 0.02),
        "w_ff1": nrm(ks[21], (DEPTH, D_MODEL, D_FF), D_MODEL ** -0.5),
        "w_ff2": nrm(ks[22], (DEPTH, D_FF, D_MODEL), D_FF ** -0.5),
    }


def reference(x_prompt, x_sample, cache_k, cache_v, state_pool, state_gla, page_table,
              w_in, w_gla_a2, b_gla_a, gla_norm, w_pool_grp, pool_scale, w_pool_out,
              w_mb_out, w_gla_out, w_o, norm_mix_pre, norm_mix_post, norm_ffn_pre,
              norm_ffn_post, w_ff1, w_ff2):
    B = x_prompt.shape[0]
    hp = x_prompt
    hs = x_sample
    kp_l, vp_l, pp_l, gp_l = [], [], [], []
    ks_l, vs_l, ps_l, gs_l = [], [], [], []
    for l in range(DEPTH):
        lw = (w_in[l], w_gla_a2[l], b_gla_a[l], gla_norm[l], w_pool_grp[l], pool_scale[l],
              w_pool_out[l], w_mb_out[l], w_gla_out[l], w_o[l], norm_mix_pre[l],
              norm_mix_post[l], norm_ffn_pre[l], norm_ffn_post[l], w_ff1[l], w_ff2[l])
        hp, kp, vp, pp, gp = hybrid_layer(
            hp, 0, jnp.zeros((B, POOL_HIST, POOL_W), hp.dtype),
            jnp.zeros((B, GLA_HEADS, GLA_DK, GLA_DV), jnp.float32), moba_prompt, *lw)
        attend_s = functools.partial(moba_sample, pool_k=cache_k[l], pool_v=cache_v[l],
                                     page_table=page_table)
        hs, ksm, vsm, psm, gsm = hybrid_layer(hs, PAST_LEN, state_pool[l], state_gla[l],
                                              attend_s, *lw)
        kp_l.append(kp); vp_l.append(vp); pp_l.append(pp); gp_l.append(gp)
        ks_l.append(ksm); vs_l.append(vsm); ps_l.append(psm); gs_l.append(gsm)
    y_prompt = hp
    y_sample = hs
    k_prompt = jnp.stack(kp_l)
    v_prompt = jnp.stack(vp_l)
    pool_prompt = jnp.stack(pp_l)
    gla_prompt = jnp.stack(gp_l)
    k_sample = jnp.stack(ks_l)
    v_sample = jnp.stack(vs_l)
    pool_sample = jnp.stack(ps_l)
    gla_sample = jnp.stack(gs_l)
    return (y_prompt, y_sample, k_prompt, v_prompt, pool_prompt, gla_prompt,
            k_sample, v_sample, pool_sample, gla_sample)
```

```python
import functools

import jax
import jax.numpy as jnp
from jax import lax
from jax.experimental import pallas as pl
from jax.experimental.pallas import tpu as pltpu

F32 = jnp.float32
BF16 = jnp.bfloat16

D_MODEL = 1024
PAGE_SIZE = 128
POOL_GC = 128
POOL_W = 512
POOL_WINDOWS = (2, 4, 8, 16)
POOL_HIST = 15
MB_HEADS = 8
MB_HD = 64
MB_W = 512
MB_BLOCK = 256
MB_TOPK = 3
ROPE_THETA = 500000.0
ROPE_DIM = 16
GLA_HEADS = 4
GLA_DK = 64
GLA_DV = 128
GLA_KW = 256
GLA_VW = 512
GLA_RANK = 16
GLA_NORMALIZER = 16.0
GLA_CHUNK = 64
D_FF = 4096
EPS = 1e-6
W_MAIN = POOL_W + 3 * MB_W + 2 * GLA_KW + 2 * GLA_VW
NEG_INF = float("-inf")

VMEM_LIMIT_BYTES = 56 * 1024 * 1024
ROW_TILE = 256


def _cparams(sem):
    return pltpu.CompilerParams(dimension_semantics=sem, vmem_limit_bytes=VMEM_LIMIT_BYTES)


def _const_spec(shape):
    nd = len(shape)
    return pl.BlockSpec(shape, lambda *_: (0,) * nd, pipeline_mode=pl.Buffered(1))


def _rms(x, g):
    ms = jnp.mean(x * x, axis=-1, keepdims=True)
    return x * lax.rsqrt(ms + EPS) * g


def _sigmoid(x):
    return 1.0 / (1.0 + jnp.exp(-x))


def _split3(x):
    hi = x.astype(BF16)
    r = x - hi.astype(F32)
    mid = r.astype(BF16)
    lo = (r - mid.astype(F32)).astype(BF16)
    return hi, mid, lo


def _in_proj_kernel(x_ref, gpre_ref, wm_ref, wa_ref, wg_ref, wa2_ref, ba_ref,
                    cos_ref, sa_ref, sb_ref,
                    u_ref, q_ref, k_ref, v_ref, qg_ref, kg_ref, vg_ref, rg_ref, lg_ref, sg_ref):
    h = _rms(x_ref[...], gpre_ref[...]).astype(BF16)

    def proj(w_ref, a, b):
        return jnp.dot(h, w_ref[:, a:b], preferred_element_type=F32)

    cos, sa, sb = cos_ref[...], sa_ref[...], sb_ref[...]

    def rope_store(o_ref, col0):
        for j in range(MB_W // 128):
            xj = proj(wm_ref, col0 + 128 * j, col0 + 128 * (j + 1))
            half = ROPE_DIM // 2
            o_ref[:, 128 * j:128 * (j + 1)] = (
                xj * cos + pltpu.roll(xj, 128 - half, 1) * sa + pltpu.roll(xj, half, 1) * sb)

    u_ref[...] = proj(wm_ref, 0, 512)
    rope_store(q_ref, 512)
    rope_store(k_ref, 1024)
    v_ref[...] = proj(wm_ref, 1536, 2048)
    qg_ref[...] = proj(wm_ref, 2048, 2304)
    kg_ref[...] = proj(wm_ref, 2304, 2560)
    vg_ref[...] = proj(wm_ref, 2560, 3072)
    r = proj(wm_ref, 3072, 3584)
    rg_ref[...] = r * _sigmoid(r)
    a = jnp.dot(h, wa_ref[...], preferred_element_type=F32)
    xg = jnp.dot(a.astype(BF16), wa2_ref[...], preferred_element_type=F32) + ba_ref[...]
    lg_ref[...] = (jnp.minimum(xg, 0.0) - jnp.log1p(jnp.exp(-jnp.abs(xg)))) * (1.0 / GLA_NORMALIZER)
    for j in range(3):
        sg_ref[:, 1024 * j:1024 * (j + 1)] = _sigmoid(proj(wg_ref, 1024 * j, 1024 * (j + 1)))


def _in_proj(x, gpre, wm, wa, wg, wa2, ba, tabs):
    M = x.shape[0]
    tm = min(ROW_TILE, M)
    period = tabs[0].shape[0] // tm
    row = lambda w: pl.BlockSpec((tm, w), lambda i: (i, 0))
    tab = pl.BlockSpec((tm, 128), lambda i: (i % period, 0))
    widths = (POOL_W, MB_W, MB_W, MB_W, GLA_KW, GLA_KW, GLA_VW, GLA_VW, GLA_KW, 3 * D_MODEL)
    return pl.pallas_call(
        _in_proj_kernel,
        grid=(M // tm,),
        in_specs=[row(D_MODEL), _const_spec((1, D_MODEL)), _const_spec(wm.shape), _const_spec(wa.shape),
                  _const_spec(wg.shape), _const_spec(wa2.shape), _const_spec((1, GLA_KW)), tab, tab, tab],
        out_specs=[row(w) for w in widths],
        out_shape=[jax.ShapeDtypeStruct((M, w), F32) for w in widths],
        compiler_params=_cparams(("arbitrary",)),
        name="in_proj",
    )(x, gpre, wm, wa, wg, wa2, ba, *tabs)


def _rope_tables(pos):
    half = ROPE_DIM // 2
    inv = ROPE_THETA ** (-jnp.arange(half, dtype=F32) / half)
    ang = pos.astype(F32)[:, None] * inv[None, :]
    cos, sin = jnp.cos(ang), jnp.sin(ang)
    n = pos.shape[0]
    zeros = jnp.zeros((n, MB_HD - ROPE_DIM), F32)
    zh = jnp.zeros((n, half), F32)
    c = jnp.concatenate([cos, cos, zeros + 1.0], axis=1)
    sa = jnp.concatenate([-sin, zh, zeros], axis=1)
    sb = jnp.concatenate([zh, sin, zeros], axis=1)
    two = lambda t: jnp.concatenate([t, t], axis=1)
    return two(c), two(sa), two(sb)


def _pool_kernel(u_ref, hist_ref, wgrp_ref, scale_ref, y_ref, ext_ref, *, pos0, T, R):
    ext_ref[0:16, :] = hist_ref[0]
    ext_ref[16:16 + T, :] = u_ref[0]
    for c in range(T // R):
        r0 = 16 + c * R
        posp1 = lax.broadcasted_iota(jnp.int32, (R, 1), 0) + (pos0 + c * R + 1)
        for g, w in enumerate(POOL_WINDOWS):
            lanes = slice(g * POOL_GC, (g + 1) * POOL_GC)
            cur = ext_ref[r0:r0 + R, lanes]
            acc = cur
            for j in range(1, w):
                acc = acc + ext_ref[r0 - j:r0 - j + R, lanes]
            cnt = jnp.minimum(posp1, w).astype(F32)
            d = acc / cnt - cur
            yg = jnp.dot(d.astype(BF16), wgrp_ref[g], preferred_element_type=F32)
            y_ref[0, c * R:(c + 1) * R, lanes] = yg * scale_ref[:, lanes]


def _pool(u, hist16, wgrp, scale, pos0):
    Bn, T, _ = u.shape
    R = min(T, 256)
    seq = pl.BlockSpec((1, T, POOL_W), lambda b: (b, 0, 0))
    return pl.pallas_call(
        functools.partial(_pool_kernel, pos0=pos0, T=T, R=R),
        grid=(Bn,),
        in_specs=[seq, pl.BlockSpec((1, 16, POOL_W), lambda b: (b, 0, 0)),
                  _const_spec(wgrp.shape), _const_spec((1, POOL_W))],
        out_specs=seq,
        out_shape=jax.ShapeDtypeStruct((Bn, T, POOL_W), F32),
        scratch_shapes=[pltpu.VMEM((16 + T, POOL_W), F32)],
        compiler_params=_cparams(("arbitrary",)),
        name="pool",
    )(u, hist16, wgrp, scale)


def _top_blocks(gate, n_valid, n_blocks):
    iota = lax.broadcasted_iota(jnp.int32, gate.shape, 1)
    g = jnp.where(iota < n_valid, gate, NEG_INF)
    sel = jnp.zeros(gate.shape, F32)
    for _ in range(MB_TOPK):
        m = jnp.max(g, axis=1, keepdims=True)
        idx = jnp.min(jnp.where(g == m, iota, n_blocks), axis=1, keepdims=True)
        pick = (iota == idx) & (m > NEG_INF)
        sel = jnp.where(pick, 1.0, sel)
        g = jnp.where(pick, NEG_INF, g)
    return sel


def _dot_nt(a, b, precision=None):
    return lax.dot_general(a, b, (((1,), (1,)), ((), ())), preferred_element_type=F32, precision=precision)


def _moba_prompt_kernel(q_ref, k_ref, v_ref, y_ref, kmean_ref, *, n_blocks):
    qi = pl.program_id(2)

    @pl.when(qi == 0)
    def _():
        for n in range(n_blocks):
            kb = k_ref[0, n * MB_BLOCK:(n + 1) * MB_BLOCK, :]
            kmean_ref[n:n + 1, :] = jnp.sum(kb, axis=0, keepdims=True) * (1.0 / MB_BLOCK)

    row = lax.broadcasted_iota(jnp.int32, (MB_BLOCK, MB_BLOCK), 0)
    col = lax.broadcasted_iota(jnp.int32, (MB_BLOCK, MB_BLOCK), 1)
    blk_iota = lax.broadcasted_iota(jnp.int32, (MB_BLOCK, n_blocks), 1)
    own0 = pl.multiple_of(qi * MB_BLOCK, MB_BLOCK)
    outs = []
    for hh in range(2):
        lanes = slice(hh * MB_HD, (hh + 1) * MB_HD)
        qh = q_ref[0, :, lanes]
        gate = _dot_nt(qh, kmean_ref[:, lanes], precision=lax.Precision.HIGHEST)
        sel = _top_blocks(gate, qi, n_blocks)
        qs = (qh * (MB_HD ** -0.5)).astype(BF16)

        s = _dot_nt(qs, k_ref[0, pl.ds(own0, MB_BLOCK), lanes].astype(BF16))
        s = jnp.where(col <= row, s, NEG_INF)
        m0 = jnp.max(s, axis=1, keepdims=True)
        p = jnp.exp(s - m0)
        l0 = jnp.sum(p, axis=1, keepdims=True)
        acc0 = jnp.dot(p.astype(BF16), v_ref[0, pl.ds(own0, MB_BLOCK), lanes].astype(BF16),
                       preferred_element_type=F32)

        def body(j, carry):
            m, l, acc = carry
            j0 = pl.multiple_of(j * MB_BLOCK, MB_BLOCK)
            sj = _dot_nt(qs, k_ref[0, pl.ds(j0, MB_BLOCK), lanes].astype(BF16))
            chosen = jnp.max(jnp.where(blk_iota == j, sel, 0.0), axis=1, keepdims=True) > 0.0
            sj = jnp.where(chosen, sj, NEG_INF)
            m_new = jnp.maximum(m, jnp.max(sj, axis=1, keepdims=True))
            alpha = jnp.exp(m - m_new)
            pj = jnp.exp(sj - m_new)
            l_new = alpha * l + jnp.sum(pj, axis=1, keepdims=True)
            acc_new = alpha * acc + jnp.dot(pj.astype(BF16), v_ref[0, pl.ds(j0, MB_BLOCK), lanes].astype(BF16),
                                            preferred_element_type=F32)
            return m_new, l_new, acc_new

        m, l, acc = lax.fori_loop(0, qi, body, (m0, l0, acc0))
        outs.append(acc / l)
    y_ref[0] = jnp.concatenate(outs, axis=1)


def _moba_prompt(q, k, v):
    Bn, T, _ = q.shape
    n_blocks = T // MB_BLOCK
    qspec = pl.BlockSpec((1, MB_BLOCK, 128), lambda b, hp, qi: (b, qi, hp))
    kvspec = pl.BlockSpec((1, T, 128), lambda b, hp, qi: (b, 0, hp))
    return pl.pallas_call(
        functools.partial(_moba_prompt_kernel, n_blocks=n_blocks),
        grid=(Bn, MB_W // 128, n_blocks),
        in_specs=[qspec, kvspec, kvspec],
        out_specs=qspec,
        out_shape=jax.ShapeDtypeStruct((Bn, T, MB_W), F32),
        scratch_shapes=[pltpu.VMEM((n_blocks, 128), F32)],
        compiler_params=_cparams(("arbitrary", "arbitrary", "arbitrary")),
        name="moba_prompt",
    )(q, k, v)


def _moba_sample_kernel(pt_ref, q_ref, kn_ref, vn_ref, kc_ref, vc_ref, y_ref,
                        qbd_ref, s_ref, ksum_ref, acc_ref, l_ref, *, n_pages, ts):
    ph = pl.program_id(1)
    p = pl.program_id(2)
    R = MB_HEADS * ts
    ppb = MB_BLOCK // PAGE_SIZE
    n_blocks = n_pages // ppb
    rhead = lax.broadcasted_iota(jnp.int32, (R, MB_W), 0) // ts
    chead = lax.broadcasted_iota(jnp.int32, (R, MB_W), 1) // MB_HD
    p0 = pl.multiple_of(p * PAGE_SIZE, PAGE_SIZE)

    @pl.when((ph == 0) & (p == 0))
    def _():
        qt = jnp.concatenate([q_ref[0]] * MB_HEADS, axis=0)
        qbd_ref[...] = jnp.where(rhead == chead, qt, 0.0)
        ksum_ref[...] = jnp.zeros(ksum_ref.shape, F32)

    @pl.when(ph == 0)
    def _():
        kp = kc_ref[0, 0]
        qs = (qbd_ref[...] * (MB_HD ** -0.5)).astype(BF16)
        s_ref[:, pl.ds(p0, PAGE_SIZE)] = _dot_nt(qs, kp.astype(BF16))
        blk = lax.broadcasted_iota(jnp.int32, (n_blocks, 1), 0)
        ksum_ref[...] += jnp.where(blk == p // ppb, jnp.sum(kp, axis=0, keepdims=True), 0.0)

    @pl.when((ph == 0) & (p == n_pages - 1))
    def _():
        kmean = ksum_ref[...] * (1.0 / MB_BLOCK)
        gate = _dot_nt(qbd_ref[...], kmean, precision=lax.Precision.HIGHEST)
        sel = _top_blocks(gate, n_blocks, n_blocks)
        qs = (qbd_ref[...] * (MB_HD ** -0.5)).astype(BF16)
        s_own = _dot_nt(qs, kn_ref[0].astype(BF16))
        tq = lax.broadcasted_iota(jnp.int32, (R, ts), 0) % ts
        tk = lax.broadcasted_iota(jnp.int32, (R, ts), 1)
        s_own = jnp.where(tk <= tq, s_own, NEG_INF)
        m = jnp.max(s_own, axis=1, keepdims=True)
        for n in range(n_blocks):
            cols = slice(n * MB_BLOCK, (n + 1) * MB_BLOCK)
            sn = jnp.where(sel[:, n:n + 1] > 0.0, s_ref[:, cols], NEG_INF)
            s_ref[:, cols] = sn
            m = jnp.maximum(m, jnp.max(sn, axis=1, keepdims=True))
        p_own = jnp.exp(s_own - m)
        l = jnp.sum(p_own, axis=1, keepdims=True)
        for n in range(n_blocks):
            cols = slice(n * MB_BLOCK, (n + 1) * MB_BLOCK)
            pn = jnp.exp(s_ref[:, cols] - m)
            s_ref[:, cols] = pn
            l = l + jnp.sum(pn, axis=1, keepdims=True)
        l_ref[...] = l
        acc_ref[...] = jnp.dot(p_own.astype(BF16), vn_ref[0].astype(BF16), preferred_element_type=F32)

    @pl.when(ph == 1)
    def _():
        pp = s_ref[:, pl.ds(p0, PAGE_SIZE)].astype(BF16)
        acc_ref[...] += jnp.dot(pp, vc_ref[0, 0].astype(BF16), preferred_element_type=F32)

    @pl.when((ph == 1) & (p == n_pages - 1))
    def _():
        o = jnp.where(rhead == chead, acc_ref[...] / l_ref[...], 0.0)
        y = o[0:ts]
        for h in range(1, MB_HEADS):
            y = y + o[h * ts:(h + 1) * ts]
        y_ref[0] = y


def _moba_sample(q, kn, vn, cache_k, cache_v, page_table, layer):
    DB, ts, _ = q.shape
    n_pages = page_table.shape[1]
    n_blocks = n_pages * PAGE_SIZE // MB_BLOCK
    R = MB_HEADS * ts
    tok = pl.BlockSpec((1, ts, MB_W), lambda b, ph, p, pt: (b, 0, 0))
    last = n_pages - 1
    kspec = pl.BlockSpec((1, 1, PAGE_SIZE, MB_W),
                         lambda b, ph, p, pt: (layer, pt[b * n_pages + jnp.where(ph == 0, p, last)], 0, 0))
    vspec = pl.BlockSpec((1, 1, PAGE_SIZE, MB_W),
                         lambda b, ph, p, pt: (layer, pt[b * n_pages + jnp.where(ph == 0, 0, p)], 0, 0))
    grid_spec = pltpu.PrefetchScalarGridSpec(
        num_scalar_prefetch=1,
        grid=(DB, 2, n_pages),
        in_specs=[tok, tok, tok, kspec, vspec],
        out_specs=tok,
        scratch_shapes=[pltpu.VMEM((R, MB_W), F32),
                        pltpu.VMEM((R, n_pages * PAGE_SIZE), F32),
                        pltpu.VMEM((n_blocks, MB_W), F32),
                        pltpu.VMEM((R, MB_W), F32),
                        pltpu.VMEM((R, 1), F32)],
    )
    return pl.pallas_call(
        functools.partial(_moba_sample_kernel, n_pages=n_pages, ts=ts),
        grid_spec=grid_spec,
        out_shape=jax.ShapeDtypeStruct((DB, ts, MB_W), F32),
        compiler_params=_cparams(("arbitrary", "arbitrary", "arbitrary")),
        name="moba_sample",
    )(page_table.reshape(-1), q, kn, vn, cache_k, cache_v)


def _gla_kernel(qg_ref, kg_ref, vg_ref, rg_ref, lg_ref, s0_ref, gn_ref, y_ref, sfin_ref, st_ref, *, n_chunks):
    t = pl.program_id(1)
    C = GLA_CHUNK

    @pl.when(t == 0)
    def _():
        for h in range(GLA_HEADS):
            st_ref[:, h * GLA_DK:(h + 1) * GLA_DK] = s0_ref[0, h].T

    ri = lax.broadcasted_iota(jnp.int32, (C, C), 0)
    ci = lax.broadcasted_iota(jnp.int32, (C, C), 1)
    tri = (ci <= ri).astype(BF16)
    m1 = (ri >= 32) & (ci < 32)
    m2 = (ri // 32 == ci // 32) & (ri % 32 >= 16) & (ci % 32 < 16)
    m3 = (ri // 16 == ci // 16) & (ci <= ri)
    grp = lax.broadcasted_iota(jnp.int32, (C, GLA_KW), 0)

    def chunk(c, carry):
        r0 = pl.multiple_of(c * C, C)
        rows = pl.ds(r0, C)
        lg = lg_ref[0, rows, :]
        hi, mid, lo = _split3(lg)
        b = (jnp.dot(tri, hi, preferred_element_type=F32) + jnp.dot(tri, mid, preferred_element_type=F32)
             + jnp.dot(tri, lo, preferred_element_type=F32))
        q = qg_ref[0, rows, :] * (GLA_DK ** -0.5)
        k = kg_ref[0, rows, :]
        brow = lambda i: jnp.broadcast_to(b[i:i + 1, :], (C, GLA_KW))
        ref1 = brow(31)
        ref2 = jnp.where(grp < 32, brow(15), brow(47))
        ref3 = jnp.where(grp < 16, brow(7), jnp.where(grp < 32, brow(23), jnp.where(grp < 48, brow(39), brow(55))))
        b_last = b[C - 1:C, :]
        qs = (q * jnp.exp(b)).astype(BF16)
        ks = (k * jnp.exp(b_last - b)).astype(BF16)
        q1 = (q * jnp.exp(jnp.minimum(b - ref1, 0.0))).astype(BF16)
        k1 = (k * jnp.exp(jnp.minimum(ref1 - b, 0.0))).astype(BF16)
        q2 = (q * jnp.exp(jnp.minimum(b - ref2, 0.0))).astype(BF16)
        k2 = (k * jnp.exp(jnp.minimum(ref2 - b, 0.0))).astype(BF16)
        q3 = (q * jnp.exp(b - ref3)).astype(BF16)
        k3 = (k * jnp.exp(ref3 - b)).astype(BF16)
        st_old = st_ref[...]
        st_ref[...] = st_old * jnp.exp(b_last)
        for h in range(GLA_HEADS):
            kl = slice(h * GLA_DK, (h + 1) * GLA_DK)
            vl = slice(h * GLA_DV, (h + 1) * GLA_DV)
            a = (jnp.where(m1, _dot_nt(q1[:, kl], k1[:, kl]), 0.0)
                 + jnp.where(m2, _dot_nt(q2[:, kl], k2[:, kl]), 0.0)
                 + jnp.where(m3, _dot_nt(q3[:, kl], k3[:, kl]), 0.0))
            vf = vg_ref[0, rows, vl]
            o = (_dot_nt(qs[:, kl], st_old[:, kl].astype(BF16))
                 + jnp.dot(a.astype(BF16), vf.astype(BF16), preferred_element_type=F32))
            st_ref[:, kl] += jnp.dot(vf.T.astype(BF16), ks[:, kl], preferred_element_type=F32)
            y_ref[0, rows, vl] = _rms(o, gn_ref[:, vl]) * rg_ref[0, rows, vl]
        return carry

    lax.fori_loop(0, n_chunks, chunk, 0)

    @pl.when(t == pl.num_programs(1) - 1)
    def _():
        for h in range(GLA_HEADS):
            sfin_ref[0, h] = st_ref[:, h * GLA_DK:(h + 1) * GLA_DK].T


def _gla(qg, kg, vg, rg, lg, s0, gn):
    Bn, T, _ = qg.shape
    TT = min(T, 512)
    kspec = pl.BlockSpec((1, TT, GLA_KW), lambda b, t: (b, t, 0))
    vspec = pl.BlockSpec((1, TT, GLA_VW), lambda b, t: (b, t, 0))
    sspec = pl.BlockSpec((1, GLA_HEADS, GLA_DK, GLA_DV), lambda b, t: (b, 0, 0, 0))
    return pl.pallas_call(
        functools.partial(_gla_kernel, n_chunks=TT // GLA_CHUNK),
        grid=(Bn, T // TT),
        in_specs=[kspec, kspec, vspec, vspec, kspec, sspec, _const_spec((1, GLA_VW))],
        out_specs=[vspec, sspec],
        out_shape=[jax.ShapeDtypeStruct((Bn, T, GLA_VW), F32),
                   jax.ShapeDtypeStruct((Bn, GLA_HEADS, GLA_DK, GLA_DV), F32)],
        scratch_shapes=[pltpu.VMEM((GLA_DV, GLA_KW), F32)],
        compiler_params=_cparams(("arbitrary", "arbitrary")),
        name="gla",
    )(qg, kg, vg, rg, lg, s0, gn)


def _merge_ffn_kernel(x_ref, yp_ref, ym_ref, yg_ref, sg_ref, wp_ref, wmb_ref, wgl_ref, wo_ref,
                      gpost_ref, gfpre_ref, gfpost_ref, w1_ref, w2_ref, o_ref):
    def br(y_ref, w_ref, j):
        return sg_ref[:, 1024 * j:1024 * (j + 1)] * jnp.dot(y_ref[...].astype(BF16), w_ref[...],
                                                            preferred_element_type=F32)

    merged = br(yp_ref, wp_ref, 0) + br(ym_ref, wmb_ref, 1) + br(yg_ref, wgl_ref, 2)
    z = jnp.dot(merged.astype(BF16), wo_ref[...], preferred_element_type=F32)
    x1 = x_ref[...] + _rms(z, gpost_ref[...])
    hb = _rms(x1, gfpre_ref[...]).astype(BF16)
    f = jnp.zeros(x1.shape, F32)
    FC = 1024
    for c in range(D_FF // FC):
        a = jnp.maximum(jnp.dot(hb, w1_ref[:, c * FC:(c + 1) * FC], preferred_element_type=F32), 0.0)
        f = f + jnp.dot((a * a).astype(BF16), w2_ref[c * FC:(c + 1) * FC, :], preferred_element_type=F32)
    o_ref[...] = x1 + _rms(f, gfpost_ref[...])


def _merge_ffn(x, yp, ym, yg, sg, wp, wmb, wgl, wo, gpost, gfpre, gfpost, w1, w2):
    M = x.shape[0]
    tm = min(ROW_TILE, M)
    row = lambda w: pl.BlockSpec((tm, w), lambda i: (i, 0))
    vec = _const_spec((1, D_MODEL))
    return pl.pallas_call(
        _merge_ffn_kernel,
        grid=(M // tm,),
        in_specs=[row(D_MODEL), row(POOL_W), row(MB_W), row(GLA_VW), row(3 * D_MODEL),
                  _const_spec(wp.shape), _const_spec(wmb.shape), _const_spec(wgl.shape), _const_spec(wo.shape),
                  vec, vec, vec, _const_spec(w1.shape), _const_spec(w2.shape)],
        out_specs=row(D_MODEL),
        out_shape=jax.ShapeDtypeStruct((M, D_MODEL), F32),
        compiler_params=_cparams(("arbitrary",)),
        name="merge_ffn",
    )(x, yp, ym, yg, sg, wp, wmb, wgl, wo, gpost, gfpre, gfpost, w1, w2)


def _layer(x, Bn, T, tabs, hist16, pos0, s0, attend, lw):
    (wm, wa, wg, wa2, ba, gn, wgrp, pscale, wp, wmb, wgl, wo, gpre, gpost, gfpre, gfpost, w1, w2) = lw
    u, q, k, v, qg, kg, vg, rg, lg, sg = _in_proj(x, gpre, wm, wa, wg, wa2, ba, tabs)
    seq = lambda a: a.reshape(Bn, T, a.shape[-1])
    u3 = seq(u)
    yp = _pool(u3, hist16, wgrp, pscale, pos0)
    ym = attend(seq(q), seq(k), seq(v))
    Tp = -(-T // GLA_CHUNK) * GLA_CHUNK
    padt = lambda a: jnp.pad(seq(a), ((0, 0), (0, Tp - T), (0, 0))) if Tp != T else seq(a)
    yg, s_new = _gla(padt(qg), padt(kg), padt(vg), padt(rg), padt(lg), s0, gn)
    yg = yg[:, :T].reshape(Bn * T, GLA_VW)
    x = _merge_ffn(x, yp.reshape(Bn * T, POOL_W), ym.reshape(Bn * T, MB_W), yg, sg,
                   wp, wmb, wgl, wo, gpost, gfpre, gfpost, w1, w2)
    new_hist = jnp.concatenate([hist16[:, 1:], u3], axis=1)[:, -POOL_HIST:]
    kv = lambda a: a.reshape(Bn, T, MB_HEADS, MB_HD)
    return x, kv(k), kv(v), new_hist, s_new


def kernel(x_prompt, x_sample, cache_k, cache_v, state_pool, state_gla, page_table, w_in, w_gla_a2, b_gla_a,
           gla_norm, w_pool_grp, pool_scale, w_pool_out, w_mb_out, w_gla_out, w_o, norm_mix_pre,
           norm_mix_post, norm_ffn_pre, norm_ffn_post, w_ff1, w_ff2):
    B, T, D = x_prompt.shape
    DB, TS, _ = x_sample.shape
    depth = w_in.shape[0]
    n_pages = page_table.shape[1]
    past_len = n_pages * PAGE_SIZE
    assert D == D_MODEL and T % MB_BLOCK == 0 and past_len % MB_BLOCK == 0
    assert past_len // MB_BLOCK >= MB_TOPK and (B * T) % ROW_TILE == 0

    tabs_p = _rope_tables(jnp.arange(T))
    reps = max(1, min(ROW_TILE, DB * TS) // TS)
    tabs_s = tuple(jnp.tile(t, (reps, 1)) for t in _rope_tables(past_len + jnp.arange(TS)))
    ck = cache_k.reshape(depth, -1, PAGE_SIZE, MB_W)
    cv = cache_v.reshape(depth, -1, PAGE_SIZE, MB_W)
    zrow = lambda n: jnp.zeros((n, 1, POOL_W), F32)
    hist_p = jnp.zeros((B, 16, POOL_W), F32)
    s0_p = jnp.zeros((B, GLA_HEADS, GLA_DK, GLA_DV), F32)

    hp = x_prompt.reshape(B * T, D)
    hs = x_sample.reshape(DB * TS, D)
    outs_p, outs_s = [], []
    for l in range(depth):
        bf = lambda a: a.astype(BF16)
        vec = lambda a: a.reshape(1, -1)
        lw = (bf(w_in[l][:, :W_MAIN]), bf(w_in[l][:, W_MAIN:W_MAIN + GLA_RANK]), bf(w_in[l][:, W_MAIN + GLA_RANK:]),
              bf(w_gla_a2[l]), vec(b_gla_a[l]), vec(gla_norm[l]), bf(w_pool_grp[l]), vec(pool_scale[l]),
              bf(w_pool_out[l]), bf(w_mb_out[l]), bf(w_gla_out[l]), bf(w_o[l]),
              vec(norm_mix_pre[l]), vec(norm_mix_post[l]), vec(norm_ffn_pre[l]), vec(norm_ffn_post[l]),
              bf(w_ff1[l]), bf(w_ff2[l]))
        hp, *op = _layer(hp, B, T, tabs_p, hist_p, 0, s0_p, _moba_prompt, lw)
        hist_s = jnp.concatenate([zrow(DB), state_pool[l]], axis=1)
        attend_s = functools.partial(_moba_sample, cache_k=ck, cache_v=cv, page_table=page_table, layer=l)
        hs, *os_ = _layer(hs, DB, TS, tabs_s, hist_s, past_len, state_gla[l], attend_s, lw)
        outs_p.append(op)
        outs_s.append(os_)
    stack = lambda outs, i: jnp.stack([o[i] for o in outs])
    return (hp.reshape(B, T, D), hs.reshape(DB, TS, D),
            stack(outs_p, 0), stack(outs_p, 1), stack(outs_p, 2), stack(outs_p, 3),
            stack(outs_s, 0), stack(outs_s, 1), stack(outs_s, 2), stack(outs_s, 3))
```

```python
import functools

import jax
import jax.numpy as jnp
from jax import lax
from jax.experimental import pallas as pl
from jax.experimental.pallas import tpu as pltpu

F32 = jnp.float32
BF16 = jnp.bfloat16

D_MODEL = 1024
PAGE_SIZE = 128
POOL_GC = 128
POOL_W = 512
POOL_WINDOWS = (2, 4, 8, 16)
POOL_HIST = 15
MB_HEADS = 8
MB_HD = 64
MB_W = 512
MB_BLOCK = 256
MB_TOPK = 3
ROPE_THETA = 500000.0
ROPE_DIM = 16
GLA_HEADS = 4
GLA_DK = 64
GLA_DV = 128
GLA_KW = 256
GLA_VW = 512
GLA_RANK = 16
GLA_NORMALIZER = 16.0
GLA_CHUNK = 64
D_FF = 4096
EPS = 1e-6
W_MAIN = POOL_W + 3 * MB_W + 2 * GLA_KW + 2 * GLA_VW
NEG_INF = float("-inf")

VMEM_LIMIT_BYTES = 56 * 1024 * 1024
ROW_TILE = 256


def _cparams(sem):
    return pltpu.CompilerParams(dimension_semantics=sem, vmem_limit_bytes=VMEM_LIMIT_BYTES)


def _const_spec(shape):
    nd = len(shape)
    return pl.BlockSpec(shape, lambda *_: (0,) * nd, pipeline_mode=pl.Buffered(1))


def _rms(x, g):
    ms = jnp.mean(x * x, axis=-1, keepdims=True)
    return x * lax.rsqrt(ms + EPS) * g


def _sigmoid(x):
    return 1.0 / (1.0 + jnp.exp(-x))


def _split3(x):
    hi = x.astype(BF16)
    r = x - hi.astype(F32)
    mid = r.astype(BF16)
    lo = (r - mid.astype(F32)).astype(BF16)
    return hi, mid, lo


def _in_proj_kernel(x_ref, gpre_ref, wm_ref, wa_ref, wg_ref, wa2_ref, ba_ref,
                    cos_ref, sa_ref, sb_ref,
                    u_ref, q_ref, k_ref, v_ref, qg_ref, kg_ref, vg_ref, rg_ref, lg_ref, sg_ref,
                    *attn_refs):
    h = _rms(x_ref[...], gpre_ref[...]).astype(BF16)

    def proj(w_ref, a, b):
        return jnp.dot(h, w_ref[:, a:b], preferred_element_type=F32)

    cos, sa, sb = cos_ref[...], sa_ref[...], sb_ref[...]

    def rope_store(col0, *o_refs):
        col_sums = []
        for j in range(MB_W // 128):
            xj = proj(wm_ref, col0 + 128 * j, col0 + 128 * (j + 1))
            half = ROPE_DIM // 2
            rj = xj * cos + pltpu.roll(xj, 128 - half, 1) * sa + pltpu.roll(xj, half, 1) * sb
            for o_ref in o_refs:
                o_ref[:, 128 * j:128 * (j + 1)] = rj.astype(o_ref.dtype)
            col_sums.append(jnp.sum(rj, axis=0, keepdims=True))
        return col_sums

    u_ref[...] = proj(wm_ref, 0, 512)
    rope_store(512, q_ref)
    v = proj(wm_ref, 1536, 2048)
    v_ref[...] = v
    if attn_refs:
        kb_ref, vt_ref, km_ref = attn_refs
        col_sums = rope_store(1024, k_ref, kb_ref)
        vt_ref[0] = v.T.astype(BF16)
        km_ref[0] = jnp.concatenate(col_sums, axis=1) * (1.0 / MB_BLOCK)
    else:
        rope_store(1024, k_ref)
    qg_ref[...] = proj(wm_ref, 2048, 2304)
    kg_ref[...] = proj(wm_ref, 2304, 2560)
    vg_ref[...] = proj(wm_ref, 2560, 3072)
    r = proj(wm_ref, 3072, 3584)
    rg_ref[...] = r * _sigmoid(r)
    a = jnp.dot(h, wa_ref[...], preferred_element_type=F32)
    xg = jnp.dot(a.astype(BF16), wa2_ref[...], preferred_element_type=F32) + ba_ref[...]
    lg_ref[...] = (jnp.minimum(xg, 0.0) - jnp.log1p(jnp.exp(-jnp.abs(xg)))) * (1.0 / GLA_NORMALIZER)
    for j in range(3):
        sg_ref[:, 1024 * j:1024 * (j + 1)] = _sigmoid(proj(wg_ref, 1024 * j, 1024 * (j + 1)))


def _in_proj(x, gpre, wm, wa, wg, wa2, ba, tabs, seq_len=None):
    M = x.shape[0]
    tm = min(ROW_TILE, M)
    period = tabs[0].shape[0] // tm
    row = lambda w: pl.BlockSpec((tm, w), lambda i: (i, 0))
    tab = pl.BlockSpec((tm, 128), lambda i: (i % period, 0))
    widths = (POOL_W, MB_W, MB_W, MB_W, GLA_KW, GLA_KW, GLA_VW, GLA_VW, GLA_KW, 3 * D_MODEL)
    out_specs = [row(w) for w in widths]
    out_shape = [jax.ShapeDtypeStruct((M, w), F32) for w in widths]
    if seq_len is not None:
        assert tm == MB_BLOCK and seq_len % tm == 0
        tps = seq_len // tm
        out_specs += [row(MB_W),
                      pl.BlockSpec((1, MB_W, tm), lambda i: (i // tps, 0, i % tps)),
                      pl.BlockSpec((1, 1, MB_W), lambda i: (i, 0, 0))]
        out_shape += [jax.ShapeDtypeStruct((M, MB_W), BF16),
                      jax.ShapeDtypeStruct((M // seq_len, MB_W, seq_len), BF16),
                      jax.ShapeDtypeStruct((M // tm, 1, MB_W), F32)]
    return pl.pallas_call(
        _in_proj_kernel,
        grid=(M // tm,),
        in_specs=[row(D_MODEL), _const_spec((1, D_MODEL)), _const_spec(wm.shape), _const_spec(wa.shape),
                  _const_spec(wg.shape), _const_spec(wa2.shape), _const_spec((1, GLA_KW)), tab, tab, tab],
        out_specs=out_specs,
        out_shape=out_shape,
        compiler_params=_cparams(("arbitrary",)),
        name="in_proj",
    )(x, gpre, wm, wa, wg, wa2, ba, *tabs)


def _rope_tables(pos):
    half = ROPE_DIM // 2
    inv = ROPE_THETA ** (-jnp.arange(half, dtype=F32) / half)
    ang = pos.astype(F32)[:, None] * inv[None, :]
    cos, sin = jnp.cos(ang), jnp.sin(ang)
    n = pos.shape[0]
    zeros = jnp.zeros((n, MB_HD - ROPE_DIM), F32)
    zh = jnp.zeros((n, half), F32)
    c = jnp.concatenate([cos, cos, zeros + 1.0], axis=1)
    sa = jnp.concatenate([-sin, zh, zeros], axis=1)
    sb = jnp.concatenate([zh, sin, zeros], axis=1)
    two = lambda t: jnp.concatenate([t, t], axis=1)
    return two(c), two(sa), two(sb)


def _pool_kernel(u_ref, hist_ref, wgrp_ref, scale_ref, y_ref, ext_ref, *, pos0, T, R):
    ext_ref[0:16, :] = hist_ref[0]
    ext_ref[16:16 + T, :] = u_ref[0]
    for c in range(T // R):
        r0 = 16 + c * R
        posp1 = lax.broadcasted_iota(jnp.int32, (R, 1), 0) + (pos0 + c * R + 1)
        for g, w in enumerate(POOL_WINDOWS):
            lanes = slice(g * POOL_GC, (g + 1) * POOL_GC)
            cur = ext_ref[r0:r0 + R, lanes]
            acc = cur
            for j in range(1, w):
                acc = acc + ext_ref[r0 - j:r0 - j + R, lanes]
            cnt = jnp.minimum(posp1, w).astype(F32)
            d = acc / cnt - cur
            yg = jnp.dot(d.astype(BF16), wgrp_ref[g], preferred_element_type=F32)
            y_ref[0, c * R:(c + 1) * R, lanes] = yg * scale_ref[:, lanes]


def _pool(u, hist16, wgrp, scale, pos0):
    Bn, T, _ = u.shape
    R = min(T, 256)
    seq = pl.BlockSpec((1, T, POOL_W), lambda b: (b, 0, 0))
    return pl.pallas_call(
        functools.partial(_pool_kernel, pos0=pos0, T=T, R=R),
        grid=(Bn,),
        in_specs=[seq, pl.BlockSpec((1, 16, POOL_W), lambda b: (b, 0, 0)),
                  _const_spec(wgrp.shape), _const_spec((1, POOL_W))],
        out_specs=seq,
        out_shape=jax.ShapeDtypeStruct((Bn, T, POOL_W), F32),
        scratch_shapes=[pltpu.VMEM((16 + T, POOL_W), F32)],
        compiler_params=_cparams(("arbitrary",)),
        name="pool",
    )(u, hist16, wgrp, scale)


def _top_blocks(gate, n_valid, axis):
    n_blocks = gate.shape[axis]
    iota = lax.broadcasted_iota(jnp.int32, gate.shape, axis)
    g = jnp.where(iota < n_valid, gate, NEG_INF)
    sel = jnp.zeros(gate.shape, F32)
    for _ in range(MB_TOPK):
        m = jnp.max(g, axis=axis, keepdims=True)
        idx = jnp.min(jnp.where(g == m, iota, n_blocks), axis=axis, keepdims=True)
        pick = (iota == idx) & (m > NEG_INF)
        sel = jnp.where(pick, 1.0, sel)
        g = jnp.where(pick, NEG_INF, g)
    return sel


def _dot_nt(a, b, precision=None):
    return lax.dot_general(a, b, (((1,), (1,)), ((), ())), preferred_element_type=F32, precision=precision)


def _moba_prompt_kernel(q_ref, km_ref, k_ref, vt_ref, y_ref):
    qi = pl.program_id(2)
    B = MB_BLOCK
    n_blocks = km_ref.shape[1]
    q = q_ref[0]
    qlane = lax.broadcasted_iota(jnp.int32, (B, 128), 1) // MB_HD
    kmlane = lax.broadcasted_iota(jnp.int32, (n_blocks, 128), 1) // MB_HD
    krow = lax.broadcasted_iota(jnp.int32, (B, B), 0)
    qcol = lax.broadcasted_iota(jnp.int32, (B, B), 1)
    blk = lax.broadcasted_iota(jnp.int32, (n_blocks, B), 0)
    own0 = pl.multiple_of(qi * B, B)

    qs, sel, init = [], [], []
    for hh in range(2):
        qh = jnp.where(qlane == hh, q, 0.0)
        gate_t = _dot_nt(jnp.where(kmlane == hh, km_ref[0], 0.0), qh, precision=lax.Precision.HIGHEST)
        sel.append(_top_blocks(gate_t, qi, axis=0))
        qs.append((qh * (MB_HD ** -0.5)).astype(BF16))
        s = _dot_nt(k_ref[0, pl.ds(own0, B), :], qs[hh])
        s = jnp.where(krow <= qcol, s, NEG_INF)
        m0 = jnp.max(s, axis=0, keepdims=True)
        p = jnp.exp(s - m0)
        l0 = jnp.sum(p, axis=0, keepdims=True)
        acc0 = jnp.dot(vt_ref[0, :, pl.ds(own0, B)], p.astype(BF16), preferred_element_type=F32)
        init += [m0, l0, acc0]

    def body(j, carry):
        j0 = pl.multiple_of(j * B, B)
        kj = k_ref[0, pl.ds(j0, B), :]
        vtj = vt_ref[0, :, pl.ds(j0, B)]
        out = []
        for hh in range(2):
            m, l, acc = carry[3 * hh:3 * hh + 3]
            chosen = jnp.max(jnp.where(blk == j, sel[hh], 0.0), axis=0, keepdims=True) > 0.0
            sj = jnp.where(chosen, _dot_nt(kj, qs[hh]), NEG_INF)
            m_new = jnp.maximum(m, jnp.max(sj, axis=0, keepdims=True))
            alpha = jnp.exp(m - m_new)
            pj = jnp.exp(sj - m_new)
            l_new = alpha * l + jnp.sum(pj, axis=0, keepdims=True)
            acc_new = alpha * acc + jnp.dot(vtj, pj.astype(BF16), preferred_element_type=F32)
            out += [m_new, l_new, acc_new]
        return tuple(out)

    res = lax.fori_loop(0, qi, body, tuple(init))
    o0 = res[2] / res[1]
    o1 = res[5] / res[4]
    drow = lax.broadcasted_iota(jnp.int32, (128, B), 0) // MB_HD
    y_ref[0] = jnp.where(drow == 0, o0, o1).T


def _moba_prompt(q, km, kb, vt):
    Bn, T, _ = q.shape
    n_blocks = T // MB_BLOCK
    qspec = pl.BlockSpec((1, MB_BLOCK, 128), lambda b, hp, qi: (b, qi, hp))
    return pl.pallas_call(
        _moba_prompt_kernel,
        grid=(Bn, MB_W // 128, n_blocks),
        in_specs=[qspec,
                  pl.BlockSpec((1, n_blocks, 128), lambda b, hp, qi: (b, 0, hp)),
                  pl.BlockSpec((1, T, 128), lambda b, hp, qi: (b, 0, hp)),
                  pl.BlockSpec((1, 128, T), lambda b, hp, qi: (b, hp, 0))],
        out_specs=qspec,
        out_shape=jax.ShapeDtypeStruct((Bn, T, MB_W), F32),
        compiler_params=_cparams(("arbitrary", "arbitrary", "arbitrary")),
        name="moba_prompt",
    )(q, km, kb, vt)


def _head_rows(x, ts):
    return jnp.concatenate([x[:, h * MB_HD:(h + 1) * MB_HD] for h in range(MB_HEADS)], axis=0)


def _moba_sample_kernel(pt_ref, q_ref, kn_ref, vn_ref, ka_ref, kb_ref, va_ref, vb_ref, y_ref,
                        q2_ref, bias_ref, oblk_ref, m_ref, l_ref, g_ref, *, n_blocks, ts):
    n = pl.program_id(1)
    R = MB_HEADS * ts
    W = PAGE_SIZE * MB_HEADS
    blane = lax.broadcasted_iota(jnp.int32, (R, n_blocks), 1)

    @pl.when(n == 0)
    def _():
        for ref in (m_ref, l_ref, g_ref):
            ref[...] = jnp.zeros(ref.shape, F32)
        q2_ref[...] = _head_rows(q_ref[0], ts)
        rh = lax.broadcasted_iota(jnp.int32, (R, W), 0) // ts
        lh = lax.broadcasted_iota(jnp.int32, (R, W), 1) % MB_HEADS
        bias_ref[...] = jnp.where(rh == lh, 0.0, NEG_INF)

    q2 = q2_ref[...]
    q2b = (q2 * (MB_HD ** -0.5)).astype(BF16)
    bias = bias_ref[...]

    def scores(k_ref):
        kp = k_ref[0, 0]
        s = _dot_nt(q2b, kp.reshape(W, MB_HD).astype(BF16)) + bias
        return s, jnp.sum(kp, axis=0)

    sa, ksa = scores(ka_ref)
    sb, ksb = scores(kb_ref)
    m_n = jnp.maximum(jnp.max(sa, axis=1, keepdims=True), jnp.max(sb, axis=1, keepdims=True))
    pa = jnp.exp(sa - m_n)
    pb = jnp.exp(sb - m_n)
    l_n = jnp.sum(pa, axis=1, keepdims=True) + jnp.sum(pb, axis=1, keepdims=True)
    oblk_ref[n] = (jnp.dot(pa.astype(BF16), va_ref[0, 0].reshape(W, MB_HD).astype(BF16), preferred_element_type=F32)
                   + jnp.dot(pb.astype(BF16), vb_ref[0, 0].reshape(W, MB_HD).astype(BF16), preferred_element_type=F32))
    m_ref[...] = jnp.where(blane == n, m_n, m_ref[...])
    l_ref[...] = jnp.where(blane == n, l_n, l_ref[...])
    kmean = (ksa + ksb) * (1.0 / MB_BLOCK)
    g_all = _dot_nt(q2, kmean, precision=lax.Precision.HIGHEST)
    own = (lax.broadcasted_iota(jnp.int32, (R, MB_HEADS), 0) // ts
           == lax.broadcasted_iota(jnp.int32, (R, MB_HEADS), 1))
    g_n = jnp.sum(jnp.where(own, g_all, 0.0), axis=1, keepdims=True)
    g_ref[...] = jnp.where(blane == n, g_n, g_ref[...])

    @pl.when(n == n_blocks - 1)
    def _():
        sel = _top_blocks(g_ref[...], n_blocks, axis=1) > 0.0
        kn2 = _head_rows(kn_ref[0], ts).astype(BF16)
        vn2 = _head_rows(vn_ref[0], ts).astype(BF16)
        rr = lax.broadcasted_iota(jnp.int32, (R, R), 0)
        cc = lax.broadcasted_iota(jnp.int32, (R, R), 1)
        s_own = jnp.where((rr // ts == cc // ts) & (cc % ts <= rr % ts), _dot_nt(q2b, kn2), NEG_INF)
        mb = jnp.where(sel, m_ref[...], NEG_INF)
        m_all = jnp.maximum(jnp.max(s_own, axis=1, keepdims=True), jnp.max(mb, axis=1, keepdims=True))
        w = jnp.exp(mb - m_all)
        p_own = jnp.exp(s_own - m_all)
        l = jnp.sum(p_own, axis=1, keepdims=True) + jnp.sum(w * l_ref[...], axis=1, keepdims=True)
        o = jnp.dot(p_own.astype(BF16), vn2, preferred_element_type=F32)
        for nb in range(n_blocks):
            o = o + w[:, nb:nb + 1] * oblk_ref[nb]
        o = o / l
        y_ref[0] = jnp.concatenate([o[h * ts:(h + 1) * ts] for h in range(MB_HEADS)], axis=1)


def _moba_sample(q, kn, vn, cache_k, cache_v, page_table, layer):
    DB, ts, _ = q.shape
    n_pages = page_table.shape[1]
    ppb = MB_BLOCK // PAGE_SIZE
    n_blocks = n_pages // ppb
    R = MB_HEADS * ts
    tok = pl.BlockSpec((1, ts, MB_W), lambda b, n, pt: (b, 0, 0))

    def page(i):
        return pl.BlockSpec((1, 1, PAGE_SIZE, MB_HEADS, MB_HD),
                            lambda b, n, pt: (layer, pt[b * n_pages + n * ppb + i], 0, 0, 0))

    grid_spec = pltpu.PrefetchScalarGridSpec(
        num_scalar_prefetch=1,
        grid=(DB, n_blocks),
        in_specs=[tok, tok, tok, page(0), page(1), page(0), page(1)],
        out_specs=tok,
        scratch_shapes=[pltpu.VMEM((R, MB_HD), F32),
                        pltpu.VMEM((R, PAGE_SIZE * MB_HEADS), F32),
                        pltpu.VMEM((n_blocks, R, MB_HD), F32),
                        pltpu.VMEM((R, n_blocks), F32),
                        pltpu.VMEM((R, n_blocks), F32),
                        pltpu.VMEM((R, n_blocks), F32)],
    )
    return pl.pallas_call(
        functools.partial(_moba_sample_kernel, n_blocks=n_blocks, ts=ts),
        grid_spec=grid_spec,
        out_shape=jax.ShapeDtypeStruct((DB, ts, MB_W), F32),
        compiler_params=_cparams(("arbitrary", "arbitrary")),
        name="moba_sample",
    )(page_table.reshape(-1), q, kn, vn, cache_k, cache_k, cache_v, cache_v)


def _gla_kernel(qg_ref, kg_ref, vg_ref, rg_ref, lg_ref, s0_ref, gn_ref, y_ref, sfin_ref, st_ref, *, n_chunks):
    t = pl.program_id(1)
    C = GLA_CHUNK

    @pl.when(t == 0)
    def _():
        for h in range(GLA_HEADS):
            st_ref[:, h * GLA_DK:(h + 1) * GLA_DK] = s0_ref[0, h].T

    ri = lax.broadcasted_iota(jnp.int32, (C, C), 0)
    ci = lax.broadcasted_iota(jnp.int32, (C, C), 1)
    tri = (ci <= ri).astype(BF16)
    m1 = (ri >= 32) & (ci < 32)
    m2 = (ri // 32 == ci // 32) & (ri % 32 >= 16) & (ci % 32 < 16)
    m3 = (ri // 16 == ci // 16) & (ci <= ri)
    grp = lax.broadcasted_iota(jnp.int32, (C, GLA_KW), 0)

    def chunk(c, carry):
        r0 = pl.multiple_of(c * C, C)
        rows = pl.ds(r0, C)
        lg = lg_ref[0, rows, :]
        hi, mid, lo = _split3(lg)
        b = (jnp.dot(tri, hi, preferred_element_type=F32) + jnp.dot(tri, mid, preferred_element_type=F32)
             + jnp.dot(tri, lo, preferred_element_type=F32))
        q = qg_ref[0, rows, :] * (GLA_DK ** -0.5)
        k = kg_ref[0, rows, :]
        brow = lambda i: jnp.broadcast_to(b[i:i + 1, :], (C, GLA_KW))
        ref1 = brow(31)
        ref2 = jnp.where(grp < 32, brow(15), brow(47))
        ref3 = jnp.where(grp < 16, brow(7), jnp.where(grp < 32, brow(23), jnp.where(grp < 48, brow(39), brow(55))))
        b_last = b[C - 1:C, :]
        qs = (q * jnp.exp(b)).astype(BF16)
        ks = (k * jnp.exp(b_last - b)).astype(BF16)
        q1 = (q * jnp.exp(jnp.minimum(b - ref1, 0.0))).astype(BF16)
        k1 = (k * jnp.exp(jnp.minimum(ref1 - b, 0.0))).astype(BF16)
        q2 = (q * jnp.exp(jnp.minimum(b - ref2, 0.0))).astype(BF16)
        k2 = (k * jnp.exp(jnp.minimum(ref2 - b, 0.0))).astype(BF16)
        q3 = (q * jnp.exp(b - ref3)).astype(BF16)
        k3 = (k * jnp.exp(ref3 - b)).astype(BF16)
        st_old = st_ref[...]
        st_ref[...] = st_old * jnp.exp(b_last)
        for h in range(GLA_HEADS):
            kl = slice(h * GLA_DK, (h + 1) * GLA_DK)
            vl = slice(h * GLA_DV, (h + 1) * GLA_DV)
            a = (jnp.where(m1, _dot_nt(q1[:, kl], k1[:, kl]), 0.0)
                 + jnp.where(m2, _dot_nt(q2[:, kl], k2[:, kl]), 0.0)
                 + jnp.where(m3, _dot_nt(q3[:, kl], k3[:, kl]), 0.0))
            vf = vg_ref[0, rows, vl]
            o = (_dot_nt(qs[:, kl], st_old[:, kl].astype(BF16))
                 + jnp.dot(a.astype(BF16), vf.astype(BF16), preferred_element_type=F32))
            st_ref[:, kl] += jnp.dot(vf.T.astype(BF16), ks[:, kl], preferred_element_type=F32)
            y_ref[0, rows, vl] = _rms(o, gn_ref[:, vl]) * rg_ref[0, rows, vl]
        return carry

    lax.fori_loop(0, n_chunks, chunk, 0)

    @pl.when(t == pl.num_programs(1) - 1)
    def _():
        for h in range(GLA_HEADS):
            sfin_ref[0, h] = st_ref[:, h * GLA_DK:(h + 1) * GLA_DK].T


def _gla(qg, kg, vg, rg, lg, s0, gn):
    Bn, T, _ = qg.shape
    TT = min(T, 512)
    kspec = pl.BlockSpec((1, TT, GLA_KW), lambda b, t: (b, t, 0))
    vspec = pl.BlockSpec((1, TT, GLA_VW), lambda b, t: (b, t, 0))
    sspec = pl.BlockSpec((1, GLA_HEADS, GLA_DK, GLA_DV), lambda b, t: (b, 0, 0, 0))
    return pl.pallas_call(
        functools.partial(_gla_kernel, n_chunks=TT // GLA_CHUNK),
        grid=(Bn, T // TT),
        in_specs=[kspec, kspec, vspec, vspec, kspec, sspec, _const_spec((1, GLA_VW))],
        out_specs=[vspec, sspec],
        out_shape=[jax.ShapeDtypeStruct((Bn, T, GLA_VW), F32),
                   jax.ShapeDtypeStruct((Bn, GLA_HEADS, GLA_DK, GLA_DV), F32)],
        scratch_shapes=[pltpu.VMEM((GLA_DV, GLA_KW), F32)],
        compiler_params=_cparams(("arbitrary", "arbitrary")),
        name="gla",
    )(qg, kg, vg, rg, lg, s0, gn)


def _merge_ffn_kernel(x_ref, yp_ref, ym_ref, yg_ref, sg_ref, wp_ref, wmb_ref, wgl_ref, wo_ref,
                      gpost_ref, gfpre_ref, gfpost_ref, w1_ref, w2_ref, o_ref):
    def br(y_ref, w_ref, j):
        return sg_ref[:, 1024 * j:1024 * (j + 1)] * jnp.dot(y_ref[...].astype(BF16), w_ref[...],
                                                            preferred_element_type=F32)

    merged = br(yp_ref, wp_ref, 0) + br(ym_ref, wmb_ref, 1) + br(yg_ref, wgl_ref, 2)
    z = jnp.dot(merged.astype(BF16), wo_ref[...], preferred_element_type=F32)
    x1 = x_ref[...] + _rms(z, gpost_ref[...])
    hb = _rms(x1, gfpre_ref[...]).astype(BF16)
    f = jnp.zeros(x1.shape, F32)
    FC = 1024
    for c in range(D_FF // FC):
        a = jnp.maximum(jnp.dot(hb, w1_ref[:, c * FC:(c + 1) * FC], preferred_element_type=F32), 0.0)
        f = f + jnp.dot((a * a).astype(BF16), w2_ref[c * FC:(c + 1) * FC, :], preferred_element_type=F32)
    o_ref[...] = x1 + _rms(f, gfpost_ref[...])


def _merge_ffn(x, yp, ym, yg, sg, wp, wmb, wgl, wo, gpost, gfpre, gfpost, w1, w2):
    M = x.shape[0]
    tm = min(ROW_TILE, M)
    row = lambda w: pl.BlockSpec((tm, w), lambda i: (i, 0))
    vec = _const_spec((1, D_MODEL))
    return pl.pallas_call(
        _merge_ffn_kernel,
        grid=(M // tm,),
        in_specs=[row(D_MODEL), row(POOL_W), row(MB_W), row(GLA_VW), row(3 * D_MODEL),
                  _const_spec(wp.shape), _const_spec(wmb.shape), _const_spec(wgl.shape), _const_spec(wo.shape),
                  vec, vec, vec, _const_spec(w1.shape), _const_spec(w2.shape)],
        out_specs=row(D_MODEL),
        out_shape=jax.ShapeDtypeStruct((M, D_MODEL), F32),
        compiler_params=_cparams(("arbitrary",)),
        name="merge_ffn",
    )(x, yp, ym, yg, sg, wp, wmb, wgl, wo, gpost, gfpre, gfpost, w1, w2)


def _layer(x, Bn, T, tabs, hist16, pos0, s0, sample_attend, lw):
    (wm, wa, wg, wa2, ba, gn, wgrp, pscale, wp, wmb, wgl, wo, gpre, gpost, gfpre, gfpost, w1, w2) = lw
    seq = lambda a: a.reshape(Bn, T, a.shape[-1])
    if sample_attend is None:
        u, q, k, v, qg, kg, vg, rg, lg, sg, kb, vt, km = _in_proj(x, gpre, wm, wa, wg, wa2, ba, tabs, seq_len=T)
        ym = _moba_prompt(seq(q), km.reshape(Bn, T // MB_BLOCK, MB_W), seq(kb), vt)
    else:
        u, q, k, v, qg, kg, vg, rg, lg, sg = _in_proj(x, gpre, wm, wa, wg, wa2, ba, tabs)
        ym = sample_attend(seq(q), seq(k), seq(v))
    u3 = seq(u)
    yp = _pool(u3, hist16, wgrp, pscale, pos0)
    Tp = -(-T // GLA_CHUNK) * GLA_CHUNK
    padt = lambda a: jnp.pad(seq(a), ((0, 0), (0, Tp - T), (0, 0))) if Tp != T else seq(a)
    yg, s_new = _gla(padt(qg), padt(kg), padt(vg), padt(rg), padt(lg), s0, gn)
    yg = yg[:, :T].reshape(Bn * T, GLA_VW)
    x = _merge_ffn(x, yp.reshape(Bn * T, POOL_W), ym.reshape(Bn * T, MB_W), yg, sg,
                   wp, wmb, wgl, wo, gpost, gfpre, gfpost, w1, w2)
    new_hist = jnp.concatenate([hist16[:, 1:], u3], axis=1)[:, -POOL_HIST:]
    kv = lambda a: a.reshape(Bn, T, MB_HEADS, MB_HD)
    return x, kv(k), kv(v), new_hist, s_new


def kernel(x_prompt, x_sample, cache_k, cache_v, state_pool, state_gla, page_table, w_in, w_gla_a2, b_gla_a,
           gla_norm, w_pool_grp, pool_scale, w_pool_out, w_mb_out, w_gla_out, w_o, norm_mix_pre,
           norm_mix_post, norm_ffn_pre, norm_ffn_post, w_ff1, w_ff2):
    B, T, D = x_prompt.shape
    DB, TS, _ = x_sample.shape
    depth = w_in.shape[0]
    n_pages = page_table.shape[1]
    past_len = n_pages * PAGE_SIZE
    assert D == D_MODEL and T % MB_BLOCK == 0 and past_len % MB_BLOCK == 0
    assert past_len // MB_BLOCK >= MB_TOPK and (B * T) % ROW_TILE == 0 and MB_BLOCK == 2 * PAGE_SIZE

    tabs_p = _rope_tables(jnp.arange(T))
    reps = max(1, min(ROW_TILE, DB * TS) // TS)
    tabs_s = tuple(jnp.tile(t, (reps, 1)) for t in _rope_tables(past_len + jnp.arange(TS)))
    zrow = lambda n: jnp.zeros((n, 1, POOL_W), F32)
    hist_p = jnp.zeros((B, 16, POOL_W), F32)
    s0_p = jnp.zeros((B, GLA_HEADS, GLA_DK, GLA_DV), F32)

    hp = x_prompt.reshape(B * T, D)
    hs = x_sample.reshape(DB * TS, D)
    outs_p, outs_s = [], []
    for l in range(depth):
        bf = lambda a: a.astype(BF16)
        vec = lambda a: a.reshape(1, -1)
        lw = (bf(w_in[l][:, :W_MAIN]), bf(w_in[l][:, W_MAIN:W_MAIN + GLA_RANK]), bf(w_in[l][:, W_MAIN + GLA_RANK:]),
              bf(w_gla_a2[l]), vec(b_gla_a[l]), vec(gla_norm[l]), bf(w_pool_grp[l]), vec(pool_scale[l]),
              bf(w_pool_out[l]), bf(w_mb_out[l]), bf(w_gla_out[l]), bf(w_o[l]),
              vec(norm_mix_pre[l]), vec(norm_mix_post[l]), vec(norm_ffn_pre[l]), vec(norm_ffn_post[l]),
              bf(w_ff1[l]), bf(w_ff2[l]))
        hp, *op = _layer(hp, B, T, tabs_p, hist_p, 0, s0_p, None, lw)
        hist_s = jnp.concatenate([zrow(DB), state_pool[l]], axis=1)
        attend_s = functools.partial(_moba_sample, cache_k=cache_k, cache_v=cache_v, page_table=page_table, layer=l)
        hs, *os_ = _layer(hs, DB, TS, tabs_s, hist_s, past_len, state_gla[l], attend_s, lw)
        outs_p.append(op)
        outs_s.append(os_)
    stack = lambda outs, i: jnp.stack([o[i] for o in outs])
    return (hp.reshape(B, T, D), hs.reshape(DB, TS, D),
            stack(outs_p, 0), stack(outs_p, 1), stack(outs_p, 2), stack(outs_p, 3),
            stack(outs_s, 0), stack(outs_s, 1), stack(outs_s, 2), stack(outs_s, 3))
```

```python
import functools

import jax
import jax.numpy as jnp
from jax import lax
from jax.experimental import pallas as pl
from jax.experimental.pallas import tpu as pltpu

F32 = jnp.float32
BF16 = jnp.bfloat16

D_MODEL = 1024
PAGE_SIZE = 128
POOL_GC = 128
POOL_W = 512
POOL_WINDOWS = (2, 4, 8, 16)
POOL_HIST = 15
MB_HEADS = 8
MB_HD = 64
MB_W = 512
MB_BLOCK = 256
MB_TOPK = 3
ROPE_THETA = 500000.0
ROPE_DIM = 16
GLA_HEADS = 4
GLA_DK = 64
GLA_DV = 128
GLA_KW = 256
GLA_VW = 512
GLA_RANK = 16
GLA_NORMALIZER = 16.0
GLA_CHUNK = 64
D_FF = 4096
EPS = 1e-6
W_MAIN = POOL_W + 3 * MB_W + 2 * GLA_KW + 2 * GLA_VW
NEG_INF = float("-inf")

VMEM_LIMIT_BYTES = 56 * 1024 * 1024
ROW_TILE = 256


def _cparams(sem):
    return pltpu.CompilerParams(dimension_semantics=sem, vmem_limit_bytes=VMEM_LIMIT_BYTES)


def _const_spec(shape):
    nd = len(shape)
    return pl.BlockSpec(shape, lambda *_: (0,) * nd, pipeline_mode=pl.Buffered(1))


def _rms(x, g):
    ms = jnp.mean(x * x, axis=-1, keepdims=True)
    return x * lax.rsqrt(ms + EPS) * g


def _sigmoid(x):
    return 1.0 / (1.0 + jnp.exp(-x))


def _split3(x):
    hi = x.astype(BF16)
    r = x - hi.astype(F32)
    mid = r.astype(BF16)
    lo = (r - mid.astype(F32)).astype(BF16)
    return hi, mid, lo


def _in_proj_kernel(x_ref, gpre_ref, wm_ref, wa_ref, wg_ref, wa2_ref, ba_ref,
                    cos_ref, sa_ref, sb_ref,
                    u_ref, q_ref, k_ref, v_ref, qg_ref, kg_ref, vg_ref, rg_ref, lg_ref, sg_ref,
                    *attn_refs):
    h = _rms(x_ref[...], gpre_ref[...]).astype(BF16)

    def proj(w_ref, a, b):
        return jnp.dot(h, w_ref[:, a:b], preferred_element_type=F32)

    cos, sa, sb = cos_ref[...], sa_ref[...], sb_ref[...]

    def rope_store(col0, *o_refs):
        col_sums = []
        for j in range(MB_W // 128):
            xj = proj(wm_ref, col0 + 128 * j, col0 + 128 * (j + 1))
            half = ROPE_DIM // 2
            rj = xj * cos + pltpu.roll(xj, 128 - half, 1) * sa + pltpu.roll(xj, half, 1) * sb
            for o_ref in o_refs:
                o_ref[:, 128 * j:128 * (j + 1)] = rj.astype(o_ref.dtype)
            col_sums.append(jnp.sum(rj, axis=0, keepdims=True))
        return col_sums

    u_ref[...] = proj(wm_ref, 0, 512)
    rope_store(512, q_ref)
    v = proj(wm_ref, 1536, 2048)
    v_ref[...] = v
    if attn_refs:
        kb_ref, vt_ref, km_ref = attn_refs
        col_sums = rope_store(1024, k_ref, kb_ref)
        vt_ref[0] = v.T.astype(BF16)
        km_ref[0] = jnp.concatenate(col_sums, axis=1) * (1.0 / MB_BLOCK)
    else:
        rope_store(1024, k_ref)
    qg_ref[...] = proj(wm_ref, 2048, 2304)
    kg_ref[...] = proj(wm_ref, 2304, 2560)
    vg_ref[...] = proj(wm_ref, 2560, 3072)
    r = proj(wm_ref, 3072, 3584)
    rg_ref[...] = r * _sigmoid(r)
    a = jnp.dot(h, wa_ref[...], preferred_element_type=F32)
    xg = jnp.dot(a.astype(BF16), wa2_ref[...], preferred_element_type=F32) + ba_ref[...]
    lg_ref[...] = (jnp.minimum(xg, 0.0) - jnp.log1p(jnp.exp(-jnp.abs(xg)))) * (1.0 / GLA_NORMALIZER)
    for j in range(3):
        sg_ref[:, 1024 * j:1024 * (j + 1)] = _sigmoid(proj(wg_ref, 1024 * j, 1024 * (j + 1)))


def _in_proj(x, gpre, wm, wa, wg, wa2, ba, tabs, seq_len=None):
    M = x.shape[0]
    tm = min(ROW_TILE, M)
    period = tabs[0].shape[0] // tm
    row = lambda w: pl.BlockSpec((tm, w), lambda i: (i, 0))
    tab = pl.BlockSpec((tm, 128), lambda i: (i % period, 0))
    widths = (POOL_W, MB_W, MB_W, MB_W, GLA_KW, GLA_KW, GLA_VW, GLA_VW, GLA_KW, 3 * D_MODEL)
    out_specs = [row(w) for w in widths]
    out_shape = [jax.ShapeDtypeStruct((M, w), F32) for w in widths]
    if seq_len is not None:
        assert tm == MB_BLOCK and seq_len % tm == 0
        tps = seq_len // tm
        out_specs += [row(MB_W),
                      pl.BlockSpec((1, MB_W, tm), lambda i: (i // tps, 0, i % tps)),
                      pl.BlockSpec((1, 1, MB_W), lambda i: (i, 0, 0))]
        out_shape += [jax.ShapeDtypeStruct((M, MB_W), BF16),
                      jax.ShapeDtypeStruct((M // seq_len, MB_W, seq_len), BF16),
                      jax.ShapeDtypeStruct((M // tm, 1, MB_W), F32)]
    return pl.pallas_call(
        _in_proj_kernel,
        grid=(M // tm,),
        in_specs=[row(D_MODEL), _const_spec((1, D_MODEL)), _const_spec(wm.shape), _const_spec(wa.shape),
                  _const_spec(wg.shape), _const_spec(wa2.shape), _const_spec((1, GLA_KW)), tab, tab, tab],
        out_specs=out_specs,
        out_shape=out_shape,
        compiler_params=_cparams(("arbitrary",)),
        name="in_proj",
    )(x, gpre, wm, wa, wg, wa2, ba, *tabs)


def _rope_tables(pos):
    half = ROPE_DIM // 2
    inv = ROPE_THETA ** (-jnp.arange(half, dtype=F32) / half)
    ang = pos.astype(F32)[:, None] * inv[None, :]
    cos, sin = jnp.cos(ang), jnp.sin(ang)
    n = pos.shape[0]
    zeros = jnp.zeros((n, MB_HD - ROPE_DIM), F32)
    zh = jnp.zeros((n, half), F32)
    c = jnp.concatenate([cos, cos, zeros + 1.0], axis=1)
    sa = jnp.concatenate([-sin, zh, zeros], axis=1)
    sb = jnp.concatenate([zh, sin, zeros], axis=1)
    two = lambda t: jnp.concatenate([t, t], axis=1)
    return two(c), two(sa), two(sb)


def _pool_kernel(u_ref, hist_ref, wgrp_ref, scale_ref, y_ref, ext_ref, *, pos0, T, R):
    ext_ref[0:16, :] = hist_ref[0]
    ext_ref[16:16 + T, :] = u_ref[0]
    for c in range(T // R):
        r0 = 16 + c * R
        posp1 = lax.broadcasted_iota(jnp.int32, (R, 1), 0) + (pos0 + c * R + 1)
        for g, w in enumerate(POOL_WINDOWS):
            lanes = slice(g * POOL_GC, (g + 1) * POOL_GC)
            cur = ext_ref[r0:r0 + R, lanes]
            acc = cur
            for j in range(1, w):
                acc = acc + ext_ref[r0 - j:r0 - j + R, lanes]
            cnt = jnp.minimum(posp1, w).astype(F32)
            d = acc / cnt - cur
            yg = jnp.dot(d.astype(BF16), wgrp_ref[g], preferred_element_type=F32)
            y_ref[0, c * R:(c + 1) * R, lanes] = yg * scale_ref[:, lanes]


def _pool(u, hist16, wgrp, scale, pos0):
    Bn, T, _ = u.shape
    R = min(T, 256)
    seq = pl.BlockSpec((1, T, POOL_W), lambda b: (b, 0, 0))
    return pl.pallas_call(
        functools.partial(_pool_kernel, pos0=pos0, T=T, R=R),
        grid=(Bn,),
        in_specs=[seq, pl.BlockSpec((1, 16, POOL_W), lambda b: (b, 0, 0)),
                  _const_spec(wgrp.shape), _const_spec((1, POOL_W))],
        out_specs=seq,
        out_shape=jax.ShapeDtypeStruct((Bn, T, POOL_W), F32),
        scratch_shapes=[pltpu.VMEM((16 + T, POOL_W), F32)],
        compiler_params=_cparams(("arbitrary",)),
        name="pool",
    )(u, hist16, wgrp, scale)


def _top_blocks(gate, n_valid, axis):
    n_blocks = gate.shape[axis]
    iota = lax.broadcasted_iota(jnp.int32, gate.shape, axis)
    g = jnp.where(iota < n_valid, gate, NEG_INF)
    sel = jnp.zeros(gate.shape, F32)
    for _ in range(MB_TOPK):
        m = jnp.max(g, axis=axis, keepdims=True)
        idx = jnp.min(jnp.where(g == m, iota, n_blocks), axis=axis, keepdims=True)
        pick = (iota == idx) & (m > NEG_INF)
        sel = jnp.where(pick, 1.0, sel)
        g = jnp.where(pick, NEG_INF, g)
    return sel


def _col_max(s):
    r, c = s.shape
    return jnp.max(jnp.max(s.reshape(r // 8, 8, c), axis=0), axis=0, keepdims=True)


def _dot_nt(a, b, precision=None):
    return lax.dot_general(a, b, (((1,), (1,)), ((), ())), preferred_element_type=F32, precision=precision)


def _moba_prompt_kernel(q_ref, km_ref, k_ref, vt_ref, y_ref, s_ref):
    B = MB_BLOCK
    n_blocks = km_ref.shape[1]
    qlane = lax.broadcasted_iota(jnp.int32, (B, 128), 1) // MB_HD
    kmlane = lax.broadcasted_iota(jnp.int32, (n_blocks, 128), 1) // MB_HD
    causal = lax.broadcasted_iota(jnp.int32, (B, B), 0) <= lax.broadcasted_iota(jnp.int32, (B, B), 1)
    drow = lax.broadcasted_iota(jnp.int32, (128, B), 0) // MB_HD

    for c in range(n_blocks):
        q = q_ref[0, c * B:(c + 1) * B, :]
        outs = []
        for hh in range(2):
            qh = jnp.where(qlane == hh, q, 0.0)
            qs = (qh * (MB_HD ** -0.5)).astype(BF16)
            if c > 0:
                gate_t = _dot_nt(jnp.where(kmlane == hh, km_ref[0], 0.0), qh, precision=lax.Precision.HIGHEST)
                sel = jnp.where(_top_blocks(gate_t, c, axis=0) > 0.0, 0.0, NEG_INF)
            slot = 2 * (c % 2) + hh
            m = None
            for j in range(c + 1):
                s = _dot_nt(k_ref[0, j * B:(j + 1) * B, :], qs)
                s = jnp.where(causal, s, NEG_INF) if j == c else s + sel[j:j + 1, :]
                s_ref[slot, j] = s
                mj = _col_max(s)
                m = mj if m is None else jnp.maximum(m, mj)
            l = jnp.zeros((1, B), F32)
            acc = jnp.zeros((128, B), F32)
            for j in range(c + 1):
                p = jnp.exp(s_ref[slot, j] - m)
                l = l + jnp.sum(p, axis=0, keepdims=True)
                acc = acc + jnp.dot(vt_ref[0, :, j * B:(j + 1) * B], p.astype(BF16),
                                    preferred_element_type=F32)
            outs.append(acc / l)
        y_ref[0, c * B:(c + 1) * B, :] = jnp.where(drow == 0, outs[0], outs[1]).T


def _moba_prompt(q, km, kb, vt):
    Bn, T, _ = q.shape
    n_blocks = T // MB_BLOCK
    seq = pl.BlockSpec((1, T, 128), lambda b, hp: (b, 0, hp))
    return pl.pallas_call(
        _moba_prompt_kernel,
        grid=(Bn, MB_W // 128),
        in_specs=[seq,
                  pl.BlockSpec((1, n_blocks, 128), lambda b, hp: (b, 0, hp)),
                  seq,
                  pl.BlockSpec((1, 128, T), lambda b, hp: (b, hp, 0))],
        out_specs=seq,
        out_shape=jax.ShapeDtypeStruct((Bn, T, MB_W), F32),
        scratch_shapes=[pltpu.VMEM((4, n_blocks, MB_BLOCK, MB_BLOCK), F32)],
        compiler_params=_cparams(("arbitrary", "arbitrary")),
        name="moba_prompt",
    )(q, km, kb, vt)


def _moba_sample_kernel(pt_ref, q_ref, kn_ref, vn_ref, ka_ref, kb_ref, va_ref, vb_ref, y_ref,
                        qbd_ref, oblk_ref, km_ref, m_ref, l_ref, *, n_blocks, ts):
    n = pl.program_id(1)
    R = MB_HEADS * ts
    own = (lax.broadcasted_iota(jnp.int32, (R, MB_W), 0) // ts
           == lax.broadcasted_iota(jnp.int32, (R, MB_W), 1) // MB_HD)
    blane = lax.broadcasted_iota(jnp.int32, (R, n_blocks), 1)

    @pl.when(n == 0)
    def _():
        for ref in (m_ref, l_ref, km_ref):
            ref[...] = jnp.zeros(ref.shape, F32)
        qbd_ref[...] = jnp.where(own, jnp.concatenate([q_ref[0]] * MB_HEADS, axis=0), 0.0)

    qbd = qbd_ref[...]
    qb = (qbd * (MB_HD ** -0.5)).astype(BF16)
    ka = ka_ref[0, 0].reshape(MB_W, PAGE_SIZE)
    kb = kb_ref[0, 0].reshape(MB_W, PAGE_SIZE)
    sa = jnp.dot(qb, ka.astype(BF16), preferred_element_type=F32)
    sb = jnp.dot(qb, kb.astype(BF16), preferred_element_type=F32)
    kmean = (jnp.sum(ka, axis=1, keepdims=True) + jnp.sum(kb, axis=1, keepdims=True)) * (1.0 / MB_BLOCK)
    kml = lax.broadcasted_iota(jnp.int32, (MB_W, n_blocks), 1)
    km_ref[...] = jnp.where(kml == n, kmean, km_ref[...])
    m_n = jnp.maximum(jnp.max(sa, axis=1, keepdims=True), jnp.max(sb, axis=1, keepdims=True))
    pa = jnp.exp(sa - m_n)
    pb = jnp.exp(sb - m_n)
    l_n = jnp.sum(pa, axis=1, keepdims=True) + jnp.sum(pb, axis=1, keepdims=True)
    va = va_ref[0, 0].reshape(MB_W, PAGE_SIZE).astype(BF16)
    vb = vb_ref[0, 0].reshape(MB_W, PAGE_SIZE).astype(BF16)
    oblk_ref[n] = _dot_nt(pa.astype(BF16), va) + _dot_nt(pb.astype(BF16), vb)
    m_ref[...] = jnp.where(blane == n, m_n, m_ref[...])
    l_ref[...] = jnp.where(blane == n, l_n, l_ref[...])

    @pl.when(n == n_blocks - 1)
    def _():
        gate = jnp.dot(qbd, km_ref[...], preferred_element_type=F32, precision=lax.Precision.HIGHEST)
        sel = _top_blocks(gate, n_blocks, axis=1) > 0.0
        s_own = _dot_nt(qb, kn_ref[0].astype(BF16))
        tq = lax.broadcasted_iota(jnp.int32, (R, ts), 0) % ts
        tk = lax.broadcasted_iota(jnp.int32, (R, ts), 1)
        s_own = jnp.where(tk <= tq, s_own, NEG_INF)
        mb = jnp.where(sel, m_ref[...], NEG_INF)
        m_all = jnp.maximum(jnp.max(s_own, axis=1, keepdims=True), jnp.max(mb, axis=1, keepdims=True))
        w = jnp.exp(mb - m_all)
        p_own = jnp.exp(s_own - m_all)
        l = jnp.sum(p_own, axis=1, keepdims=True) + jnp.sum(w * l_ref[...], axis=1, keepdims=True)
        o = jnp.dot(p_own.astype(BF16), vn_ref[0].astype(BF16), preferred_element_type=F32)
        for nb in range(n_blocks):
            o = o + w[:, nb:nb + 1] * oblk_ref[nb]
        o = jnp.where(own, o / l, 0.0)
        y = o[0:ts]
        for h in range(1, MB_HEADS):
            y = y + o[h * ts:(h + 1) * ts]
        y_ref[0] = y


def _moba_sample(q, kn, vn, cache_kt, cache_vt, page_table, layer):
    DB, ts, _ = q.shape
    n_pages = page_table.shape[1]
    ppb = MB_BLOCK // PAGE_SIZE
    n_blocks = n_pages // ppb
    R = MB_HEADS * ts
    tok = pl.BlockSpec((1, ts, MB_W), lambda b, n, pt: (b, 0, 0))

    def page(i):
        return pl.BlockSpec((1, 1, MB_HEADS, MB_HD, PAGE_SIZE),
                            lambda b, n, pt: (layer, pt[b * n_pages + n * ppb + i], 0, 0, 0))

    grid_spec = pltpu.PrefetchScalarGridSpec(
        num_scalar_prefetch=1,
        grid=(DB, n_blocks),
        in_specs=[tok, tok, tok, page(0), page(1), page(0), page(1)],
        out_specs=tok,
        scratch_shapes=[pltpu.VMEM((R, MB_W), F32),
                        pltpu.VMEM((n_blocks, R, MB_W), F32),
                        pltpu.VMEM((MB_W, n_blocks), F32),
                        pltpu.VMEM((R, n_blocks), F32),
                        pltpu.VMEM((R, n_blocks), F32)],
    )
    return pl.pallas_call(
        functools.partial(_moba_sample_kernel, n_blocks=n_blocks, ts=ts),
        grid_spec=grid_spec,
        out_shape=jax.ShapeDtypeStruct((DB, ts, MB_W), F32),
        compiler_params=_cparams(("arbitrary", "arbitrary")),
        name="moba_sample",
    )(page_table.reshape(-1), q, kn, vn, cache_kt, cache_kt, cache_vt, cache_vt)


def _gla_kernel(qg_ref, kg_ref, vg_ref, rg_ref, lg_ref, s0_ref, gn_ref, y_ref, sfin_ref, st_ref, *, n_chunks):
    t = pl.program_id(1)
    C = GLA_CHUNK

    @pl.when(t == 0)
    def _():
        for h in range(GLA_HEADS):
            st_ref[:, h * GLA_DK:(h + 1) * GLA_DK] = s0_ref[0, h].T

    ri = lax.broadcasted_iota(jnp.int32, (C, C), 0)
    ci = lax.broadcasted_iota(jnp.int32, (C, C), 1)
    tri = (ci <= ri).astype(BF16)
    m1 = (ri >= 32) & (ci < 32)
    m2 = (ri // 32 == ci // 32) & (ri % 32 >= 16) & (ci % 32 < 16)
    m3 = (ri // 16 == ci // 16) & (ci <= ri)
    grp = lax.broadcasted_iota(jnp.int32, (C, GLA_KW), 0)

    def chunk(c, carry):
        r0 = pl.multiple_of(c * C, C)
        rows = pl.ds(r0, C)
        lg = lg_ref[0, rows, :]
        hi, mid, lo = _split3(lg)
        b = (jnp.dot(tri, hi, preferred_element_type=F32) + jnp.dot(tri, mid, preferred_element_type=F32)
             + jnp.dot(tri, lo, preferred_element_type=F32))
        q = qg_ref[0, rows, :] * (GLA_DK ** -0.5)
        k = kg_ref[0, rows, :]
        brow = lambda i: jnp.broadcast_to(b[i:i + 1, :], (C, GLA_KW))
        ref1 = brow(31)
        ref2 = jnp.where(grp < 32, brow(15), brow(47))
        ref3 = jnp.where(grp < 16, brow(7), jnp.where(grp < 32, brow(23), jnp.where(grp < 48, brow(39), brow(55))))
        b_last = b[C - 1:C, :]
        qs = (q * jnp.exp(b)).astype(BF16)
        ks = (k * jnp.exp(b_last - b)).astype(BF16)
        q1 = (q * jnp.exp(jnp.minimum(b - ref1, 0.0))).astype(BF16)
        k1 = (k * jnp.exp(jnp.minimum(ref1 - b, 0.0))).astype(BF16)
        q2 = (q * jnp.exp(jnp.minimum(b - ref2, 0.0))).astype(BF16)
        k2 = (k * jnp.exp(jnp.minimum(ref2 - b, 0.0))).astype(BF16)
        q3 = (q * jnp.exp(b - ref3)).astype(BF16)
        k3 = (k * jnp.exp(ref3 - b)).astype(BF16)
        st_old = st_ref[...]
        st_ref[...] = st_old * jnp.exp(b_last)
        for h in range(GLA_HEADS):
            kl = slice(h * GLA_DK, (h + 1) * GLA_DK)
            vl = slice(h * GLA_DV, (h + 1) * GLA_DV)
            a = (jnp.where(m1, _dot_nt(q1[:, kl], k1[:, kl]), 0.0)
                 + jnp.where(m2, _dot_nt(q2[:, kl], k2[:, kl]), 0.0)
                 + jnp.where(m3, _dot_nt(q3[:, kl], k3[:, kl]), 0.0))
            vf = vg_ref[0, rows, vl]
            o = (_dot_nt(qs[:, kl], st_old[:, kl].astype(BF16))
                 + jnp.dot(a.astype(BF16), vf.astype(BF16), preferred_element_type=F32))
            st_ref[:, kl] += jnp.dot(vf.T.astype(BF16), ks[:, kl], preferred_element_type=F32)
            y_ref[0, rows, vl] = _rms(o, gn_ref[:, vl]) * rg_ref[0, rows, vl]
        return carry

    lax.fori_loop(0, n_chunks, chunk, 0, unroll=True)

    @pl.when(t == pl.num_programs(1) - 1)
    def _():
        for h in range(GLA_HEADS):
            sfin_ref[0, h] = st_ref[:, h * GLA_DK:(h + 1) * GLA_DK].T


def _gla(qg, kg, vg, rg, lg, s0, gn):
    Bn, T, _ = qg.shape
    TT = min(T, 512)
    kspec = pl.BlockSpec((1, TT, GLA_KW), lambda b, t: (b, t, 0))
    vspec = pl.BlockSpec((1, TT, GLA_VW), lambda b, t: (b, t, 0))
    sspec = pl.BlockSpec((1, GLA_HEADS, GLA_DK, GLA_DV), lambda b, t: (b, 0, 0, 0))
    return pl.pallas_call(
        functools.partial(_gla_kernel, n_chunks=TT // GLA_CHUNK),
        grid=(Bn, T // TT),
        in_specs=[kspec, kspec, vspec, vspec, kspec, sspec, _const_spec((1, GLA_VW))],
        out_specs=[vspec, sspec],
        out_shape=[jax.ShapeDtypeStruct((Bn, T, GLA_VW), F32),
                   jax.ShapeDtypeStruct((Bn, GLA_HEADS, GLA_DK, GLA_DV), F32)],
        scratch_shapes=[pltpu.VMEM((GLA_DV, GLA_KW), F32)],
        compiler_params=_cparams(("arbitrary", "arbitrary")),
        name="gla",
    )(qg, kg, vg, rg, lg, s0, gn)


def _merge_ffn_kernel(x_ref, yp_ref, ym_ref, yg_ref, sg_ref, wp_ref, wmb_ref, wgl_ref, wo_ref,
                      gpost_ref, gfpre_ref, gfpost_ref, w1_ref, w2_ref, o_ref):
    def br(y_ref, w_ref, j):
        return sg_ref[:, 1024 * j:1024 * (j + 1)] * jnp.dot(y_ref[...].astype(BF16), w_ref[...],
                                                            preferred_element_type=F32)

    merged = br(yp_ref, wp_ref, 0) + br(ym_ref, wmb_ref, 1) + br(yg_ref, wgl_ref, 2)
    z = jnp.dot(merged.astype(BF16), wo_ref[...], preferred_element_type=F32)
    x1 = x_ref[...] + _rms(z, gpost_ref[...])
    hb = _rms(x1, gfpre_ref[...]).astype(BF16)
    f = jnp.zeros(x1.shape, F32)
    FC = 1024
    for c in range(D_FF // FC):
        a = jnp.maximum(jnp.dot(hb, w1_ref[:, c * FC:(c + 1) * FC], preferred_element_type=F32), 0.0)
        f = f + jnp.dot((a * a).astype(BF16), w2_ref[c * FC:(c + 1) * FC, :], preferred_element_type=F32)
    o_ref[...] = x1 + _rms(f, gfpost_ref[...])


def _merge_ffn(x, yp, ym, yg, sg, wp, wmb, wgl, wo, gpost, gfpre, gfpost, w1, w2):
    M = x.shape[0]
    tm = min(ROW_TILE, M)
    row = lambda w: pl.BlockSpec((tm, w), lambda i: (i, 0))
    vec = _const_spec((1, D_MODEL))
    return pl.pallas_call(
        _merge_ffn_kernel,
        grid=(M // tm,),
        in_specs=[row(D_MODEL), row(POOL_W), row(MB_W), row(GLA_VW), row(3 * D_MODEL),
                  _const_spec(wp.shape), _const_spec(wmb.shape), _const_spec(wgl.shape), _const_spec(wo.shape),
                  vec, vec, vec, _const_spec(w1.shape), _const_spec(w2.shape)],
        out_specs=row(D_MODEL),
        out_shape=jax.ShapeDtypeStruct((M, D_MODEL), F32),
        compiler_params=_cparams(("arbitrary",)),
        name="merge_ffn",
    )(x, yp, ym, yg, sg, wp, wmb, wgl, wo, gpost, gfpre, gfpost, w1, w2)


def _layer(x, Bn, T, tabs, hist16, pos0, s0, sample_attend, lw):
    (wm, wa, wg, wa2, ba, gn, wgrp, pscale, wp, wmb, wgl, wo, gpre, gpost, gfpre, gfpost, w1, w2) = lw
    seq = lambda a: a.reshape(Bn, T, a.shape[-1])
    if sample_attend is None:
        u, q, k, v, qg, kg, vg, rg, lg, sg, kb, vt, km = _in_proj(x, gpre, wm, wa, wg, wa2, ba, tabs, seq_len=T)
        ym = _moba_prompt(seq(q), km.reshape(Bn, T // MB_BLOCK, MB_W), seq(kb), vt)
    else:
        u, q, k, v, qg, kg, vg, rg, lg, sg = _in_proj(x, gpre, wm, wa, wg, wa2, ba, tabs)
        ym = sample_attend(seq(q), seq(k), seq(v))
    u3 = seq(u)
    yp = _pool(u3, hist16, wgrp, pscale, pos0)
    Tp = -(-T // GLA_CHUNK) * GLA_CHUNK
    padt = lambda a: jnp.pad(seq(a), ((0, 0), (0, Tp - T), (0, 0))) if Tp != T else seq(a)
    yg, s_new = _gla(padt(qg), padt(kg), padt(vg), padt(rg), padt(lg), s0, gn)
    yg = yg[:, :T].reshape(Bn * T, GLA_VW)
    x = _merge_ffn(x, yp.reshape(Bn * T, POOL_W), ym.reshape(Bn * T, MB_W), yg, sg,
                   wp, wmb, wgl, wo, gpost, gfpre, gfpost, w1, w2)
    new_hist = jnp.concatenate([hist16[:, 1:], u3], axis=1)[:, -POOL_HIST:]
    kv = lambda a: a.reshape(Bn, T, MB_HEADS, MB_HD)
    return x, kv(k), kv(v), new_hist, s_new


def kernel(x_prompt, x_sample, cache_k, cache_v, state_pool, state_gla, page_table, w_in, w_gla_a2, b_gla_a,
           gla_norm, w_pool_grp, pool_scale, w_pool_out, w_mb_out, w_gla_out, w_o, norm_mix_pre,
           norm_mix_post, norm_ffn_pre, norm_ffn_post, w_ff1, w_ff2):
    B, T, D = x_prompt.shape
    DB, TS, _ = x_sample.shape
    depth = w_in.shape[0]
    n_pages = page_table.shape[1]
    past_len = n_pages * PAGE_SIZE
    assert D == D_MODEL and T % MB_BLOCK == 0 and past_len % MB_BLOCK == 0
    assert past_len // MB_BLOCK >= MB_TOPK and (B * T) % ROW_TILE == 0 and MB_BLOCK == 2 * PAGE_SIZE

    tabs_p = _rope_tables(jnp.arange(T))
    reps = max(1, min(ROW_TILE, DB * TS) // TS)
    tabs_s = tuple(jnp.tile(t, (reps, 1)) for t in _rope_tables(past_len + jnp.arange(TS)))
    ckt = jnp.transpose(cache_k, (0, 1, 3, 4, 2))
    cvt = jnp.transpose(cache_v, (0, 1, 3, 4, 2))
    zrow = lambda n: jnp.zeros((n, 1, POOL_W), F32)
    hist_p = jnp.zeros((B, 16, POOL_W), F32)
    s0_p = jnp.zeros((B, GLA_HEADS, GLA_DK, GLA_DV), F32)

    hp = x_prompt.reshape(B * T, D)
    hs = x_sample.reshape(DB * TS, D)
    outs_p, outs_s = [], []
    for l in range(depth):
        bf = lambda a: a.astype(BF16)
        vec = lambda a: a.reshape(1, -1)
        lw = (bf(w_in[l][:, :W_MAIN]), bf(w_in[l][:, W_MAIN:W_MAIN + GLA_RANK]), bf(w_in[l][:, W_MAIN + GLA_RANK:]),
              bf(w_gla_a2[l]), vec(b_gla_a[l]), vec(gla_norm[l]), bf(w_pool_grp[l]), vec(pool_scale[l]),
              bf(w_pool_out[l]), bf(w_mb_out[l]), bf(w_gla_out[l]), bf(w_o[l]),
              vec(norm_mix_pre[l]), vec(norm_mix_post[l]), vec(norm_ffn_pre[l]), vec(norm_ffn_post[l]),
              bf(w_ff1[l]), bf(w_ff2[l]))
        hp, *op = _layer(hp, B, T, tabs_p, hist_p, 0, s0_p, None, lw)
        hist_s = jnp.concatenate([zrow(DB), state_pool[l]], axis=1)
        attend_s = functools.partial(_moba_sample, cache_kt=ckt, cache_vt=cvt, page_table=page_table, layer=l)
        hs, *os_ = _layer(hs, DB, TS, tabs_s, hist_s, past_len, state_gla[l], attend_s, lw)
        outs_p.append(op)
        outs_s.append(os_)
    stack = lambda outs, i: jnp.stack([o[i] for o in outs])
    return (hp.reshape(B, T, D), hs.reshape(DB, TS, D),
            stack(outs_p, 0), stack(outs_p, 1), stack(outs_p, 2), stack(outs_p, 3),
            stack(outs_s, 0), stack(outs_s, 1), stack(outs_s, 2), stack(outs_s, 3))
```

```python
import functools

import jax
import jax.numpy as jnp
from jax import lax
from jax.experimental import pallas as pl
from jax.experimental.pallas import tpu as pltpu

F32 = jnp.float32
BF16 = jnp.bfloat16

D_MODEL = 1024
PAGE_SIZE = 128
POOL_GC = 128
POOL_W = 512
POOL_WINDOWS = (2, 4, 8, 16)
POOL_HIST = 15
MB_HEADS = 8
MB_HD = 64
MB_W = 512
MB_BLOCK = 256
MB_TOPK = 3
ROPE_THETA = 500000.0
ROPE_DIM = 16
GLA_HEADS = 4
GLA_DK = 64
GLA_DV = 128
GLA_KW = 256
GLA_VW = 512
GLA_RANK = 16
GLA_NORMALIZER = 16.0
GLA_CHUNK = 64
D_FF = 4096
EPS = 1e-6
W_MAIN = POOL_W + 3 * MB_W + 2 * GLA_KW + 2 * GLA_VW
NEG_INF = float("-inf")

VMEM_LIMIT_BYTES = 56 * 1024 * 1024
ROW_TILE = 256
FFN_ROW_TILE = 512
SAMPLE_BLOCKS_PER_STEP = 4


def _cparams(sem):
    return pltpu.CompilerParams(dimension_semantics=sem, vmem_limit_bytes=VMEM_LIMIT_BYTES)


def _const_spec(shape):
    nd = len(shape)
    return pl.BlockSpec(shape, lambda *_: (0,) * nd, pipeline_mode=pl.Buffered(1))


def _rms(x, g):
    ms = jnp.mean(x * x, axis=-1, keepdims=True)
    return x * lax.rsqrt(ms + EPS) * g


def _sigmoid(x):
    return 1.0 / (1.0 + jnp.exp(-x))


def _split3(x):
    hi = x.astype(BF16)
    r = x - hi.astype(F32)
    mid = r.astype(BF16)
    lo = (r - mid.astype(F32)).astype(BF16)
    return hi, mid, lo


def _in_proj_kernel(*refs, prompt, first_layer):
    x_ref, gpre_ref, wm_ref, wa_ref, wg_ref, wa2_ref, ba_ref, cos_ref, sa_ref, sb_ref = refs[:10]
    n_in = 10 + (0 if not prompt else 1 if first_layer else 3)
    wvt_ref = refs[10] if prompt else None
    outs = refs[n_in:]
    u_ref, q_ref = outs[:2]
    qg_ref, kg_ref, vg_ref, rg_ref, lg_ref, sg_ref = outs[-6:]
    h = _rms(x_ref[...], gpre_ref[...]).astype(BF16)

    def proj(w_ref, a, b):
        return jnp.dot(h, w_ref[:, a:b], preferred_element_type=F32)

    cos, sa, sb = cos_ref[...], sa_ref[...], sb_ref[...]

    def rope(col0):
        half = ROPE_DIM // 2
        for j in range(MB_W // 128):
            xj = proj(wm_ref, col0 + 128 * j, col0 + 128 * (j + 1))
            yield j, xj * cos + pltpu.roll(xj, 128 - half, 1) * sa + pltpu.roll(xj, half, 1) * sb

    u_ref[...] = proj(wm_ref, 0, 512)
    for j, rj in rope(512):
        q_ref[:, 128 * j:128 * (j + 1)] = rj
    if prompt:
        kb_ref, km_ref, kt_ref, vt_ref = outs[2:6]
        col_sums = []
        for j, rj in rope(1024):
            kb_ref[:, 128 * j:128 * (j + 1)] = rj.astype(BF16)
            kt_ref[0, 0, 128 * j:128 * (j + 1), :] = rj.T
            col_sums.append(jnp.sum(rj, axis=0, keepdims=True))
        km_ref[0] = jnp.concatenate(col_sums, axis=1) * (1.0 / MB_BLOCK)
        vt_ref[0, 0] = _dot_nt(wvt_ref[...], h)
        for d in range(1, kt_ref.shape[0]):
            kt_ref[d] = jnp.zeros(kt_ref.shape[1:], F32)
            vt_ref[d] = jnp.zeros(vt_ref.shape[1:], F32)
    else:
        k_ref, v_ref = outs[2:4]
        for j, rj in rope(1024):
            k_ref[:, 128 * j:128 * (j + 1)] = rj
        v_ref[...] = proj(wm_ref, 1536, 2048)
    qg_ref[...] = proj(wm_ref, 2048, 2304)
    kg_ref[...] = proj(wm_ref, 2304, 2560)
    vg_ref[...] = proj(wm_ref, 2560, 3072)
    r = proj(wm_ref, 3072, 3584)
    rg_ref[...] = r * _sigmoid(r)
    a = jnp.dot(h, wa_ref[...], preferred_element_type=F32)
    xg = jnp.dot(a.astype(BF16), wa2_ref[...], preferred_element_type=F32) + ba_ref[...]
    lg_ref[...] = (jnp.minimum(xg, 0.0) - jnp.log1p(jnp.exp(-jnp.abs(xg)))) * (1.0 / GLA_NORMALIZER)
    for j in range(3):
        sg_ref[:, 1024 * j:1024 * (j + 1)] = _sigmoid(proj(wg_ref, 1024 * j, 1024 * (j + 1)))


def _in_proj(x, gpre, wm, wa, wg, wa2, ba, tabs, prompt_state=None):
    M = x.shape[0]
    tm = min(ROW_TILE, M)
    period = tabs[0].shape[0] // tm
    row = lambda w: pl.BlockSpec((tm, w), lambda i: (i, 0))
    tab = pl.BlockSpec((tm, 128), lambda i: (i % period, 0))
    f32 = lambda w: jax.ShapeDtypeStruct((M, w), F32)
    tail_w = (GLA_KW, GLA_KW, GLA_VW, GLA_VW, GLA_KW, 3 * D_MODEL)
    in_specs = [row(D_MODEL), _const_spec((1, D_MODEL)), _const_spec(wm.shape), _const_spec(wa.shape),
                _const_spec(wg.shape), _const_spec(wa2.shape), _const_spec((1, GLA_KW)), tab, tab, tab]
    args = [x, gpre, wm, wa, wg, wa2, ba, *tabs]
    aliases = {}
    if prompt_state is None:
        out_specs = [row(POOL_W), row(MB_W), row(MB_W), row(MB_W)]
        out_shape = [f32(POOL_W), f32(MB_W), f32(MB_W), f32(MB_W)]
        first = False
    else:
        seq_len, depth, layer, wvt, kt_all, vt_all = prompt_state
        assert tm == MB_BLOCK and seq_len % tm == 0
        tps = seq_len // tm
        first = layer == 0
        stacked = jax.ShapeDtypeStruct((depth, M // seq_len, MB_W, seq_len), F32)
        in_specs.append(_const_spec(wvt.shape))
        args.append(wvt)
        if first:
            st_spec = pl.BlockSpec((depth, 1, MB_W, tm), lambda i: (0, i // tps, 0, i % tps))
        else:
            st_spec = pl.BlockSpec((1, 1, MB_W, tm), lambda i: (layer, i // tps, 0, i % tps))
            in_specs += [pl.BlockSpec(memory_space=pl.ANY)] * 2
            args += [kt_all, vt_all]
            aliases = {11: 4, 12: 5}
        out_specs = [row(POOL_W), row(MB_W), row(MB_W), pl.BlockSpec((1, 1, MB_W), lambda i: (i, 0, 0)),
                     st_spec, st_spec]
        out_shape = [f32(POOL_W), f32(MB_W), jax.ShapeDtypeStruct((M, MB_W), BF16),
                     jax.ShapeDtypeStruct((M // tm, 1, MB_W), F32), stacked, stacked]
    return pl.pallas_call(
        functools.partial(_in_proj_kernel, prompt=prompt_state is not None, first_layer=first),
        grid=(M // tm,),
        in_specs=in_specs,
        out_specs=out_specs + [row(w) for w in tail_w],
        out_shape=out_shape + [f32(w) for w in tail_w],
        input_output_aliases=aliases,
        compiler_params=_cparams(("arbitrary",)),
        name="in_proj",
    )(*args)


def _rope_tables(pos):
    half = ROPE_DIM // 2
    inv = ROPE_THETA ** (-jnp.arange(half, dtype=F32) / half)
    ang = pos.astype(F32)[:, None] * inv[None, :]
    cos, sin = jnp.cos(ang), jnp.sin(ang)
    n = pos.shape[0]
    zeros = jnp.zeros((n, MB_HD - ROPE_DIM), F32)
    zh = jnp.zeros((n, half), F32)
    c = jnp.concatenate([cos, cos, zeros + 1.0], axis=1)
    sa = jnp.concatenate([-sin, zh, zeros], axis=1)
    sb = jnp.concatenate([zh, sin, zeros], axis=1)
    two = lambda t: jnp.concatenate([t, t], axis=1)
    return two(c), two(sa), two(sb)


def _pool_kernel(u_ref, hist_ref, wgrp_ref, scale_ref, y_ref, ext_ref, *, pos0, T, R):
    ext_ref[0:16, :] = hist_ref[0]
    ext_ref[16:16 + T, :] = u_ref[0]
    for c in range(T // R):
        r0 = 16 + c * R
        posp1 = lax.broadcasted_iota(jnp.int32, (R, 1), 0) + (pos0 + c * R + 1)
        for g, w in enumerate(POOL_WINDOWS):
            lanes = slice(g * POOL_GC, (g + 1) * POOL_GC)
            cur = ext_ref[r0:r0 + R, lanes]
            acc = cur
            for j in range(1, w):
                acc = acc + ext_ref[r0 - j:r0 - j + R, lanes]
            cnt = jnp.minimum(posp1, w).astype(F32)
            d = acc / cnt - cur
            yg = jnp.dot(d.astype(BF16), wgrp_ref[g], preferred_element_type=F32)
            y_ref[0, c * R:(c + 1) * R, lanes] = yg * scale_ref[:, lanes]


def _pool(u, hist16, wgrp, scale, pos0):
    Bn, T, _ = u.shape
    R = min(T, 256)
    seq = pl.BlockSpec((1, T, POOL_W), lambda b: (b, 0, 0))
    return pl.pallas_call(
        functools.partial(_pool_kernel, pos0=pos0, T=T, R=R),
        grid=(Bn,),
        in_specs=[seq, pl.BlockSpec((1, 16, POOL_W), lambda b: (b, 0, 0)),
                  _const_spec(wgrp.shape), _const_spec((1, POOL_W))],
        out_specs=seq,
        out_shape=jax.ShapeDtypeStruct((Bn, T, POOL_W), F32),
        scratch_shapes=[pltpu.VMEM((16 + T, POOL_W), F32)],
        compiler_params=_cparams(("arbitrary",)),
        name="pool",
    )(u, hist16, wgrp, scale)


def _top_blocks(gate, n_valid, axis):
    n_blocks = gate.shape[axis]
    iota = lax.broadcasted_iota(jnp.int32, gate.shape, axis)
    g = jnp.where(iota < n_valid, gate, NEG_INF)
    sel = jnp.zeros(gate.shape, F32)
    for _ in range(MB_TOPK):
        m = jnp.max(g, axis=axis, keepdims=True)
        idx = jnp.min(jnp.where(g == m, iota, n_blocks), axis=axis, keepdims=True)
        pick = (iota == idx) & (m > NEG_INF)
        sel = jnp.where(pick, 1.0, sel)
        g = jnp.where(pick, NEG_INF, g)
    return sel


def _col_max(s):
    r, c = s.shape
    return jnp.max(jnp.max(s.reshape(r // 8, 8, c), axis=0), axis=0, keepdims=True)


def _dot_nt(a, b, precision=None):
    return lax.dot_general(a, b, (((1,), (1,)), ((), ())), preferred_element_type=F32, precision=precision)


def _moba_prompt_kernel(q_ref, km_ref, k_ref, vt_ref, y_ref, s_ref, vtb_ref):
    B = MB_BLOCK
    n_blocks = km_ref.shape[1]
    qlane = lax.broadcasted_iota(jnp.int32, (B, 128), 1) // MB_HD
    kmlane = lax.broadcasted_iota(jnp.int32, (n_blocks, 128), 1) // MB_HD
    causal = lax.broadcasted_iota(jnp.int32, (B, B), 0) <= lax.broadcasted_iota(jnp.int32, (B, B), 1)
    drow = lax.broadcasted_iota(jnp.int32, (128, B), 0) // MB_HD
    vtb_ref[...] = vt_ref[0, 0].astype(BF16)

    for c in range(n_blocks):
        q = q_ref[0, c * B:(c + 1) * B, :]
        outs = []
        for hh in range(2):
            qh = jnp.where(qlane == hh, q, 0.0)
            qs = (qh * (MB_HD ** -0.5)).astype(BF16)
            if c > 0:
                gate_t = _dot_nt(jnp.where(kmlane == hh, km_ref[0], 0.0), qh, precision=lax.Precision.HIGHEST)
                sel = jnp.where(_top_blocks(gate_t, c, axis=0) > 0.0, 0.0, NEG_INF)
            slot = 2 * (c % 2) + hh
            m = None
            for j in range(c + 1):
                s = _dot_nt(k_ref[0, j * B:(j + 1) * B, :], qs)
                s = jnp.where(causal, s, NEG_INF) if j == c else s + sel[j:j + 1, :]
                s_ref[slot, j] = s
                mj = _col_max(s)
                m = mj if m is None else jnp.maximum(m, mj)
            l = jnp.zeros((1, B), F32)
            acc = jnp.zeros((128, B), F32)
            for j in range(c + 1):
                p = jnp.exp(s_ref[slot, j] - m)
                l = l + jnp.sum(p, axis=0, keepdims=True)
                acc = acc + jnp.dot(vtb_ref[:, j * B:(j + 1) * B], p.astype(BF16),
                                    preferred_element_type=F32)
            outs.append(acc / l)
        y_ref[0, c * B:(c + 1) * B, :] = jnp.where(drow == 0, outs[0], outs[1]).T


def _moba_prompt(q, km, kb, vt_all, layer):
    Bn, T, _ = q.shape
    n_blocks = T // MB_BLOCK
    seq = pl.BlockSpec((1, T, 128), lambda b, hp: (b, 0, hp))
    return pl.pallas_call(
        _moba_prompt_kernel,
        grid=(Bn, MB_W // 128),
        in_specs=[seq,
                  pl.BlockSpec((1, n_blocks, 128), lambda b, hp: (b, 0, hp)),
                  seq,
                  pl.BlockSpec((1, 1, 128, T), lambda b, hp: (layer, b, hp, 0))],
        out_specs=seq,
        out_shape=jax.ShapeDtypeStruct((Bn, T, MB_W), F32),
        scratch_shapes=[pltpu.VMEM((4, n_blocks, MB_BLOCK, MB_BLOCK), F32), pltpu.VMEM((128, T), BF16)],
        compiler_params=_cparams(("arbitrary", "arbitrary")),
        name="moba_prompt",
    )(q, km, kb, vt_all)


def _moba_sample_kernel(pt_ref, q_ref, kn_ref, vn_ref, *refs, n_blocks, ts, bps):
    ppb = MB_BLOCK // PAGE_SIZE
    k_refs = refs[:bps * ppb]
    v_refs = refs[bps * ppb:2 * bps * ppb]
    y_ref, qbd_ref, oblk_ref, km_ref, m_ref, l_ref = refs[2 * bps * ppb:]
    step = pl.program_id(1)
    R = MB_HEADS * ts
    own = (lax.broadcasted_iota(jnp.int32, (R, MB_W), 0) // ts
           == lax.broadcasted_iota(jnp.int32, (R, MB_W), 1) // MB_HD)
    blane = lax.broadcasted_iota(jnp.int32, (R, n_blocks), 1)
    kml = lax.broadcasted_iota(jnp.int32, (MB_W, n_blocks), 1)

    @pl.when(step == 0)
    def _():
        for ref in (m_ref, l_ref, km_ref):
            ref[...] = jnp.zeros(ref.shape, F32)
        qbd_ref[...] = jnp.where(own, jnp.concatenate([q_ref[0]] * MB_HEADS, axis=0), 0.0)

    qbd = qbd_ref[...]
    qb = (qbd * (MB_HD ** -0.5)).astype(BF16)
    page = lambda ref: ref[0, 0].reshape(MB_W, PAGE_SIZE)
    m_all, l_all, km_all = m_ref[...], l_ref[...], km_ref[...]
    for i in range(bps):
        n = step * bps + i
        ka, kb = page(k_refs[ppb * i]), page(k_refs[ppb * i + 1])
        sa = jnp.dot(qb, ka.astype(BF16), preferred_element_type=F32)
        sb = jnp.dot(qb, kb.astype(BF16), preferred_element_type=F32)
        kmean = (jnp.sum(ka, axis=1, keepdims=True) + jnp.sum(kb, axis=1, keepdims=True)) * (1.0 / MB_BLOCK)
        m_n = jnp.maximum(jnp.max(sa, axis=1, keepdims=True), jnp.max(sb, axis=1, keepdims=True))
        pa = jnp.exp(sa - m_n)
        pb = jnp.exp(sb - m_n)
        l_n = jnp.sum(pa, axis=1, keepdims=True) + jnp.sum(pb, axis=1, keepdims=True)
        va = page(v_refs[ppb * i]).astype(BF16)
        vb = page(v_refs[ppb * i + 1]).astype(BF16)
        oblk_ref[n] = _dot_nt(pa.astype(BF16), va) + _dot_nt(pb.astype(BF16), vb)
        km_all = jnp.where(kml == n, kmean, km_all)
        m_all = jnp.where(blane == n, m_n, m_all)
        l_all = jnp.where(blane == n, l_n, l_all)
    km_ref[...] = km_all
    m_ref[...] = m_all
    l_ref[...] = l_all

    @pl.when(step == n_blocks // bps - 1)
    def _():
        gate = jnp.dot(qbd, km_ref[...], preferred_element_type=F32, precision=lax.Precision.HIGHEST)
        sel = _top_blocks(gate, n_blocks, axis=1) > 0.0
        s_own = _dot_nt(qb, kn_ref[0].astype(BF16))
        tq = lax.broadcasted_iota(jnp.int32, (R, ts), 0) % ts
        tk = lax.broadcasted_iota(jnp.int32, (R, ts), 1)
        s_own = jnp.where(tk <= tq, s_own, NEG_INF)
        mb = jnp.where(sel, m_ref[...], NEG_INF)
        m_all = jnp.maximum(jnp.max(s_own, axis=1, keepdims=True), jnp.max(mb, axis=1, keepdims=True))
        w = jnp.exp(mb - m_all)
        p_own = jnp.exp(s_own - m_all)
        l = jnp.sum(p_own, axis=1, keepdims=True) + jnp.sum(w * l_ref[...], axis=1, keepdims=True)
        o = jnp.dot(p_own.astype(BF16), vn_ref[0].astype(BF16), preferred_element_type=F32)
        for nb in range(n_blocks):
            o = o + w[:, nb:nb + 1] * oblk_ref[nb]
        o = jnp.where(own, o / l, 0.0)
        y = o[0:ts]
        for h in range(1, MB_HEADS):
            y = y + o[h * ts:(h + 1) * ts]
        y_ref[0] = y


def _moba_sample(q, kn, vn, cache_kt, cache_vt, page_table, layer):
    DB, ts, _ = q.shape
    n_pages = page_table.shape[1]
    ppb = MB_BLOCK // PAGE_SIZE
    n_blocks = n_pages // ppb
    R = MB_HEADS * ts
    tok = pl.BlockSpec((1, ts, MB_W), lambda b, n, pt: (b, 0, 0))
    bps = SAMPLE_BLOCKS_PER_STEP
    assert n_blocks % bps == 0
    pps = bps * ppb

    def page(i):
        return pl.BlockSpec((1, 1, MB_HEADS, MB_HD, PAGE_SIZE),
                            lambda b, n, pt: (layer, pt[b * n_pages + n * pps + i], 0, 0, 0))

    pages = [page(i) for i in range(pps)]
    grid_spec = pltpu.PrefetchScalarGridSpec(
        num_scalar_prefetch=1,
        grid=(DB, n_blocks // bps),
        in_specs=[tok, tok, tok] + pages + pages,
        out_specs=tok,
        scratch_shapes=[pltpu.VMEM((R, MB_W), F32),
                        pltpu.VMEM((n_blocks, R, MB_W), F32),
                        pltpu.VMEM((MB_W, n_blocks), F32),
                        pltpu.VMEM((R, n_blocks), F32),
                        pltpu.VMEM((R, n_blocks), F32)],
    )
    return pl.pallas_call(
        functools.partial(_moba_sample_kernel, n_blocks=n_blocks, ts=ts, bps=bps),
        grid_spec=grid_spec,
        out_shape=jax.ShapeDtypeStruct((DB, ts, MB_W), F32),
        compiler_params=_cparams(("arbitrary", "arbitrary")),
        name="moba_sample",
    )(page_table.reshape(-1), q, kn, vn, *([cache_kt] * pps), *([cache_vt] * pps))


def _gla_kernel(qg_ref, kg_ref, vg_ref, rg_ref, lg_ref, s0_ref, gn_ref, y_ref, sfin_ref, st_ref, *, n_chunks):
    t = pl.program_id(1)
    C = GLA_CHUNK

    @pl.when(t == 0)
    def _():
        for h in range(GLA_HEADS):
            st_ref[:, h * GLA_DK:(h + 1) * GLA_DK] = s0_ref[0, h].T

    ri = lax.broadcasted_iota(jnp.int32, (C, C), 0)
    ci = lax.broadcasted_iota(jnp.int32, (C, C), 1)
    tri = (ci <= ri).astype(BF16)
    m1 = (ri >= 32) & (ci < 32)
    m2 = (ri // 32 == ci // 32) & (ri % 32 >= 16) & (ci % 32 < 16)
    m3 = (ri // 16 == ci // 16) & (ci <= ri)
    grp = lax.broadcasted_iota(jnp.int32, (C, GLA_KW), 0)

    def chunk(c, carry):
        r0 = pl.multiple_of(c * C, C)
        rows = pl.ds(r0, C)
        lg = lg_ref[0, rows, :]
        hi, mid, lo = _split3(lg)
        b = (jnp.dot(tri, hi, preferred_element_type=F32) + jnp.dot(tri, mid, preferred_element_type=F32)
             + jnp.dot(tri, lo, preferred_element_type=F32))
        q = qg_ref[0, rows, :] * (GLA_DK ** -0.5)
        k = kg_ref[0, rows, :]
        brow = lambda i: jnp.broadcast_to(b[i:i + 1, :], (C, GLA_KW))
        ref1 = brow(31)
        ref2 = jnp.where(grp < 32, brow(15), brow(47))
        ref3 = jnp.where(grp < 16, brow(7), jnp.where(grp < 32, brow(23), jnp.where(grp < 48, brow(39), brow(55))))
        b_last = b[C - 1:C, :]
        qs = (q * jnp.exp(b)).astype(BF16)
        ks = (k * jnp.exp(b_last - b)).astype(BF16)
        q1 = (q * jnp.exp(jnp.minimum(b - ref1, 0.0))).astype(BF16)
        k1 = (k * jnp.exp(jnp.minimum(ref1 - b, 0.0))).astype(BF16)
        q2 = (q * jnp.exp(jnp.minimum(b - ref2, 0.0))).astype(BF16)
        k2 = (k * jnp.exp(jnp.minimum(ref2 - b, 0.0))).astype(BF16)
        q3 = (q * jnp.exp(b - ref3)).astype(BF16)
        k3 = (k * jnp.exp(ref3 - b)).astype(BF16)
        st_old = st_ref[...]
        st_ref[...] = st_old * jnp.exp(b_last)
        for h in range(GLA_HEADS):
            kl = slice(h * GLA_DK, (h + 1) * GLA_DK)
            vl = slice(h * GLA_DV, (h + 1) * GLA_DV)
            a = (jnp.where(m1, _dot_nt(q1[:, kl], k1[:, kl]), 0.0)
                 + jnp.where(m2, _dot_nt(q2[:, kl], k2[:, kl]), 0.0)
                 + jnp.where(m3, _dot_nt(q3[:, kl], k3[:, kl]), 0.0))
            vf = vg_ref[0, rows, vl]
            o = (_dot_nt(qs[:, kl], st_old[:, kl].astype(BF16))
                 + jnp.dot(a.astype(BF16), vf.astype(BF16), preferred_element_type=F32))
            st_ref[:, kl] += jnp.dot(vf.T.astype(BF16), ks[:, kl], preferred_element_type=F32)
            y_ref[0, rows, vl] = _rms(o, gn_ref[:, vl]) * rg_ref[0, rows, vl]
        return carry

    lax.fori_loop(0, n_chunks, chunk, 0, unroll=True)

    @pl.when(t == pl.num_programs(1) - 1)
    def _():
        for h in range(GLA_HEADS):
            sfin_ref[0, h] = st_ref[:, h * GLA_DK:(h + 1) * GLA_DK].T


def _gla(qg, kg, vg, rg, lg, s0, gn):
    Bn, T, _ = qg.shape
    TT = min(T, 512)
    kspec = pl.BlockSpec((1, TT, GLA_KW), lambda b, t: (b, t, 0))
    vspec = pl.BlockSpec((1, TT, GLA_VW), lambda b, t: (b, t, 0))
    sspec = pl.BlockSpec((1, GLA_HEADS, GLA_DK, GLA_DV), lambda b, t: (b, 0, 0, 0))
    return pl.pallas_call(
        functools.partial(_gla_kernel, n_chunks=TT // GLA_CHUNK),
        grid=(Bn, T // TT),
        in_specs=[kspec, kspec, vspec, vspec, kspec, sspec, _const_spec((1, GLA_VW))],
        out_specs=[vspec, sspec],
        out_shape=[jax.ShapeDtypeStruct((Bn, T, GLA_VW), F32),
                   jax.ShapeDtypeStruct((Bn, GLA_HEADS, GLA_DK, GLA_DV), F32)],
        scratch_shapes=[pltpu.VMEM((GLA_DV, GLA_KW), F32)],
        compiler_params=_cparams(("arbitrary", "arbitrary")),
        name="gla",
    )(qg, kg, vg, rg, lg, s0, gn)


def _merge_ffn_kernel(x_ref, yp_ref, ym_ref, yg_ref, sg_ref, wp_ref, wmb_ref, wgl_ref, wo_ref,
                      gpost_ref, gfpre_ref, gfpost_ref, w1_ref, w2_ref, o_ref):
    def br(y_ref, w_ref, j):
        return sg_ref[:, 1024 * j:1024 * (j + 1)] * jnp.dot(y_ref[...].astype(BF16), w_ref[...],
                                                            preferred_element_type=F32)

    merged = br(yp_ref, wp_ref, 0) + br(ym_ref, wmb_ref, 1) + br(yg_ref, wgl_ref, 2)
    z = jnp.dot(merged.astype(BF16), wo_ref[...], preferred_element_type=F32)
    x1 = x_ref[...] + _rms(z, gpost_ref[...])
    hb = _rms(x1, gfpre_ref[...]).astype(BF16)
    f = jnp.zeros(x1.shape, F32)
    FC = 1024
    for c in range(D_FF // FC):
        a = jnp.maximum(jnp.dot(hb, w1_ref[:, c * FC:(c + 1) * FC], preferred_element_type=F32), 0.0)
        f = f + jnp.dot((a * a).astype(BF16), w2_ref[c * FC:(c + 1) * FC, :], preferred_element_type=F32)
    o_ref[...] = x1 + _rms(f, gfpost_ref[...])


def _merge_ffn(x, yp, ym, yg, sg, wp, wmb, wgl, wo, gpost, gfpre, gfpost, w1, w2):
    M = x.shape[0]
    tm = min(FFN_ROW_TILE, M)
    row = lambda w: pl.BlockSpec((tm, w), lambda i: (i, 0))
    vec = _const_spec((1, D_MODEL))
    return pl.pallas_call(
        _merge_ffn_kernel,
        grid=(M // tm,),
        in_specs=[row(D_MODEL), row(POOL_W), row(MB_W), row(GLA_VW), row(3 * D_MODEL),
                  _const_spec(wp.shape), _const_spec(wmb.shape), _const_spec(wgl.shape), _const_spec(wo.shape),
                  vec, vec, vec, _const_spec(w1.shape), _const_spec(w2.shape)],
        out_specs=row(D_MODEL),
        out_shape=jax.ShapeDtypeStruct((M, D_MODEL), F32),
        compiler_params=_cparams(("arbitrary",)),
        name="merge_ffn",
    )(x, yp, ym, yg, sg, wp, wmb, wgl, wo, gpost, gfpre, gfpost, w1, w2)


def _layer(x, Bn, T, tabs, hist16, pos0, s0, sample_attend, lw, prompt_state=None):
    (wm, wa, wg, wa2, ba, gn, wgrp, pscale, wp, wmb, wgl, wo, gpre, gpost, gfpre, gfpost, w1, w2) = lw
    seq = lambda a: a.reshape(Bn, T, a.shape[-1])
    if sample_attend is None:
        depth, layer = prompt_state[:2]
        u, q, kb, km, k, v, qg, kg, vg, rg, lg, sg = _in_proj(x, gpre, wm, wa, wg, wa2, ba, tabs,
                                                             prompt_state=(T,) + tuple(prompt_state))
        ym = _moba_prompt(seq(q), km.reshape(Bn, T // MB_BLOCK, MB_W), seq(kb), v, layer)
    else:
        u, q, k, v, qg, kg, vg, rg, lg, sg = _in_proj(x, gpre, wm, wa, wg, wa2, ba, tabs)
        ym = sample_attend(seq(q), seq(k), seq(v))
        k, v = (a.reshape(Bn, T, MB_HEADS, MB_HD) for a in (k, v))
    u3 = seq(u)
    yp = _pool(u3, hist16, wgrp, pscale, pos0)
    Tp = -(-T // GLA_CHUNK) * GLA_CHUNK
    padt = lambda a: jnp.pad(seq(a), ((0, 0), (0, Tp - T), (0, 0))) if Tp != T else seq(a)
    yg, s_new = _gla(padt(qg), padt(kg), padt(vg), padt(rg), padt(lg), s0, gn)
    yg = yg[:, :T].reshape(Bn * T, GLA_VW)
    x = _merge_ffn(x, yp.reshape(Bn * T, POOL_W), ym.reshape(Bn * T, MB_W), yg, sg,
                   wp, wmb, wgl, wo, gpost, gfpre, gfpost, w1, w2)
    new_hist = jnp.concatenate([hist16[:, 1:], u3], axis=1)[:, -POOL_HIST:]
    return x, k, v, new_hist, s_new


def kernel(x_prompt, x_sample, cache_k, cache_v, state_pool, state_gla, page_table, w_in, w_gla_a2, b_gla_a,
           gla_norm, w_pool_grp, pool_scale, w_pool_out, w_mb_out, w_gla_out, w_o, norm_mix_pre,
           norm_mix_post, norm_ffn_pre, norm_ffn_post, w_ff1, w_ff2):
    B, T, D = x_prompt.shape
    DB, TS, _ = x_sample.shape
    depth = w_in.shape[0]
    n_pages = page_table.shape[1]
    past_len = n_pages * PAGE_SIZE
    assert D == D_MODEL and T % MB_BLOCK == 0 and past_len % MB_BLOCK == 0
    assert past_len // MB_BLOCK >= MB_TOPK and (B * T) % ROW_TILE == 0 and MB_BLOCK == 2 * PAGE_SIZE

    tabs_p = _rope_tables(jnp.arange(T))
    reps = max(1, min(ROW_TILE, DB * TS) // TS)
    tabs_s = tuple(jnp.tile(t, (reps, 1)) for t in _rope_tables(past_len + jnp.arange(TS)))
    ckt = jnp.transpose(cache_k, (0, 1, 3, 4, 2))
    cvt = jnp.transpose(cache_v, (0, 1, 3, 4, 2))
    zrow = lambda n: jnp.zeros((n, 1, POOL_W), F32)
    hist_p = jnp.zeros((B, 16, POOL_W), F32)
    s0_p = jnp.zeros((B, GLA_HEADS, GLA_DK, GLA_DV), F32)

    hp = x_prompt.reshape(B * T, D)
    hs = x_sample.reshape(DB * TS, D)
    outs_p, outs_s = [], []
    kt_all = vt_all = None
    for l in range(depth):
        bf = lambda a: a.astype(BF16)
        vec = lambda a: a.reshape(1, -1)
        lw = (bf(w_in[l][:, :W_MAIN]), bf(w_in[l][:, W_MAIN:W_MAIN + GLA_RANK]), bf(w_in[l][:, W_MAIN + GLA_RANK:]),
              bf(w_gla_a2[l]), vec(b_gla_a[l]), vec(gla_norm[l]), bf(w_pool_grp[l]), vec(pool_scale[l]),
              bf(w_pool_out[l]), bf(w_mb_out[l]), bf(w_gla_out[l]), bf(w_o[l]),
              vec(norm_mix_pre[l]), vec(norm_mix_post[l]), vec(norm_ffn_pre[l]), vec(norm_ffn_post[l]),
              bf(w_ff1[l]), bf(w_ff2[l]))
        wvt = bf(w_in[l][:, POOL_W + 2 * MB_W:POOL_W + 3 * MB_W].T)
        hp, kt_all, vt_all, *op = _layer(hp, B, T, tabs_p, hist_p, 0, s0_p, None, lw,
                                         prompt_state=(depth, l, wvt, kt_all, vt_all))
        hist_s = jnp.concatenate([zrow(DB), state_pool[l]], axis=1)
        attend_s = functools.partial(_moba_sample, cache_kt=ckt, cache_vt=cvt, page_table=page_table, layer=l)
        hs, *os_ = _layer(hs, DB, TS, tabs_s, hist_s, past_len, state_gla[l], attend_s, lw)
        outs_p.append(op)
        outs_s.append(os_)
    stack = lambda outs, i: jnp.stack([o[i] for o in outs])
    untr = lambda a: jnp.transpose(a.reshape(depth, B, MB_HEADS, MB_HD, T), (0, 1, 4, 2, 3))
    return (hp.reshape(B, T, D), hs.reshape(DB, TS, D),
            untr(kt_all), untr(vt_all), stack(outs_p, 0), stack(outs_p, 1),
            stack(outs_s, 0), stack(outs_s, 1), stack(outs_s, 2), stack(outs_s, 3))
```

```python
import functools

import jax
import jax.numpy as jnp
from jax import lax
from jax.experimental import pallas as pl
from jax.experimental.pallas import tpu as pltpu

F32 = jnp.float32
BF16 = jnp.bfloat16

D_MODEL = 1024
PAGE_SIZE = 128
POOL_GC = 128
POOL_W = 512
POOL_WINDOWS = (2, 4, 8, 16)
POOL_HIST = 15
MB_HEADS = 8
MB_HD = 64
MB_W = 512
MB_BLOCK = 256
MB_TOPK = 3
ROPE_THETA = 500000.0
ROPE_DIM = 16
GLA_HEADS = 4
GLA_DK = 64
GLA_DV = 128
GLA_KW = 256
GLA_VW = 512
GLA_RANK = 16
GLA_NORMALIZER = 16.0
GLA_CHUNK = 64
D_FF = 4096
EPS = 1e-6
W_MAIN = POOL_W + 3 * MB_W + 2 * GLA_KW + 2 * GLA_VW
NEG_INF = float("-inf")

VMEM_LIMIT_BYTES = 56 * 1024 * 1024
ROW_TILE = 256
FFN_ROW_TILE = 512
FFN_SUB_ROWS = 256
SAMPLE_BLOCKS_PER_STEP = 4
GLA_GROUP = 4


def _cparams(sem):
    return pltpu.CompilerParams(dimension_semantics=sem, vmem_limit_bytes=VMEM_LIMIT_BYTES)


def _const_spec(shape):
    nd = len(shape)
    return pl.BlockSpec(shape, lambda *_: (0,) * nd, pipeline_mode=pl.Buffered(1))


def _rms(x, g):
    ms = jnp.mean(x * x, axis=-1, keepdims=True)
    return x * lax.rsqrt(ms + EPS) * g


def _sigmoid(x):
    return 1.0 / (1.0 + jnp.exp(-x))


def _split3(x):
    hi = x.astype(BF16)
    r = x - hi.astype(F32)
    mid = r.astype(BF16)
    lo = (r - mid.astype(F32)).astype(BF16)
    return hi, mid, lo


def _in_proj_kernel(*refs, prompt, first_layer):
    x_ref, gpre_ref, wm_ref, wa_ref, wg_ref, wa2_ref, ba_ref, cos_ref, sa_ref, sb_ref = refs[:10]
    n_in = 10 + (0 if not prompt else 1 if first_layer else 3)
    wvt_ref = refs[10] if prompt else None
    outs = refs[n_in:]
    u_ref, q_ref = outs[:2]
    qg_ref, kg_ref, vg_ref, rg_ref, lg_ref, sg_ref = outs[-6:]
    h = _rms(x_ref[...], gpre_ref[...]).astype(BF16)

    def proj(w_ref, a, b):
        return jnp.dot(h, w_ref[:, a:b], preferred_element_type=F32)

    cos, sa, sb = cos_ref[...], sa_ref[...], sb_ref[...]

    def rope(col0):
        half = ROPE_DIM // 2
        for j in range(MB_W // 128):
            xj = proj(wm_ref, col0 + 128 * j, col0 + 128 * (j + 1))
            yield j, xj * cos + pltpu.roll(xj, 128 - half, 1) * sa + pltpu.roll(xj, half, 1) * sb

    u_ref[...] = proj(wm_ref, 0, 512)
    for j, rj in rope(512):
        q_ref[:, 128 * j:128 * (j + 1)] = rj
    if prompt:
        kb_ref, km_ref, kt_ref, vt_ref = outs[2:6]
        col_sums = []
        for j, rj in rope(1024):
            kb_ref[:, 128 * j:128 * (j + 1)] = rj.astype(BF16)
            kt_ref[0, 0, 128 * j:128 * (j + 1), :] = rj.T
            col_sums.append(jnp.sum(rj, axis=0, keepdims=True))
        km_ref[0] = jnp.concatenate(col_sums, axis=1) * (1.0 / MB_BLOCK)
        vt_ref[0, 0] = _dot_nt(wvt_ref[...], h)
        for d in range(1, kt_ref.shape[0]):
            kt_ref[d] = jnp.zeros(kt_ref.shape[1:], F32)
            vt_ref[d] = jnp.zeros(vt_ref.shape[1:], F32)
    else:
        k_ref, v_ref = outs[2:4]
        for j, rj in rope(1024):
            k_ref[:, 128 * j:128 * (j + 1)] = rj
        v_ref[...] = proj(wm_ref, 1536, 2048)
    qg_ref[...] = proj(wm_ref, 2048, 2304)
    kg_ref[...] = proj(wm_ref, 2304, 2560)
    vg_ref[...] = proj(wm_ref, 2560, 3072)
    r = proj(wm_ref, 3072, 3584)
    rg_ref[...] = r * _sigmoid(r)
    a = jnp.dot(h, wa_ref[...], preferred_element_type=F32)
    xg = jnp.dot(a.astype(BF16), wa2_ref[...], preferred_element_type=F32) + ba_ref[...]
    lg_ref[...] = (jnp.minimum(xg, 0.0) - jnp.log1p(jnp.exp(-jnp.abs(xg)))) * (1.0 / GLA_NORMALIZER)
    for j in range(3):
        sg_ref[:, 1024 * j:1024 * (j + 1)] = _sigmoid(proj(wg_ref, 1024 * j, 1024 * (j + 1))).astype(BF16)


def _in_proj(x, gpre, wm, wa, wg, wa2, ba, tabs, prompt_state=None):
    M = x.shape[0]
    tm = min(ROW_TILE, M)
    period = tabs[0].shape[0] // tm
    row = lambda w: pl.BlockSpec((tm, w), lambda i: (i, 0))
    tab = pl.BlockSpec((tm, 128), lambda i: (i % period, 0))
    f32 = lambda w: jax.ShapeDtypeStruct((M, w), F32)
    tail_w = (GLA_KW, GLA_KW, GLA_VW, GLA_VW, GLA_KW, 3 * D_MODEL)
    in_specs = [row(D_MODEL), _const_spec((1, D_MODEL)), _const_spec(wm.shape), _const_spec(wa.shape),
                _const_spec(wg.shape), _const_spec(wa2.shape), _const_spec((1, GLA_KW)), tab, tab, tab]
    args = [x, gpre, wm, wa, wg, wa2, ba, *tabs]
    aliases = {}
    if prompt_state is None:
        out_specs = [row(POOL_W), row(MB_W), row(MB_W), row(MB_W)]
        out_shape = [f32(POOL_W), f32(MB_W), f32(MB_W), f32(MB_W)]
        first = False
    else:
        seq_len, depth, layer, wvt, kt_all, vt_all = prompt_state
        assert tm == MB_BLOCK and seq_len % tm == 0
        tps = seq_len // tm
        first = layer == 0
        stacked = jax.ShapeDtypeStruct((depth, M // seq_len, MB_W, seq_len), F32)
        in_specs.append(_const_spec(wvt.shape))
        args.append(wvt)
        if first:
            st_spec = pl.BlockSpec((depth, 1, MB_W, tm), lambda i: (0, i // tps, 0, i % tps))
        else:
            st_spec = pl.BlockSpec((1, 1, MB_W, tm), lambda i: (layer, i // tps, 0, i % tps))
            in_specs += [pl.BlockSpec(memory_space=pl.ANY)] * 2
            args += [kt_all, vt_all]
            aliases = {11: 4, 12: 5}
        out_specs = [row(POOL_W), row(MB_W), row(MB_W), pl.BlockSpec((1, 1, MB_W), lambda i: (i, 0, 0)),
                     st_spec, st_spec]
        out_shape = [f32(POOL_W), f32(MB_W), jax.ShapeDtypeStruct((M, MB_W), BF16),
                     jax.ShapeDtypeStruct((M // tm, 1, MB_W), F32), stacked, stacked]
    return pl.pallas_call(
        functools.partial(_in_proj_kernel, prompt=prompt_state is not None, first_layer=first),
        grid=(M // tm,),
        in_specs=in_specs,
        out_specs=out_specs + [row(w) for w in tail_w],
        out_shape=out_shape + [f32(w) for w in tail_w[:-1]] + [jax.ShapeDtypeStruct((M, tail_w[-1]), BF16)],
        input_output_aliases=aliases,
        compiler_params=_cparams(("arbitrary",)),
        name="in_proj",
    )(*args)


def _rope_tables(pos):
    half = ROPE_DIM // 2
    inv = ROPE_THETA ** (-jnp.arange(half, dtype=F32) / half)
    ang = pos.astype(F32)[:, None] * inv[None, :]
    cos, sin = jnp.cos(ang), jnp.sin(ang)
    n = pos.shape[0]
    zeros = jnp.zeros((n, MB_HD - ROPE_DIM), F32)
    zh = jnp.zeros((n, half), F32)
    c = jnp.concatenate([cos, cos, zeros + 1.0], axis=1)
    sa = jnp.concatenate([-sin, zh, zeros], axis=1)
    sb = jnp.concatenate([zh, sin, zeros], axis=1)
    two = lambda t: jnp.concatenate([t, t], axis=1)
    return two(c), two(sa), two(sb)


def _pool_kernel(u_ref, hist_ref, wgrp_ref, scale_ref, y_ref, ext_ref, *, pos0, T, R):
    ext_ref[0:16, :] = hist_ref[0]
    ext_ref[16:16 + T, :] = u_ref[0]
    for c in range(T // R):
        r0 = 16 + c * R
        posp1 = lax.broadcasted_iota(jnp.int32, (R, 1), 0) + (pos0 + c * R + 1)
        for g, w in enumerate(POOL_WINDOWS):
            lanes = slice(g * POOL_GC, (g + 1) * POOL_GC)
            cur = ext_ref[r0:r0 + R, lanes]
            acc = cur
            for j in range(1, w):
                acc = acc + ext_ref[r0 - j:r0 - j + R, lanes]
            cnt = jnp.minimum(posp1, w).astype(F32)
            d = acc / cnt - cur
            yg = jnp.dot(d.astype(BF16), wgrp_ref[g], preferred_element_type=F32)
            y_ref[0, c * R:(c + 1) * R, lanes] = (yg * scale_ref[:, lanes]).astype(BF16)


def _pool(u, hist16, wgrp, scale, pos0):
    Bn, T, _ = u.shape
    R = min(T, 256)
    seq = pl.BlockSpec((1, T, POOL_W), lambda b: (b, 0, 0))
    return pl.pallas_call(
        functools.partial(_pool_kernel, pos0=pos0, T=T, R=R),
        grid=(Bn,),
        in_specs=[seq, pl.BlockSpec((1, 16, POOL_W), lambda b: (b, 0, 0)),
                  _const_spec(wgrp.shape), _const_spec((1, POOL_W))],
        out_specs=seq,
        out_shape=jax.ShapeDtypeStruct((Bn, T, POOL_W), BF16),
        scratch_shapes=[pltpu.VMEM((16 + T, POOL_W), F32)],
        compiler_params=_cparams(("arbitrary",)),
        name="pool",
    )(u, hist16, wgrp, scale)


def _top_blocks(gate, n_valid, axis):
    n_blocks = gate.shape[axis]
    iota = lax.broadcasted_iota(jnp.int32, gate.shape, axis)
    g = jnp.where(iota < n_valid, gate, NEG_INF)
    sel = jnp.zeros(gate.shape, F32)
    for _ in range(MB_TOPK):
        m = jnp.max(g, axis=axis, keepdims=True)
        idx = jnp.min(jnp.where(g == m, iota, n_blocks), axis=axis, keepdims=True)
        pick = (iota == idx) & (m > NEG_INF)
        sel = jnp.where(pick, 1.0, sel)
        g = jnp.where(pick, NEG_INF, g)
    return sel


def _col_max(s):
    r, c = s.shape
    return jnp.max(jnp.max(s.reshape(r // 8, 8, c), axis=0), axis=0, keepdims=True)


def _dot_nt(a, b, precision=None):
    return lax.dot_general(a, b, (((1,), (1,)), ((), ())), preferred_element_type=F32, precision=precision)


def _moba_prompt_kernel(q_ref, km_ref, k_ref, vt_ref, y_ref, s_ref, vtb_ref):
    B = MB_BLOCK
    n_blocks = km_ref.shape[1]
    qlane = lax.broadcasted_iota(jnp.int32, (B, 128), 1) // MB_HD
    kmlane = lax.broadcasted_iota(jnp.int32, (n_blocks, 128), 1) // MB_HD
    causal = lax.broadcasted_iota(jnp.int32, (B, B), 0) <= lax.broadcasted_iota(jnp.int32, (B, B), 1)
    drow = lax.broadcasted_iota(jnp.int32, (128, B), 0) // MB_HD
    vtb_ref[...] = vt_ref[0, 0].astype(BF16)

    def scores_stage(c):
        q = q_ref[0, c * B:(c + 1) * B, :]
        qh = [jnp.where(qlane == hh, q, 0.0) for hh in range(2)]
        qs = [(x * (MB_HD ** -0.5)).astype(BF16) for x in qh]
        raw = {(hh, j): _dot_nt(k_ref[0, j * B:(j + 1) * B, :], qs[hh])
               for hh in range(2) for j in range(c + 1)}
        maxima = []
        for hh in range(2):
            if c > 0:
                gate_t = _dot_nt(jnp.where(kmlane == hh, km_ref[0], 0.0), qh[hh], precision=lax.Precision.HIGHEST)
                sel = jnp.where(_top_blocks(gate_t, c, axis=0) > 0.0, 0.0, NEG_INF)
            m = None
            for j in range(c + 1):
                s = jnp.where(causal, raw[hh, j], NEG_INF) if j == c else raw[hh, j] + sel[j:j + 1, :]
                s_ref[2 * (c % 2) + hh, j] = s
                mj = _col_max(s)
                m = mj if m is None else jnp.maximum(m, mj)
            maxima.append(m)
        return maxima

    def values_stage(c, maxima):
        outs = []
        for hh in range(2):
            l = jnp.zeros((1, B), F32)
            acc = jnp.zeros((128, B), F32)
            for j in range(c + 1):
                p = jnp.exp(s_ref[2 * (c % 2) + hh, j] - maxima[hh])
                l = l + jnp.sum(p, axis=0, keepdims=True)
                acc = acc + jnp.dot(vtb_ref[:, j * B:(j + 1) * B], p.astype(BF16),
                                    preferred_element_type=F32)
            outs.append(acc / l)
        y_ref[0, c * B:(c + 1) * B, :] = jnp.where(drow == 0, outs[0], outs[1]).T.astype(BF16)

    pending = None
    for c in range(n_blocks):
        maxima = scores_stage(c)
        if pending is not None:
            values_stage(*pending)
        pending = (c, maxima)
    values_stage(*pending)


def _moba_prompt(q, km, kb, vt_all, layer):
    Bn, T, _ = q.shape
    n_blocks = T // MB_BLOCK
    seq = pl.BlockSpec((1, T, 128), lambda b, hp: (b, 0, hp))
    return pl.pallas_call(
        _moba_prompt_kernel,
        grid=(Bn, MB_W // 128),
        in_specs=[seq,
                  pl.BlockSpec((1, n_blocks, 128), lambda b, hp: (b, 0, hp)),
                  seq,
                  pl.BlockSpec((1, 1, 128, T), lambda b, hp: (layer, b, hp, 0))],
        out_specs=seq,
        out_shape=jax.ShapeDtypeStruct((Bn, T, MB_W), BF16),
        scratch_shapes=[pltpu.VMEM((4, n_blocks, MB_BLOCK, MB_BLOCK), F32), pltpu.VMEM((128, T), BF16)],
        compiler_params=_cparams(("arbitrary", "arbitrary")),
        name="moba_prompt",
    )(q, km, kb, vt_all)


def _moba_sample_kernel(pt_ref, q_ref, kn_ref, vn_ref, *refs, n_blocks, ts, bps):
    ppb = MB_BLOCK // PAGE_SIZE
    k_refs = refs[:bps * ppb]
    v_refs = refs[bps * ppb:2 * bps * ppb]
    y_ref, qbd_ref, oblk_ref, km_ref, m_ref, l_ref = refs[2 * bps * ppb:]
    step = pl.program_id(1)
    R = MB_HEADS * ts
    own = (lax.broadcasted_iota(jnp.int32, (R, MB_W), 0) // ts
           == lax.broadcasted_iota(jnp.int32, (R, MB_W), 1) // MB_HD)
    blane = lax.broadcasted_iota(jnp.int32, (R, n_blocks), 1)
    kml = lax.broadcasted_iota(jnp.int32, (MB_W, n_blocks), 1)

    @pl.when(step == 0)
    def _():
        for ref in (m_ref, l_ref, km_ref):
            ref[...] = jnp.zeros(ref.shape, F32)
        qbd_ref[...] = jnp.where(own, jnp.concatenate([q_ref[0]] * MB_HEADS, axis=0), 0.0)

    qbd = qbd_ref[...]
    qb = (qbd * (MB_HD ** -0.5)).astype(BF16)
    page = lambda ref: ref[0, 0].reshape(MB_W, PAGE_SIZE)
    m_all, l_all, km_all = m_ref[...], l_ref[...], km_ref[...]
    blocks = lambda refs, i: jnp.concatenate([page(refs[ppb * i + j]) for j in range(ppb)], axis=1)
    scores, probs = [], []
    for i in range(bps):
        kblk = blocks(k_refs, i)
        scores.append(jnp.dot(qb, kblk.astype(BF16), preferred_element_type=F32))
        kmean = jnp.sum(kblk, axis=1, keepdims=True) * (1.0 / MB_BLOCK)
        km_all = jnp.where(kml == step * bps + i, kmean, km_all)
    for i, s in enumerate(scores):
        n = step * bps + i
        m_n = jnp.max(s, axis=1, keepdims=True)
        p = jnp.exp(s - m_n)
        probs.append(p.astype(BF16))
        m_all = jnp.where(blane == n, m_n, m_all)
        l_all = jnp.where(blane == n, jnp.sum(p, axis=1, keepdims=True), l_all)
    for i, p in enumerate(probs):
        oblk_ref[step * bps + i] = _dot_nt(p, blocks(v_refs, i).astype(BF16))
    km_ref[...] = km_all
    m_ref[...] = m_all
    l_ref[...] = l_all

    @pl.when(step == n_blocks // bps - 1)
    def _():
        gate = jnp.dot(qbd, km_ref[...], preferred_element_type=F32, precision=lax.Precision.HIGHEST)
        sel = _top_blocks(gate, n_blocks, axis=1) > 0.0
        s_own = _dot_nt(qb, kn_ref[0].astype(BF16))
        tq = lax.broadcasted_iota(jnp.int32, (R, ts), 0) % ts
        tk = lax.broadcasted_iota(jnp.int32, (R, ts), 1)
        s_own = jnp.where(tk <= tq, s_own, NEG_INF)
        mb = jnp.where(sel, m_ref[...], NEG_INF)
        m_all = jnp.maximum(jnp.max(s_own, axis=1, keepdims=True), jnp.max(mb, axis=1, keepdims=True))
        w = jnp.exp(mb - m_all)
        p_own = jnp.exp(s_own - m_all)
        l = jnp.sum(p_own, axis=1, keepdims=True) + jnp.sum(w * l_ref[...], axis=1, keepdims=True)
        o = jnp.dot(p_own.astype(BF16), vn_ref[0].astype(BF16), preferred_element_type=F32)
        for nb in range(n_blocks):
            o = o + w[:, nb:nb + 1] * oblk_ref[nb]
        o = jnp.where(own, o / l, 0.0)
        y = o[0:ts]
        for h in range(1, MB_HEADS):
            y = y + o[h * ts:(h + 1) * ts]
        y_ref[0] = y.astype(BF16)


def _moba_sample(q, kn, vn, cache_kt, cache_vt, page_table, layer):
    DB, ts, _ = q.shape
    n_pages = page_table.shape[1]
    ppb = MB_BLOCK // PAGE_SIZE
    n_blocks = n_pages // ppb
    R = MB_HEADS * ts
    tok = pl.BlockSpec((1, ts, MB_W), lambda b, n, pt: (b, 0, 0))
    bps = SAMPLE_BLOCKS_PER_STEP
    assert n_blocks % bps == 0
    pps = bps * ppb

    def page(i):
        return pl.BlockSpec((1, 1, MB_HEADS, MB_HD, PAGE_SIZE),
                            lambda b, n, pt: (layer, pt[b * n_pages + n * pps + i], 0, 0, 0))

    pages = [page(i) for i in range(pps)]
    grid_spec = pltpu.PrefetchScalarGridSpec(
        num_scalar_prefetch=1,
        grid=(DB, n_blocks // bps),
        in_specs=[tok, tok, tok] + pages + pages,
        out_specs=tok,
        scratch_shapes=[pltpu.VMEM((R, MB_W), F32),
                        pltpu.VMEM((n_blocks, R, MB_W), F32),
                        pltpu.VMEM((MB_W, n_blocks), F32),
                        pltpu.VMEM((R, n_blocks), F32),
                        pltpu.VMEM((R, n_blocks), F32)],
    )
    return pl.pallas_call(
        functools.partial(_moba_sample_kernel, n_blocks=n_blocks, ts=ts, bps=bps),
        grid_spec=grid_spec,
        out_shape=jax.ShapeDtypeStruct((DB, ts, MB_W), BF16),
        compiler_params=_cparams(("arbitrary", "arbitrary")),
        name="moba_sample",
    )(page_table.reshape(-1), q, kn, vn, *([cache_kt] * pps), *([cache_vt] * pps))


def _gla_kernel(qg_ref, kg_ref, vg_ref, rg_ref, lg_ref, s0_ref, gn_ref, y_ref, sfin_ref, st_ref, *, n_chunks):
    t = pl.program_id(1)
    C = GLA_CHUNK

    @pl.when(t == 0)
    def _():
        for h in range(GLA_HEADS):
            st_ref[:, h * GLA_DK:(h + 1) * GLA_DK] = s0_ref[0, h].T

    ri = lax.broadcasted_iota(jnp.int32, (C, C), 0)
    ci = lax.broadcasted_iota(jnp.int32, (C, C), 1)
    tri = (ci <= ri).astype(BF16)
    m1 = (ri >= 32) & (ci < 32)
    m2 = (ri // 32 == ci // 32) & (ri % 32 >= 16) & (ci % 32 < 16)
    m3 = (ri // 16 == ci // 16) & (ci <= ri)
    grp = lax.broadcasted_iota(jnp.int32, (C, GLA_KW), 0)

    heads = [(slice(h * GLA_DK, (h + 1) * GLA_DK), slice(h * GLA_DV, (h + 1) * GLA_DV)) for h in range(GLA_HEADS)]
    sts = [st_ref[:, kl] for kl, _ in heads]
    for g0 in range(0, n_chunks, GLA_GROUP):
        cs = range(g0, min(g0 + GLA_GROUP, n_chunks))
        rows = {c: slice(c * C, (c + 1) * C) for c in cs}
        bcum = {}
        for c in cs:
            hi, mid, lo = _split3(lg_ref[0, rows[c], :])
            bcum[c] = (jnp.dot(tri, hi, preferred_element_type=F32) + jnp.dot(tri, mid, preferred_element_type=F32)
                       + jnp.dot(tri, lo, preferred_element_type=F32))
        scaled = {}
        for c in cs:
            b = bcum[c]
            q = qg_ref[0, rows[c], :] * (GLA_DK ** -0.5)
            k = kg_ref[0, rows[c], :]
            brow = lambda i: jnp.broadcast_to(b[i:i + 1, :], (C, GLA_KW))
            ref1 = brow(31)
            ref2 = jnp.where(grp < 32, brow(15), brow(47))
            ref3 = jnp.where(grp < 16, brow(7),
                             jnp.where(grp < 32, brow(23), jnp.where(grp < 48, brow(39), brow(55))))
            b_last = b[C - 1:C, :]
            scaled[c] = dict(
                qs=(q * jnp.exp(b)).astype(BF16), ks=(k * jnp.exp(b_last - b)).astype(BF16),
                q1=(q * jnp.exp(jnp.minimum(b - ref1, 0.0))).astype(BF16),
                k1=(k * jnp.exp(jnp.minimum(ref1 - b, 0.0))).astype(BF16),
                q2=(q * jnp.exp(jnp.minimum(b - ref2, 0.0))).astype(BF16),
                k2=(k * jnp.exp(jnp.minimum(ref2 - b, 0.0))).astype(BF16),
                q3=(q * jnp.exp(b - ref3)).astype(BF16), k3=(k * jnp.exp(ref3 - b)).astype(BF16),
                decay=jnp.exp(b_last))
        levels = {}
        for c in cs:
            x = scaled[c]
            for h, (kl, _) in enumerate(heads):
                levels[c, h] = (_dot_nt(x["q1"][:, kl], x["k1"][:, kl]), _dot_nt(x["q2"][:, kl], x["k2"][:, kl]),
                                _dot_nt(x["q3"][:, kl], x["k3"][:, kl]))
        intra, update = {}, {}
        for c in cs:
            for h, (kl, vl) in enumerate(heads):
                l1, l2, l3 = levels[c, h]
                a = jnp.where(m1, l1, 0.0) + jnp.where(m2, l2, 0.0) + jnp.where(m3, l3, 0.0)
                vf = vg_ref[0, rows[c], vl]
                intra[c, h] = jnp.dot(a.astype(BF16), vf.astype(BF16), preferred_element_type=F32)
                update[c, h] = jnp.dot(vf.T.astype(BF16), scaled[c]["ks"][:, kl], preferred_element_type=F32)
        for c in cs:
            for h, (kl, vl) in enumerate(heads):
                o = _dot_nt(scaled[c]["qs"][:, kl], sts[h].astype(BF16)) + intra[c, h]
                sts[h] = sts[h] * scaled[c]["decay"][:, kl] + update[c, h]
                y_ref[0, rows[c], vl] = (_rms(o, gn_ref[:, vl]) * rg_ref[0, rows[c], vl]).astype(BF16)
    for (kl, _), st in zip(heads, sts):
        st_ref[:, kl] = st

    @pl.when(t == pl.num_programs(1) - 1)
    def _():
        for h in range(GLA_HEADS):
            sfin_ref[0, h] = st_ref[:, h * GLA_DK:(h + 1) * GLA_DK].T


def _gla(qg, kg, vg, rg, lg, s0, gn):
    Bn, T, _ = qg.shape
    TT = min(T, 512)
    kspec = pl.BlockSpec((1, TT, GLA_KW), lambda b, t: (b, t, 0))
    vspec = pl.BlockSpec((1, TT, GLA_VW), lambda b, t: (b, t, 0))
    sspec = pl.BlockSpec((1, GLA_HEADS, GLA_DK, GLA_DV), lambda b, t: (b, 0, 0, 0))
    return pl.pallas_call(
        functools.partial(_gla_kernel, n_chunks=TT // GLA_CHUNK),
        grid=(Bn, T // TT),
        in_specs=[kspec, kspec, vspec, vspec, kspec, sspec, _const_spec((1, GLA_VW))],
        out_specs=[vspec, sspec],
        out_shape=[jax.ShapeDtypeStruct((Bn, T, GLA_VW), BF16),
                   jax.ShapeDtypeStruct((Bn, GLA_HEADS, GLA_DK, GLA_DV), F32)],
        scratch_shapes=[pltpu.VMEM((GLA_DV, GLA_KW), F32)],
        compiler_params=_cparams(("arbitrary", "arbitrary")),
        name="gla",
    )(qg, kg, vg, rg, lg, s0, gn)


def _merge_ffn_kernel(x_ref, yp_ref, ym_ref, yg_ref, sg_ref, wp_ref, wmb_ref, wgl_ref, wo_ref,
                      gpost_ref, gfpre_ref, gfpost_ref, w1_ref, w2_ref, o_ref):
    tm = x_ref.shape[0]
    subs = [slice(r, r + FFN_SUB_ROWS) for r in range(0, tm, FFN_SUB_ROWS)] if tm > FFN_SUB_ROWS else [slice(0, tm)]

    def dot(a, w):
        return jnp.dot(a.astype(BF16), w, preferred_element_type=F32)

    branches = [[dot(y_ref[r, :], w_ref[...]) for y_ref, w_ref in
                 ((yp_ref, wp_ref), (ym_ref, wmb_ref), (yg_ref, wgl_ref))] for r in subs]
    merged = [sum(sg_ref[r, 1024 * j:1024 * (j + 1)] * b for j, b in enumerate(bs))
              for r, bs in zip(subs, branches)]
    z = [dot(m, wo_ref[...]) for m in merged]
    x1 = [x_ref[r, :] + _rms(zi, gpost_ref[...]) for r, zi in zip(subs, z)]
    hb = [_rms(xi, gfpre_ref[...]).astype(BF16) for xi in x1]
    FC = 1024
    f = [jnp.zeros(xi.shape, F32) for xi in x1]

    def second(i, c, a):
        a = jnp.maximum(a, 0.0)
        f[i] = f[i] + dot(a * a, w2_ref[c * FC:(c + 1) * FC, :])

    pending = None
    for c in range(D_FF // FC):
        for i in range(len(subs)):
            a = jnp.dot(hb[i], w1_ref[:, c * FC:(c + 1) * FC], preferred_element_type=F32)
            if pending is not None:
                second(*pending)
            pending = (i, c, a)
    second(*pending)
    for i, r in enumerate(subs):
        o_ref[r, :] = x1[i] + _rms(f[i], gfpost_ref[...])


def _merge_ffn(x, yp, ym, yg, sg, wp, wmb, wgl, wo, gpost, gfpre, gfpost, w1, w2):
    M = x.shape[0]
    tm = min(FFN_ROW_TILE, M)
    row = lambda w: pl.BlockSpec((tm, w), lambda i: (i, 0))
    vec = _const_spec((1, D_MODEL))
    return pl.pallas_call(
        _merge_ffn_kernel,
        grid=(M // tm,),
        in_specs=[row(D_MODEL), row(POOL_W), row(MB_W), row(GLA_VW), row(3 * D_MODEL),
                  _const_spec(wp.shape), _const_spec(wmb.shape), _const_spec(wgl.shape), _const_spec(wo.shape),
                  vec, vec, vec, _const_spec(w1.shape), _const_spec(w2.shape)],
        out_specs=row(D_MODEL),
        out_shape=jax.ShapeDtypeStruct((M, D_MODEL), F32),
        compiler_params=_cparams(("arbitrary",)),
        name="merge_ffn",
    )(x, yp, ym, yg, sg, wp, wmb, wgl, wo, gpost, gfpre, gfpost, w1, w2)


def _layer(x, Bn, T, tabs, hist16, pos0, s0, sample_attend, lw, prompt_state=None):
    (wm, wa, wg, wa2, ba, gn, wgrp, pscale, wp, wmb, wgl, wo, gpre, gpost, gfpre, gfpost, w1, w2) = lw
    seq = lambda a: a.reshape(Bn, T, a.shape[-1])
    if sample_attend is None:
        depth, layer = prompt_state[:2]
        u, q, kb, km, k, v, qg, kg, vg, rg, lg, sg = _in_proj(x, gpre, wm, wa, wg, wa2, ba, tabs,
                                                             prompt_state=(T,) + tuple(prompt_state))
        ym = _moba_prompt(seq(q), km.reshape(Bn, T // MB_BLOCK, MB_W), seq(kb), v, layer)
    else:
        u, q, k, v, qg, kg, vg, rg, lg, sg = _in_proj(x, gpre, wm, wa, wg, wa2, ba, tabs)
        ym = sample_attend(seq(q), seq(k), seq(v))
        k, v = (a.reshape(Bn, T, MB_HEADS, MB_HD) for a in (k, v))
    u3 = seq(u)
    yp = _pool(u3, hist16, wgrp, pscale, pos0)
    Tp = -(-T // GLA_CHUNK) * GLA_CHUNK
    padt = lambda a: jnp.pad(seq(a), ((0, 0), (0, Tp - T), (0, 0))) if Tp != T else seq(a)
    yg, s_new = _gla(padt(qg), padt(kg), padt(vg), padt(rg), padt(lg), s0, gn)
    yg = yg[:, :T].reshape(Bn * T, GLA_VW)
    x = _merge_ffn(x, yp.reshape(Bn * T, POOL_W), ym.reshape(Bn * T, MB_W), yg, sg,
                   wp, wmb, wgl, wo, gpost, gfpre, gfpost, w1, w2)
    new_hist = jnp.concatenate([hist16[:, 1:], u3], axis=1)[:, -POOL_HIST:]
    return x, k, v, new_hist, s_new


def kernel(x_prompt, x_sample, cache_k, cache_v, state_pool, state_gla, page_table, w_in, w_gla_a2, b_gla_a,
           gla_norm, w_pool_grp, pool_scale, w_pool_out, w_mb_out, w_gla_out, w_o, norm_mix_pre,
           norm_mix_post, norm_ffn_pre, norm_ffn_post, w_ff1, w_ff2):
    B, T, D = x_prompt.shape
    DB, TS, _ = x_sample.shape
    depth = w_in.shape[0]
    n_pages = page_table.shape[1]
    past_len = n_pages * PAGE_SIZE
    assert D == D_MODEL and T % MB_BLOCK == 0 and past_len % MB_BLOCK == 0
    assert past_len // MB_BLOCK >= MB_TOPK and (B * T) % ROW_TILE == 0 and MB_BLOCK == 2 * PAGE_SIZE

    tabs_p = _rope_tables(jnp.arange(T))
    reps = max(1, min(ROW_TILE, DB * TS) // TS)
    tabs_s = tuple(jnp.tile(t, (reps, 1)) for t in _rope_tables(past_len + jnp.arange(TS)))
    ckt = jnp.transpose(cache_k, (0, 1, 3, 4, 2))
    cvt = jnp.transpose(cache_v, (0, 1, 3, 4, 2))
    zrow = lambda n: jnp.zeros((n, 1, POOL_W), F32)
    hist_p = jnp.zeros((B, 16, POOL_W), F32)
    s0_p = jnp.zeros((B, GLA_HEADS, GLA_DK, GLA_DV), F32)

    hp = x_prompt.reshape(B * T, D)
    hs = x_sample.reshape(DB * TS, D)
    outs_p, outs_s = [], []
    kt_all = vt_all = None
    for l in range(depth):
        bf = lambda a: a.astype(BF16)
        vec = lambda a: a.reshape(1, -1)
        lw = (bf(w_in[l][:, :W_MAIN]), bf(w_in[l][:, W_MAIN:W_MAIN + GLA_RANK]), bf(w_in[l][:, W_MAIN + GLA_RANK:]),
              bf(w_gla_a2[l]), vec(b_gla_a[l]), vec(gla_norm[l]), bf(w_pool_grp[l]), vec(pool_scale[l]),
              bf(w_pool_out[l]), bf(w_mb_out[l]), bf(w_gla_out[l]), bf(w_o[l]),
              vec(norm_mix_pre[l]), vec(norm_mix_post[l]), vec(norm_ffn_pre[l]), vec(norm_ffn_post[l]),
              bf(w_ff1[l]), bf(w_ff2[l]))
        wvt = bf(w_in[l][:, POOL_W + 2 * MB_W:POOL_W + 3 * MB_W].T)
        hp, kt_all, vt_all, *op = _layer(hp, B, T, tabs_p, hist_p, 0, s0_p, None, lw,
                                         prompt_state=(depth, l, wvt, kt_all, vt_all))
        hist_s = jnp.concatenate([zrow(DB), state_pool[l]], axis=1)
        attend_s = functools.partial(_moba_sample, cache_kt=ckt, cache_vt=cvt, page_table=page_table, layer=l)
        hs, *os_ = _layer(hs, DB, TS, tabs_s, hist_s, past_len, state_gla[l], attend_s, lw)
        outs_p.append(op)
        outs_s.append(os_)
    stack = lambda outs, i: jnp.stack([o[i] for o in outs])
    untr = lambda a: jnp.transpose(a.reshape(depth, B, MB_HEADS, MB_HD, T), (0, 1, 4, 2, 3))
    return (hp.reshape(B, T, D), hs.reshape(DB, TS, D),
            untr(kt_all), untr(vt_all), stack(outs_p, 0), stack(outs_p, 1),
            stack(outs_s, 0), stack(outs_s, 1), stack(outs_s, 2), stack(outs_s, 3))
```

```python
import functools

import jax
import jax.numpy as jnp
from jax import lax
from jax.experimental import pallas as pl
from jax.experimental.pallas import tpu as pltpu

F32 = jnp.float32
BF16 = jnp.bfloat16

D_MODEL = 1024
PAGE_SIZE = 128
POOL_GC = 128
POOL_W = 512
POOL_WINDOWS = (2, 4, 8, 16)
POOL_HIST = 15
MB_HEADS = 8
MB_HD = 64
MB_W = 512
MB_BLOCK = 256
MB_TOPK = 3
ROPE_THETA = 500000.0
ROPE_DIM = 16
GLA_HEADS = 4
GLA_DK = 64
GLA_DV = 128
GLA_KW = 256
GLA_VW = 512
GLA_RANK = 16
GLA_NORMALIZER = 16.0
GLA_CHUNK = 64
D_FF = 4096
EPS = 1e-6
W_MAIN = POOL_W + 3 * MB_W + 2 * GLA_KW + 2 * GLA_VW
NEG_INF = float("-inf")
LOG2E = 1.4426950408889634

VMEM_LIMIT_BYTES = 56 * 1024 * 1024
ROW_TILE = 512
FFN_ROW_TILE = 512
FFN_SUB_ROWS = 256
SAMPLE_BLOCKS_PER_STEP = 8
GLA_GROUP = 4


def _cparams(sem):
    return pltpu.CompilerParams(dimension_semantics=sem, vmem_limit_bytes=VMEM_LIMIT_BYTES)


def _const_spec(shape):
    nd = len(shape)
    return pl.BlockSpec(shape, lambda *_: (0,) * nd, pipeline_mode=pl.Buffered(1))


def _rms(x, g):
    ms = jnp.mean(x * x, axis=-1, keepdims=True)
    return x * lax.rsqrt(ms + EPS) * g


def _sigmoid(x):
    return 0.5 * jnp.tanh(0.5 * x) + 0.5


def _split3(x):
    hi = x.astype(BF16)
    r = x - hi.astype(F32)
    mid = r.astype(BF16)
    lo = (r - mid.astype(F32)).astype(BF16)
    return hi, mid, lo


def _in_proj_kernel(*refs, prompt, first_layer):
    x_ref, gpre_ref, wm_ref, wa_ref, wg_ref, wa2_ref, ba_ref, cos_ref, sa_ref, sb_ref = refs[:10]
    n_in = 10 + (0 if not prompt else 1 if first_layer else 3)
    wvt_ref = refs[10] if prompt else None
    outs = refs[n_in:]
    u_ref, q_ref = outs[:2]
    qg_ref, kg_ref, vg_ref, rg_ref, lg_ref, sg_ref = outs[-6:]
    h = _rms(x_ref[...], gpre_ref[...]).astype(BF16)

    def proj(w_ref, a, b):
        return jnp.dot(h, w_ref[:, a:b], preferred_element_type=F32)

    cos, sa, sb = cos_ref[...], sa_ref[...], sb_ref[...]

    def rope(col0):
        half = ROPE_DIM // 2
        for j in range(MB_W // 128):
            xj = proj(wm_ref, col0 + 128 * j, col0 + 128 * (j + 1))
            yield j, xj * cos + pltpu.roll(xj, 128 - half, 1) * sa + pltpu.roll(xj, half, 1) * sb

    for j in range(3):
        sg_ref[:, 1024 * j:1024 * (j + 1)] = _sigmoid(proj(wg_ref, 1024 * j, 1024 * (j + 1))).astype(BF16)
    r = proj(wm_ref, 3072, 3584)
    rg_ref[...] = r * _sigmoid(r)
    a = jnp.dot(h, wa_ref[...], preferred_element_type=F32)
    xg = jnp.dot(a.astype(BF16), wa2_ref[...], preferred_element_type=F32) + ba_ref[...]
    lg_ref[...] = (jnp.minimum(xg, 0.0) - jnp.log1p(jnp.exp(-jnp.abs(xg)))) * (1.0 / GLA_NORMALIZER)
    u_ref[...] = proj(wm_ref, 0, 512)
    for j, rj in rope(512):
        q_ref[:, 128 * j:128 * (j + 1)] = rj
    if prompt:
        kb_ref, km_ref, kt_ref, vt_ref = outs[2:6]
        n_blk = km_ref.shape[0]
        col_sums = [[] for _ in range(n_blk)]
        for j, rj in rope(1024):
            kb_ref[:, 128 * j:128 * (j + 1)] = rj.astype(BF16)
            kt_ref[0, 0, 128 * j:128 * (j + 1), :] = rj.T
            for i in range(n_blk):
                col_sums[i].append(jnp.sum(rj[i * MB_BLOCK:(i + 1) * MB_BLOCK], axis=0, keepdims=True))
        for i in range(n_blk):
            km_ref[i] = jnp.concatenate(col_sums[i], axis=1) * (1.0 / MB_BLOCK)
        vt_ref[0, 0] = _dot_nt(wvt_ref[...], h)
        for d in range(1, kt_ref.shape[0]):
            kt_ref[d] = jnp.zeros(kt_ref.shape[1:], F32)
            vt_ref[d] = jnp.zeros(vt_ref.shape[1:], F32)
    else:
        k_ref, v_ref = outs[2:4]
        for j, rj in rope(1024):
            k_ref[:, 128 * j:128 * (j + 1)] = rj
        v_ref[...] = proj(wm_ref, 1536, 2048)
    qg_ref[...] = proj(wm_ref, 2048, 2304)
    kg_ref[...] = proj(wm_ref, 2304, 2560)
    vg_ref[...] = proj(wm_ref, 2560, 3072)


def _in_proj(x, gpre, wm, wa, wg, wa2, ba, tabs, prompt_state=None):
    M = x.shape[0]
    tm = min(ROW_TILE, M)
    period = tabs[0].shape[0] // tm
    row = lambda w: pl.BlockSpec((tm, w), lambda i: (i, 0))
    tab = pl.BlockSpec((tm, 128), lambda i: (i % period, 0))
    f32 = lambda w: jax.ShapeDtypeStruct((M, w), F32)
    tail_w = (GLA_KW, GLA_KW, GLA_VW, GLA_VW, GLA_KW, 3 * D_MODEL)
    in_specs = [row(D_MODEL), _const_spec((1, D_MODEL)), _const_spec(wm.shape), _const_spec(wa.shape),
                _const_spec(wg.shape), _const_spec(wa2.shape), _const_spec((1, GLA_KW)), tab, tab, tab]
    args = [x, gpre, wm, wa, wg, wa2, ba, *tabs]
    aliases = {}
    if prompt_state is None:
        out_specs = [row(POOL_W), row(MB_W), row(MB_W), row(MB_W)]
        out_shape = [f32(POOL_W), f32(MB_W), f32(MB_W), f32(MB_W)]
        first = False
    else:
        seq_len, depth, layer, wvt, kt_all, vt_all = prompt_state
        assert tm % MB_BLOCK == 0 and seq_len % tm == 0
        tps = seq_len // tm
        first = layer == 0
        stacked = jax.ShapeDtypeStruct((depth, M // seq_len, MB_W, seq_len), F32)
        in_specs.append(_const_spec(wvt.shape))
        args.append(wvt)
        if first:
            st_spec = pl.BlockSpec((depth, 1, MB_W, tm), lambda i: (0, i // tps, 0, i % tps))
        else:
            st_spec = pl.BlockSpec((1, 1, MB_W, tm), lambda i: (layer, i // tps, 0, i % tps))
            in_specs += [pl.BlockSpec(memory_space=pl.ANY)] * 2
            args += [kt_all, vt_all]
            aliases = {11: 4, 12: 5}
        out_specs = [row(POOL_W), row(MB_W), row(MB_W),
                     pl.BlockSpec((tm // MB_BLOCK, 1, MB_W), lambda i: (i, 0, 0)),
                     st_spec, st_spec]
        out_shape = [f32(POOL_W), f32(MB_W), jax.ShapeDtypeStruct((M, MB_W), BF16),
                     jax.ShapeDtypeStruct((M // MB_BLOCK, 1, MB_W), F32), stacked, stacked]
    return pl.pallas_call(
        functools.partial(_in_proj_kernel, prompt=prompt_state is not None, first_layer=first),
        grid=(M // tm,),
        in_specs=in_specs,
        out_specs=out_specs + [row(w) for w in tail_w],
        out_shape=out_shape + [f32(w) for w in tail_w[:-1]] + [jax.ShapeDtypeStruct((M, tail_w[-1]), BF16)],
        input_output_aliases=aliases,
        compiler_params=_cparams(("arbitrary",)),
        name="in_proj",
    )(*args)


def _rope_tables(pos):
    half = ROPE_DIM // 2
    inv = ROPE_THETA ** (-jnp.arange(half, dtype=F32) / half)
    ang = pos.astype(F32)[:, None] * inv[None, :]
    cos, sin = jnp.cos(ang), jnp.sin(ang)
    n = pos.shape[0]
    zeros = jnp.zeros((n, MB_HD - ROPE_DIM), F32)
    zh = jnp.zeros((n, half), F32)
    c = jnp.concatenate([cos, cos, zeros + 1.0], axis=1)
    sa = jnp.concatenate([-sin, zh, zeros], axis=1)
    sb = jnp.concatenate([zh, sin, zeros], axis=1)
    two = lambda t: jnp.concatenate([t, t], axis=1)
    return two(c), two(sa), two(sb)


def _pool_kernel(u_ref, hist_ref, wgrp_ref, scale_ref, y_ref, ext_ref, *, pos0, T, R):
    ext_ref[0:16, :] = hist_ref[0]
    ext_ref[16:16 + T, :] = u_ref[0]
    for c in range(T // R):
        r0 = 16 + c * R
        posp1 = lax.broadcasted_iota(jnp.int32, (R, 1), 0) + (pos0 + c * R + 1)
        for g, w in enumerate(POOL_WINDOWS):
            lanes = slice(g * POOL_GC, (g + 1) * POOL_GC)
            cur = ext_ref[r0:r0 + R, lanes]
            acc = cur
            for j in range(1, w):
                acc = acc + ext_ref[r0 - j:r0 - j + R, lanes]
            cnt = jnp.minimum(posp1, w).astype(F32)
            d = acc / cnt - cur
            yg = jnp.dot(d.astype(BF16), wgrp_ref[g], preferred_element_type=F32)
            y_ref[0, c * R:(c + 1) * R, lanes] = (yg * scale_ref[:, lanes]).astype(BF16)


def _pool(u, hist16, wgrp, scale, pos0):
    Bn, T, _ = u.shape
    R = min(T, 256)
    seq = pl.BlockSpec((1, T, POOL_W), lambda b: (b, 0, 0))
    return pl.pallas_call(
        functools.partial(_pool_kernel, pos0=pos0, T=T, R=R),
        grid=(Bn,),
        in_specs=[seq, pl.BlockSpec((1, 16, POOL_W), lambda b: (b, 0, 0)),
                  _const_spec(wgrp.shape), _const_spec((1, POOL_W))],
        out_specs=seq,
        out_shape=jax.ShapeDtypeStruct((Bn, T, POOL_W), BF16),
        scratch_shapes=[pltpu.VMEM((16 + T, POOL_W), F32)],
        compiler_params=_cparams(("arbitrary",)),
        name="pool",
    )(u, hist16, wgrp, scale)


def _top_blocks(gate, n_valid, axis):
    n_blocks = gate.shape[axis]
    iota = lax.broadcasted_iota(jnp.int32, gate.shape, axis)
    g = jnp.where(iota < n_valid, gate, NEG_INF)
    sel = jnp.zeros(gate.shape, F32)
    for _ in range(MB_TOPK):
        m = jnp.max(g, axis=axis, keepdims=True)
        idx = jnp.min(jnp.where(g == m, iota, n_blocks), axis=axis, keepdims=True)
        pick = (iota == idx) & (m > NEG_INF)
        sel = jnp.where(pick, 1.0, sel)
        g = jnp.where(pick, NEG_INF, g)
    return sel


def _col_max(s):
    r, c = s.shape
    return jnp.max(jnp.max(s.reshape(r // 8, 8, c), axis=0), axis=0, keepdims=True)


def _dot_nt(a, b, precision=None):
    return lax.dot_general(a, b, (((1,), (1,)), ((), ())), preferred_element_type=F32, precision=precision)


def _moba_prompt_kernel(q_ref, km_ref, k_ref, vt_ref, y_ref, s_ref, vtb_ref):
    B = MB_BLOCK
    n_blocks = km_ref.shape[1]
    qlane = lax.broadcasted_iota(jnp.int32, (B, 128), 1) // MB_HD
    kmlane = lax.broadcasted_iota(jnp.int32, (n_blocks, 128), 1) // MB_HD
    causal = lax.broadcasted_iota(jnp.int32, (B, B), 0) <= lax.broadcasted_iota(jnp.int32, (B, B), 1)
    drow = lax.broadcasted_iota(jnp.int32, (128, B), 0) // MB_HD
    vtb_ref[...] = vt_ref[0, 0].astype(BF16)

    def scores_stage(c):
        q = q_ref[0, c * B:(c + 1) * B, :]
        qh = [jnp.where(qlane == hh, q, 0.0) for hh in range(2)]
        qs = [(x * (MB_HD ** -0.5 * LOG2E)).astype(BF16) for x in qh]
        raw = {(hh, j): _dot_nt(k_ref[0, j * B:(j + 1) * B, :], qs[hh])
               for hh in range(2) for j in range(c + 1)}
        maxima = []
        for hh in range(2):
            if c > 0:
                gate_t = _dot_nt(jnp.where(kmlane == hh, km_ref[0], 0.0), qh[hh], precision=lax.Precision.HIGHEST)
                sel = jnp.where(_top_blocks(gate_t, c, axis=0) > 0.0, 0.0, NEG_INF)
            m = None
            for j in range(c + 1):
                s = jnp.where(causal, raw[hh, j], NEG_INF) if j == c else raw[hh, j] + sel[j:j + 1, :]
                s_ref[2 * (c % 2) + hh, j] = s
                mj = _col_max(s)
                m = mj if m is None else jnp.maximum(m, mj)
            maxima.append(m)
        return maxima

    def values_stage(c, maxima):
        outs = []
        for hh in range(2):
            l = jnp.zeros((1, B), F32)
            probs = []
            for j in range(c + 1):
                p = jnp.exp2(s_ref[2 * (c % 2) + hh, j] - maxima[hh])
                l = l + jnp.sum(p, axis=0, keepdims=True)
                probs.append(p.astype(BF16))
            acc = jnp.dot(vtb_ref[:, :(c + 1) * B], jnp.concatenate(probs, axis=0), preferred_element_type=F32)
            outs.append(acc / l)
        y_ref[0, c * B:(c + 1) * B, :] = jnp.where(drow == 0, outs[0], outs[1]).T.astype(BF16)

    pending = None
    for c in range(n_blocks):
        maxima = scores_stage(c)
        if pending is not None:
            values_stage(*pending)
        pending = (c, maxima)
    values_stage(*pending)


def _moba_prompt(q, km, kb, vt_all, layer):
    Bn, T, _ = q.shape
    n_blocks = T // MB_BLOCK
    seq = pl.BlockSpec((1, T, 128), lambda b, hp: (b, 0, hp))
    return pl.pallas_call(
        _moba_prompt_kernel,
        grid=(Bn, MB_W // 128),
        in_specs=[seq,
                  pl.BlockSpec((1, n_blocks, 128), lambda b, hp: (b, 0, hp)),
                  seq,
                  pl.BlockSpec((1, 1, 128, T), lambda b, hp: (layer, b, hp, 0))],
        out_specs=seq,
        out_shape=jax.ShapeDtypeStruct((Bn, T, MB_W), BF16),
        scratch_shapes=[pltpu.VMEM((4, n_blocks, MB_BLOCK, MB_BLOCK), F32), pltpu.VMEM((128, T), BF16)],
        compiler_params=_cparams(("arbitrary", "arbitrary")),
        name="moba_prompt",
    )(q, km, kb, vt_all)


def _moba_sample_kernel(pt_ref, q_ref, kn_ref, vn_ref, *refs, n_blocks, ts, bps):
    ppb = MB_BLOCK // PAGE_SIZE
    k_refs = refs[:bps * ppb]
    v_refs = refs[bps * ppb:2 * bps * ppb]
    y_ref, qbd_ref, oblk_ref, km_ref, m_ref, l_ref = refs[2 * bps * ppb:]
    step = pl.program_id(1)
    R = MB_HEADS * ts
    own = (lax.broadcasted_iota(jnp.int32, (R, MB_W), 0) // ts
           == lax.broadcasted_iota(jnp.int32, (R, MB_W), 1) // MB_HD)
    blane = lax.broadcasted_iota(jnp.int32, (R, n_blocks), 1)
    kml = lax.broadcasted_iota(jnp.int32, (MB_W, n_blocks), 1)

    @pl.when(step == 0)
    def _():
        for ref in (m_ref, l_ref, km_ref):
            ref[...] = jnp.zeros(ref.shape, F32)
        qbd_ref[...] = jnp.where(own, jnp.concatenate([q_ref[0]] * MB_HEADS, axis=0), 0.0)

    qbd = qbd_ref[...]
    qb = (qbd * (MB_HD ** -0.5)).astype(BF16)
    page = lambda ref: ref[0, 0].reshape(MB_W, PAGE_SIZE)
    m_all, l_all, km_all = m_ref[...], l_ref[...], km_ref[...]
    blocks = lambda refs, i: jnp.concatenate([page(refs[ppb * i + j]) for j in range(ppb)], axis=1)
    scores, probs = [], []
    for i in range(bps):
        kblk = blocks(k_refs, i)
        scores.append(jnp.dot(qb, kblk.astype(BF16), preferred_element_type=F32))
        kmean = jnp.sum(kblk, axis=1, keepdims=True) * (1.0 / MB_BLOCK)
        km_all = jnp.where(kml == step * bps + i, kmean, km_all)
    for i, s in enumerate(scores):
        n = step * bps + i
        m_n = jnp.max(s, axis=1, keepdims=True)
        p = jnp.exp(s - m_n)
        probs.append(p.astype(BF16))
        m_all = jnp.where(blane == n, m_n, m_all)
        l_all = jnp.where(blane == n, jnp.sum(p, axis=1, keepdims=True), l_all)
    for i, p in enumerate(probs):
        oblk_ref[step * bps + i] = _dot_nt(p, blocks(v_refs, i).astype(BF16))
    km_ref[...] = km_all
    m_ref[...] = m_all
    l_ref[...] = l_all

    @pl.when(step == n_blocks // bps - 1)
    def _():
        gate = jnp.dot(qbd, km_ref[...], preferred_element_type=F32, precision=lax.Precision.HIGHEST)
        sel = _top_blocks(gate, n_blocks, axis=1) > 0.0
        s_own = _dot_nt(qb, kn_ref[0].astype(BF16))
        tq = lax.broadcasted_iota(jnp.int32, (R, ts), 0) % ts
        tk = lax.broadcasted_iota(jnp.int32, (R, ts), 1)
        s_own = jnp.where(tk <= tq, s_own, NEG_INF)
        mb = jnp.where(sel, m_ref[...], NEG_INF)
        m_all = jnp.maximum(jnp.max(s_own, axis=1, keepdims=True), jnp.max(mb, axis=1, keepdims=True))
        w = jnp.exp(mb - m_all)
        p_own = jnp.exp(s_own - m_all)
        l = jnp.sum(p_own, axis=1, keepdims=True) + jnp.sum(w * l_ref[...], axis=1, keepdims=True)
        o = jnp.dot(p_own.astype(BF16), vn_ref[0].astype(BF16), preferred_element_type=F32)
        for nb in range(n_blocks):
            o = o + w[:, nb:nb + 1] * oblk_ref[nb]
        o = jnp.where(own, o / l, 0.0)
        y = o[0:ts]
        for h in range(1, MB_HEADS):
            y = y + o[h * ts:(h + 1) * ts]
        y_ref[0] = y.astype(BF16)


def _moba_sample(q, kn, vn, cache_kt, cache_vt, page_table, layer):
    DB, ts, _ = q.shape
    n_pages = page_table.shape[1]
    ppb = MB_BLOCK // PAGE_SIZE
    n_blocks = n_pages // ppb
    R = MB_HEADS * ts
    tok = pl.BlockSpec((1, ts, MB_W), lambda b, n, pt: (b, 0, 0))
    bps = SAMPLE_BLOCKS_PER_STEP
    assert n_blocks % bps == 0
    pps = bps * ppb

    def page(i):
        return pl.BlockSpec((1, 1, MB_HEADS, MB_HD, PAGE_SIZE),
                            lambda b, n, pt: (layer, pt[b * n_pages + n * pps + i], 0, 0, 0))

    pages = [page(i) for i in range(pps)]
    grid_spec = pltpu.PrefetchScalarGridSpec(
        num_scalar_prefetch=1,
        grid=(DB, n_blocks // bps),
        in_specs=[tok, tok, tok] + pages + pages,
        out_specs=tok,
        scratch_shapes=[pltpu.VMEM((R, MB_W), F32),
                        pltpu.VMEM((n_blocks, R, MB_W), F32),
                        pltpu.VMEM((MB_W, n_blocks), F32),
                        pltpu.VMEM((R, n_blocks), F32),
                        pltpu.VMEM((R, n_blocks), F32)],
    )
    return pl.pallas_call(
        functools.partial(_moba_sample_kernel, n_blocks=n_blocks, ts=ts, bps=bps),
        grid_spec=grid_spec,
        out_shape=jax.ShapeDtypeStruct((DB, ts, MB_W), BF16),
        compiler_params=_cparams(("arbitrary", "arbitrary")),
        name="moba_sample",
    )(page_table.reshape(-1), q, kn, vn, *([cache_kt] * pps), *([cache_vt] * pps))


def _gla_kernel(qg_ref, kg_ref, vg_ref, rg_ref, lg_ref, s0_ref, gn_ref, y_ref, sfin_ref, st_ref, *, n_chunks):
    t = pl.program_id(1)
    C = GLA_CHUNK

    @pl.when(t == 0)
    def _():
        for h in range(GLA_HEADS):
            st_ref[:, h * GLA_DK:(h + 1) * GLA_DK] = s0_ref[0, h].T

    ri = lax.broadcasted_iota(jnp.int32, (C, C), 0)
    ci = lax.broadcasted_iota(jnp.int32, (C, C), 1)
    tri = (ci <= ri).astype(BF16)
    m1 = (ri >= 32) & (ci < 32)
    m2 = (ri // 32 == ci // 32) & (ri % 32 >= 16) & (ci % 32 < 16)
    m3 = (ri // 16 == ci // 16) & (ci <= ri)
    grp = lax.broadcasted_iota(jnp.int32, (C, GLA_KW), 0)

    heads = [(slice(h * GLA_DK, (h + 1) * GLA_DK), slice(h * GLA_DV, (h + 1) * GLA_DV)) for h in range(GLA_HEADS)]
    sts = [st_ref[:, kl] for kl, _ in heads]
    for g0 in range(0, n_chunks, GLA_GROUP):
        cs = range(g0, min(g0 + GLA_GROUP, n_chunks))
        rows = {c: slice(c * C, (c + 1) * C) for c in cs}
        bcum = {}
        for c in cs:
            hi, mid, lo = _split3(lg_ref[0, rows[c], :])
            bcum[c] = (jnp.dot(tri, hi, preferred_element_type=F32) + jnp.dot(tri, mid, preferred_element_type=F32)
                       + jnp.dot(tri, lo, preferred_element_type=F32))
        scaled = {}
        for c in cs:
            b = bcum[c]
            q = qg_ref[0, rows[c], :] * (GLA_DK ** -0.5)
            k = kg_ref[0, rows[c], :]
            brow = lambda i: jnp.broadcast_to(b[i:i + 1, :], (C, GLA_KW))
            ref1 = brow(31)
            ref2 = jnp.where(grp < 32, brow(15), brow(47))
            ref3 = jnp.where(grp < 16, brow(7),
                             jnp.where(grp < 32, brow(23), jnp.where(grp < 48, brow(39), brow(55))))
            b_last = b[C - 1:C, :]
            scaled[c] = dict(
                qs=(q * jnp.exp(b)).astype(BF16), ks=(k * jnp.exp(b_last - b)).astype(BF16),
                q1=(q * jnp.exp(jnp.minimum(b - ref1, 0.0))).astype(BF16),
                k1=(k * jnp.exp(jnp.minimum(ref1 - b, 0.0))).astype(BF16),
                q2=(q * jnp.exp(jnp.minimum(b - ref2, 0.0))).astype(BF16),
                k2=(k * jnp.exp(jnp.minimum(ref2 - b, 0.0))).astype(BF16),
                q3=(q * jnp.exp(b - ref3)).astype(BF16), k3=(k * jnp.exp(ref3 - b)).astype(BF16),
                decay=jnp.exp(b_last))
        levels = {}
        for c in cs:
            x = scaled[c]
            for h, (kl, _) in enumerate(heads):
                levels[c, h] = (_dot_nt(x["q1"][:, kl], x["k1"][:, kl]), _dot_nt(x["q2"][:, kl], x["k2"][:, kl]),
                                _dot_nt(x["q3"][:, kl], x["k3"][:, kl]))
        intra, update = {}, {}
        for c in cs:
            for h, (kl, vl) in enumerate(heads):
                l1, l2, l3 = levels[c, h]
                a = jnp.where(m1, l1, 0.0) + jnp.where(m2, l2, 0.0) + jnp.where(m3, l3, 0.0)
                vf = vg_ref[0, rows[c], vl]
                intra[c, h] = jnp.dot(a.astype(BF16), vf.astype(BF16), preferred_element_type=F32)
                update[c, h] = jnp.dot(vf.T.astype(BF16), scaled[c]["ks"][:, kl], preferred_element_type=F32)
        for c in cs:
            for h, (kl, vl) in enumerate(heads):
                o = _dot_nt(scaled[c]["qs"][:, kl], sts[h].astype(BF16)) + intra[c, h]
                sts[h] = sts[h] * scaled[c]["decay"][:, kl] + update[c, h]
                y_ref[0, rows[c], vl] = (_rms(o, gn_ref[:, vl]) * rg_ref[0, rows[c], vl]).astype(BF16)
    for (kl, _), st in zip(heads, sts):
        st_ref[:, kl] = st

    @pl.when(t == pl.num_programs(1) - 1)
    def _():
        for h in range(GLA_HEADS):
            sfin_ref[0, h] = st_ref[:, h * GLA_DK:(h + 1) * GLA_DK].T


def _gla(qg, kg, vg, rg, lg, s0, gn):
    Bn, T, _ = qg.shape
    TT = min(T, 512)
    kspec = pl.BlockSpec((1, TT, GLA_KW), lambda b, t: (b, t, 0))
    vspec = pl.BlockSpec((1, TT, GLA_VW), lambda b, t: (b, t, 0))
    sspec = pl.BlockSpec((1, GLA_HEADS, GLA_DK, GLA_DV), lambda b, t: (b, 0, 0, 0))
    return pl.pallas_call(
        functools.partial(_gla_kernel, n_chunks=TT // GLA_CHUNK),
        grid=(Bn, T // TT),
        in_specs=[kspec, kspec, vspec, vspec, kspec, sspec, _const_spec((1, GLA_VW))],
        out_specs=[vspec, sspec],
        out_shape=[jax.ShapeDtypeStruct((Bn, T, GLA_VW), BF16),
                   jax.ShapeDtypeStruct((Bn, GLA_HEADS, GLA_DK, GLA_DV), F32)],
        scratch_shapes=[pltpu.VMEM((GLA_DV, GLA_KW), F32)],
        compiler_params=_cparams(("arbitrary", "arbitrary")),
        name="gla",
    )(qg, kg, vg, rg, lg, s0, gn)


def _merge_ffn_kernel(x_ref, yp_ref, ym_ref, yg_ref, sg_ref, wp_ref, wmb_ref, wgl_ref, wo_ref,
                      gpost_ref, gfpre_ref, gfpost_ref, w1_ref, w2_ref, o_ref):
    tm = x_ref.shape[0]
    subs = [slice(r, r + FFN_SUB_ROWS) for r in range(0, tm, FFN_SUB_ROWS)] if tm > FFN_SUB_ROWS else [slice(0, tm)]

    def dot(a, w):
        return jnp.dot(a.astype(BF16), w, preferred_element_type=F32)

    branches = [[dot(y_ref[r, :], w_ref[...]) for y_ref, w_ref in
                 ((yp_ref, wp_ref), (ym_ref, wmb_ref), (yg_ref, wgl_ref))] for r in subs]
    merged = [sum(sg_ref[r, 1024 * j:1024 * (j + 1)] * b for j, b in enumerate(bs))
              for r, bs in zip(subs, branches)]
    z = [dot(m, wo_ref[...]) for m in merged]
    x1 = [x_ref[r, :] + _rms(zi, gpost_ref[...]) for r, zi in zip(subs, z)]
    hb = [_rms(xi, gfpre_ref[...]).astype(BF16) for xi in x1]
    FC = 1024
    f = [jnp.zeros(xi.shape, F32) for xi in x1]

    def second(i, c, a):
        a = jnp.maximum(a, 0.0)
        f[i] = f[i] + dot(a * a, w2_ref[c * FC:(c + 1) * FC, :])

    pending = None
    for c in range(D_FF // FC):
        for i in range(len(subs)):
            a = jnp.dot(hb[i], w1_ref[:, c * FC:(c + 1) * FC], preferred_element_type=F32)
            if pending is not None:
                second(*pending)
            pending = (i, c, a)
    second(*pending)
    for i, r in enumerate(subs):
        o_ref[r, :] = x1[i] + _rms(f[i], gfpost_ref[...])


def _merge_ffn(x, yp, ym, yg, sg, wp, wmb, wgl, wo, gpost, gfpre, gfpost, w1, w2):
    M = x.shape[0]
    tm = min(FFN_ROW_TILE, M)
    row = lambda w: pl.BlockSpec((tm, w), lambda i: (i, 0))
    vec = _const_spec((1, D_MODEL))
    return pl.pallas_call(
        _merge_ffn_kernel,
        grid=(M // tm,),
        in_specs=[row(D_MODEL), row(POOL_W), row(MB_W), row(GLA_VW), row(3 * D_MODEL),
                  _const_spec(wp.shape), _const_spec(wmb.shape), _const_spec(wgl.shape), _const_spec(wo.shape),
                  vec, vec, vec, _const_spec(w1.shape), _const_spec(w2.shape)],
        out_specs=row(D_MODEL),
        out_shape=jax.ShapeDtypeStruct((M, D_MODEL), F32),
        compiler_params=_cparams(("arbitrary",)),
        name="merge_ffn",
    )(x, yp, ym, yg, sg, wp, wmb, wgl, wo, gpost, gfpre, gfpost, w1, w2)


def _layer(x, Bn, T, tabs, hist16, pos0, s0, sample_attend, lw, prompt_state=None):
    (wm, wa, wg, wa2, ba, gn, wgrp, pscale, wp, wmb, wgl, wo, gpre, gpost, gfpre, gfpost, w1, w2) = lw
    seq = lambda a: a.reshape(Bn, T, a.shape[-1])
    if sample_attend is None:
        depth, layer = prompt_state[:2]
        u, q, kb, km, k, v, qg, kg, vg, rg, lg, sg = _in_proj(x, gpre, wm, wa, wg, wa2, ba, tabs,
                                                             prompt_state=(T,) + tuple(prompt_state))
        ym = _moba_prompt(seq(q), km.reshape(Bn, T // MB_BLOCK, MB_W), seq(kb), v, layer)
    else:
        u, q, k, v, qg, kg, vg, rg, lg, sg = _in_proj(x, gpre, wm, wa, wg, wa2, ba, tabs)
        ym = sample_attend(seq(q), seq(k), seq(v))
        k, v = (a.reshape(Bn, T, MB_HEADS, MB_HD) for a in (k, v))
    u3 = seq(u)
    yp = _pool(u3, hist16, wgrp, pscale, pos0)
    Tp = -(-T // GLA_CHUNK) * GLA_CHUNK
    padt = lambda a: jnp.pad(seq(a), ((0, 0), (0, Tp - T), (0, 0))) if Tp != T else seq(a)
    yg, s_new = _gla(padt(qg), padt(kg), padt(vg), padt(rg), padt(lg), s0, gn)
    yg = yg[:, :T].reshape(Bn * T, GLA_VW)
    x = _merge_ffn(x, yp.reshape(Bn * T, POOL_W), ym.reshape(Bn * T, MB_W), yg, sg,
                   wp, wmb, wgl, wo, gpost, gfpre, gfpost, w1, w2)
    new_hist = jnp.concatenate([hist16[:, 1:], u3], axis=1)[:, -POOL_HIST:]
    return x, k, v, new_hist, s_new


def kernel(x_prompt, x_sample, cache_k, cache_v, state_pool, state_gla, page_table, w_in, w_gla_a2, b_gla_a,
           gla_norm, w_pool_grp, pool_scale, w_pool_out, w_mb_out, w_gla_out, w_o, norm_mix_pre,
           norm_mix_post, norm_ffn_pre, norm_ffn_post, w_ff1, w_ff2):
    B, T, D = x_prompt.shape
    DB, TS, _ = x_sample.shape
    depth = w_in.shape[0]
    n_pages = page_table.shape[1]
    past_len = n_pages * PAGE_SIZE
    assert D == D_MODEL and T % MB_BLOCK == 0 and past_len % MB_BLOCK == 0
    assert past_len // MB_BLOCK >= MB_TOPK and (B * T) % ROW_TILE == 0 and MB_BLOCK == 2 * PAGE_SIZE

    tabs_p = _rope_tables(jnp.arange(T))
    reps = max(1, min(ROW_TILE, DB * TS) // TS)
    tabs_s = tuple(jnp.tile(t, (reps, 1)) for t in _rope_tables(past_len + jnp.arange(TS)))
    ckt = jnp.transpose(cache_k, (0, 1, 3, 4, 2))
    cvt = jnp.transpose(cache_v, (0, 1, 3, 4, 2))
    zrow = lambda n: jnp.zeros((n, 1, POOL_W), F32)
    hist_p = jnp.zeros((B, 16, POOL_W), F32)
    s0_p = jnp.zeros((B, GLA_HEADS, GLA_DK, GLA_DV), F32)

    hp = x_prompt.reshape(B * T, D)
    hs = x_sample.reshape(DB * TS, D)
    outs_p, outs_s = [], []
    kt_all = vt_all = None
    for l in range(depth):
        bf = lambda a: a.astype(BF16)
        vec = lambda a: a.reshape(1, -1)
        lw = (bf(w_in[l][:, :W_MAIN]), bf(w_in[l][:, W_MAIN:W_MAIN + GLA_RANK]), bf(w_in[l][:, W_MAIN + GLA_RANK:]),
              bf(w_gla_a2[l]), vec(b_gla_a[l]), vec(gla_norm[l]), bf(w_pool_grp[l]), vec(pool_scale[l]),
              bf(w_pool_out[l]), bf(w_mb_out[l]), bf(w_gla_out[l]), bf(w_o[l]),
              vec(norm_mix_pre[l]), vec(norm_mix_post[l]), vec(norm_ffn_pre[l]), vec(norm_ffn_post[l]),
              bf(w_ff1[l]), bf(w_ff2[l]))
        wvt = bf(w_in[l][:, POOL_W + 2 * MB_W:POOL_W + 3 * MB_W].T)
        hp, kt_all, vt_all, *op = _layer(hp, B, T, tabs_p, hist_p, 0, s0_p, None, lw,
                                         prompt_state=(depth, l, wvt, kt_all, vt_all))
        hist_s = jnp.concatenate([zrow(DB), state_pool[l]], axis=1)
        attend_s = functools.partial(_moba_sample, cache_kt=ckt, cache_vt=cvt, page_table=page_table, layer=l)
        hs, *os_ = _layer(hs, DB, TS, tabs_s, hist_s, past_len, state_gla[l], attend_s, lw)
        outs_p.append(op)
        outs_s.append(os_)
    stack = lambda outs, i: jnp.stack([o[i] for o in outs])
    untr = lambda a: jnp.transpose(a.reshape(depth, B, MB_HEADS, MB_HD, T), (0, 1, 4, 2, 3))
    return (hp.reshape(B, T, D), hs.reshape(DB, TS, D),
            untr(kt_all), untr(vt_all), stack(outs_p, 0), stack(outs_p, 1),
            stack(outs_s, 0), stack(outs_s, 1), stack(outs_s, 2), stack(outs_s, 3))
```

```python
import functools
import math

import jax
import jax.numpy as jnp
from jax import lax
from jax.experimental import pallas as pl
from jax.experimental.pallas import tpu as pltpu

F32 = jnp.float32
BF16 = jnp.bfloat16

D_MODEL = 1024
PAGE_SIZE = 128
POOL_GC = 128
POOL_W = 512
POOL_WINDOWS = (2, 4, 8, 16)
POOL_HIST = 15
MB_HEADS = 8
MB_HD = 64
MB_W = 512
MB_BLOCK = 256
MB_TOPK = 3
ROPE_THETA = 500000.0
ROPE_DIM = 16
GLA_HEADS = 4
GLA_DK = 64
GLA_DV = 128
GLA_KW = 256
GLA_VW = 512
GLA_RANK = 16
GLA_NORMALIZER = 16.0
GLA_CHUNK = 64
D_FF = 4096
EPS = 1e-6
W_MAIN = POOL_W + 3 * MB_W + 2 * GLA_KW + 2 * GLA_VW
NEG_INF = float("-inf")
LOG2E = 1.4426950408889634

VMEM_LIMIT_BYTES = 56 * 1024 * 1024
ROW_TILE = 512
FFN_ROW_TILE = 512
FFN_SUB_ROWS = 256
SAMPLE_BLOCKS_PER_STEP = 16
GLA_GROUP = 8


def _cparams(sem):
    return pltpu.CompilerParams(dimension_semantics=sem, vmem_limit_bytes=VMEM_LIMIT_BYTES)


def _const_spec(shape):
    nd = len(shape)
    return pl.BlockSpec(shape, lambda *_: (0,) * nd, pipeline_mode=pl.Buffered(1))


def _rms(x, g):
    ms = jnp.mean(x * x, axis=-1, keepdims=True)
    return x * lax.rsqrt(ms + EPS) * g


def _sigmoid(x):
    return 0.5 * jnp.tanh(0.5 * x) + 0.5


def _split3(x):
    hi = x.astype(BF16)
    r = x - hi.astype(F32)
    mid = r.astype(BF16)
    lo = (r - mid.astype(F32)).astype(BF16)
    return hi, mid, lo


def _in_proj_kernel(*refs, prompt, first_layer):
    x_ref, gpre_ref, wm_ref, wa_ref, wg_ref, wa2_ref, ba_ref, cos_ref, sa_ref, sb_ref = refs[:10]
    n_in = 10 + (0 if not prompt else 1 if first_layer else 3)
    wvt_ref = refs[10] if prompt else None
    outs = refs[n_in:]
    u_ref, q_ref = outs[:2]
    qg_ref, kg_ref, vg_ref, rg_ref, lg_ref, sg_ref = outs[-6:]
    h = _rms(x_ref[...], gpre_ref[...]).astype(BF16)

    def proj(w_ref, a, b):
        return jnp.dot(h, w_ref[:, a:b], preferred_element_type=F32)

    cos, sa, sb = cos_ref[...], sa_ref[...], sb_ref[...]

    def rope(col0):
        half = ROPE_DIM // 2
        for j in range(MB_W // 128):
            xj = proj(wm_ref, col0 + 128 * j, col0 + 128 * (j + 1))
            yield j, xj * cos + pltpu.roll(xj, 128 - half, 1) * sa + pltpu.roll(xj, half, 1) * sb

    for j in range(3):
        sg_ref[:, 1024 * j:1024 * (j + 1)] = _sigmoid(proj(wg_ref, 1024 * j, 1024 * (j + 1))).astype(BF16)
    r = proj(wm_ref, 3072, 3584)
    rg_ref[...] = r * _sigmoid(r)
    a = jnp.dot(h, wa_ref[...], preferred_element_type=F32)
    xg = jnp.dot(a.astype(BF16), wa2_ref[...], preferred_element_type=F32) + ba_ref[...]
    lg_ref[...] = (jnp.minimum(xg, 0.0) - jnp.log1p(jnp.exp(-jnp.abs(xg)))) * (1.0 / GLA_NORMALIZER)
    u_ref[...] = proj(wm_ref, 0, 512)
    for j, rj in rope(512):
        q_ref[:, 128 * j:128 * (j + 1)] = rj
    if prompt:
        kb_ref, km_ref, kt_ref, vt_ref = outs[2:6]
        n_blk = km_ref.shape[0]
        col_sums = [[] for _ in range(n_blk)]
        for j, rj in rope(1024):
            kb_ref[:, 128 * j:128 * (j + 1)] = rj.astype(BF16)
            kt_ref[0, 0, 128 * j:128 * (j + 1), :] = rj.T
            for i in range(n_blk):
                col_sums[i].append(jnp.sum(rj[i * MB_BLOCK:(i + 1) * MB_BLOCK], axis=0, keepdims=True))
        for i in range(n_blk):
            km_ref[i] = jnp.concatenate(col_sums[i], axis=1) * (1.0 / MB_BLOCK)
        vt_ref[0, 0] = _dot_nt(wvt_ref[...], h)
        for d in range(1, kt_ref.shape[0]):
            kt_ref[d] = jnp.zeros(kt_ref.shape[1:], F32)
            vt_ref[d] = jnp.zeros(vt_ref.shape[1:], F32)
    else:
        k_ref, v_ref = outs[2:4]
        for j, rj in rope(1024):
            k_ref[:, 128 * j:128 * (j + 1)] = rj
        v_ref[...] = proj(wm_ref, 1536, 2048)
    qg_ref[...] = proj(wm_ref, 2048, 2304)
    kg_ref[...] = proj(wm_ref, 2304, 2560)
    vg_ref[...] = proj(wm_ref, 2560, 3072)


def _in_proj(x, gpre, wm, wa, wg, wa2, ba, tabs, prompt_state=None):
    M = x.shape[0]
    tm = min(ROW_TILE, M)
    period = tabs[0].shape[0] // tm
    row = lambda w: pl.BlockSpec((tm, w), lambda i: (i, 0))
    tab = pl.BlockSpec((tm, 128), lambda i: (i % period, 0))
    f32 = lambda w: jax.ShapeDtypeStruct((M, w), F32)
    tail_w = (GLA_KW, GLA_KW, GLA_VW, GLA_VW, GLA_KW, 3 * D_MODEL)
    in_specs = [row(D_MODEL), _const_spec((1, D_MODEL)), _const_spec(wm.shape), _const_spec(wa.shape),
                _const_spec(wg.shape), _const_spec(wa2.shape), _const_spec((1, GLA_KW)), tab, tab, tab]
    args = [x, gpre, wm, wa, wg, wa2, ba, *tabs]
    aliases = {}
    if prompt_state is None:
        out_specs = [row(POOL_W), row(MB_W), row(MB_W), row(MB_W)]
        out_shape = [f32(POOL_W), f32(MB_W), f32(MB_W), f32(MB_W)]
        first = False
    else:
        seq_len, depth, layer, wvt, kt_all, vt_all = prompt_state
        assert tm % MB_BLOCK == 0 and seq_len % tm == 0
        tps = seq_len // tm
        first = layer == 0
        stacked = jax.ShapeDtypeStruct((depth, M // seq_len, MB_W, seq_len), F32)
        in_specs.append(_const_spec(wvt.shape))
        args.append(wvt)
        if first:
            st_spec = pl.BlockSpec((depth, 1, MB_W, tm), lambda i: (0, i // tps, 0, i % tps))
        else:
            st_spec = pl.BlockSpec((1, 1, MB_W, tm), lambda i: (layer, i // tps, 0, i % tps))
            in_specs += [pl.BlockSpec(memory_space=pl.ANY)] * 2
            args += [kt_all, vt_all]
            aliases = {11: 4, 12: 5}
        out_specs = [row(POOL_W), row(MB_W), row(MB_W),
                     pl.BlockSpec((tm // MB_BLOCK, 1, MB_W), lambda i: (i, 0, 0)),
                     st_spec, st_spec]
        out_shape = [f32(POOL_W), f32(MB_W), jax.ShapeDtypeStruct((M, MB_W), BF16),
                     jax.ShapeDtypeStruct((M // MB_BLOCK, 1, MB_W), F32), stacked, stacked]
    return pl.pallas_call(
        functools.partial(_in_proj_kernel, prompt=prompt_state is not None, first_layer=first),
        grid=(M // tm,),
        in_specs=in_specs,
        out_specs=out_specs + [row(w) for w in tail_w],
        out_shape=out_shape + [f32(w) for w in tail_w[:-1]] + [jax.ShapeDtypeStruct((M, tail_w[-1]), BF16)],
        input_output_aliases=aliases,
        compiler_params=_cparams(("arbitrary",)),
        name="in_proj",
    )(*args)


def _rope_tables(pos):
    half = ROPE_DIM // 2
    inv = ROPE_THETA ** (-jnp.arange(half, dtype=F32) / half)
    ang = pos.astype(F32)[:, None] * inv[None, :]
    cos, sin = jnp.cos(ang), jnp.sin(ang)
    n = pos.shape[0]
    zeros = jnp.zeros((n, MB_HD - ROPE_DIM), F32)
    zh = jnp.zeros((n, half), F32)
    c = jnp.concatenate([cos, cos, zeros + 1.0], axis=1)
    sa = jnp.concatenate([-sin, zh, zeros], axis=1)
    sb = jnp.concatenate([zh, sin, zeros], axis=1)
    two = lambda t: jnp.concatenate([t, t], axis=1)
    return two(c), two(sa), two(sb)


def _pool_kernel(u_ref, hist_ref, wgrp_ref, scale_ref, y_ref, ext_ref, *, pos0, T, R, nb):
    for s in range(nb):
        ext_ref[s, 0:16, :] = hist_ref[s]
        ext_ref[s, 16:16 + T, :] = u_ref[s]
    for c in range(T // R):
        r0 = 16 + c * R
        posp1 = lax.broadcasted_iota(jnp.int32, (R, 1), 0) + (pos0 + c * R + 1)
        for g, w in enumerate(POOL_WINDOWS):
            lanes = slice(g * POOL_GC, (g + 1) * POOL_GC)
            cnt = jnp.minimum(posp1, w).astype(F32)
            ds = []
            for s in range(nb):
                cur = ext_ref[s, r0:r0 + R, lanes]
                acc = cur
                for j in range(1, w):
                    acc = acc + ext_ref[s, r0 - j:r0 - j + R, lanes]
                ds.append(acc / cnt - cur)
            d = ds[0] if nb == 1 else jnp.concatenate(ds, axis=0)
            yg = jnp.dot(d.astype(BF16), wgrp_ref[g], preferred_element_type=F32) * scale_ref[:, lanes]
            for s in range(nb):
                y_ref[s, c * R:(c + 1) * R, lanes] = yg[s * R:(s + 1) * R].astype(BF16)


def _pool(u, hist16, wgrp, scale, pos0):
    Bn, T, _ = u.shape
    R = min(T, 256)
    nb = Bn if T < MB_BLOCK else 1
    seq = pl.BlockSpec((nb, T, POOL_W), lambda b: (b, 0, 0))
    return pl.pallas_call(
        functools.partial(_pool_kernel, pos0=pos0, T=T, R=R, nb=nb),
        grid=(Bn // nb,),
        in_specs=[seq, pl.BlockSpec((nb, 16, POOL_W), lambda b: (b, 0, 0)),
                  _const_spec(wgrp.shape), _const_spec((1, POOL_W))],
        out_specs=seq,
        out_shape=jax.ShapeDtypeStruct((Bn, T, POOL_W), BF16),
        scratch_shapes=[pltpu.VMEM((nb, 16 + T, POOL_W), F32)],
        compiler_params=_cparams(("arbitrary",)),
        name="pool",
    )(u, hist16, wgrp, scale)


def _top_blocks(gate, n_valid, axis):
    n_blocks = gate.shape[axis]
    iota = lax.broadcasted_iota(jnp.int32, gate.shape, axis)
    g = jnp.where(iota < n_valid, gate, NEG_INF)
    sel = jnp.zeros(gate.shape, F32)
    for _ in range(MB_TOPK):
        m = jnp.max(g, axis=axis, keepdims=True)
        idx = jnp.min(jnp.where(g == m, iota, n_blocks), axis=axis, keepdims=True)
        pick = (iota == idx) & (m > NEG_INF)
        sel = jnp.where(pick, 1.0, sel)
        g = jnp.where(pick, NEG_INF, g)
    return sel


def _col_max(s):
    r, c = s.shape
    return jnp.max(jnp.max(s.reshape(r // 8, 8, c), axis=0), axis=0, keepdims=True)


def _dot_nt(a, b, precision=None):
    return lax.dot_general(a, b, (((1,), (1,)), ((), ())), preferred_element_type=F32, precision=precision)


def _moba_prompt_kernel(q_ref, km_ref, k_ref, vt_ref, y_ref, s_ref, vtb_ref):
    B = MB_BLOCK
    n_blocks = km_ref.shape[1]
    qlane = lax.broadcasted_iota(jnp.int32, (B, 128), 1) // MB_HD
    kmlane = lax.broadcasted_iota(jnp.int32, (n_blocks, 128), 1) // MB_HD
    causal = lax.broadcasted_iota(jnp.int32, (B, B), 0) <= lax.broadcasted_iota(jnp.int32, (B, B), 1)
    drow = lax.broadcasted_iota(jnp.int32, (128, B), 0) // MB_HD
    vtb_ref[...] = vt_ref[0, 0].astype(BF16)

    def scores_stage(c):
        q = q_ref[0, c * B:(c + 1) * B, :]
        qh = [jnp.where(qlane == hh, q, 0.0) for hh in range(2)]
        qs = [(x * (MB_HD ** -0.5 * LOG2E)).astype(BF16) for x in qh]
        raw = {(hh, j): _dot_nt(k_ref[0, j * B:(j + 1) * B, :], qs[hh])
               for hh in range(2) for j in range(c + 1)}
        maxima = []
        for hh in range(2):
            if c > 0:
                gate_t = _dot_nt(jnp.where(kmlane == hh, km_ref[0], 0.0), qh[hh], precision=lax.Precision.HIGHEST)
                sel = jnp.where(_top_blocks(gate_t, c, axis=0) > 0.0, 0.0, NEG_INF)
            m = None
            for j in range(c + 1):
                s = jnp.where(causal, raw[hh, j], NEG_INF) if j == c else raw[hh, j] + sel[j:j + 1, :]
                s_ref[2 * (c % 2) + hh, j] = s
                mj = _col_max(s)
                m = mj if m is None else jnp.maximum(m, mj)
            maxima.append(m)
        return maxima

    def values_stage(c, maxima):
        outs = []
        for hh in range(2):
            l = jnp.zeros((1, B), F32)
            probs = []
            for j in range(c + 1):
                p = jnp.exp2(s_ref[2 * (c % 2) + hh, j] - maxima[hh])
                l = l + jnp.sum(p, axis=0, keepdims=True)
                probs.append(p.astype(BF16))
            acc = jnp.dot(vtb_ref[:, :(c + 1) * B], jnp.concatenate(probs, axis=0), preferred_element_type=F32)
            outs.append(acc / l)
        y_ref[0, c * B:(c + 1) * B, :] = jnp.where(drow == 0, outs[0], outs[1]).T.astype(BF16)

    pending = None
    for c in range(n_blocks):
        maxima = scores_stage(c)
        if pending is not None:
            values_stage(*pending)
        pending = (c, maxima)
    values_stage(*pending)


def _moba_prompt(q, km, kb, vt_all, layer):
    Bn, T, _ = q.shape
    n_blocks = T // MB_BLOCK
    seq = pl.BlockSpec((1, T, 128), lambda b, hp: (b, 0, hp))
    return pl.pallas_call(
        _moba_prompt_kernel,
        grid=(Bn, MB_W // 128),
        in_specs=[seq,
                  pl.BlockSpec((1, n_blocks, 128), lambda b, hp: (b, 0, hp)),
                  seq,
                  pl.BlockSpec((1, 1, 128, T), lambda b, hp: (layer, b, hp, 0))],
        out_specs=seq,
        out_shape=jax.ShapeDtypeStruct((Bn, T, MB_W), BF16),
        scratch_shapes=[pltpu.VMEM((4, n_blocks, MB_BLOCK, MB_BLOCK), F32), pltpu.VMEM((128, T), BF16)],
        compiler_params=_cparams(("arbitrary", "arbitrary")),
        name="moba_prompt",
    )(q, km, kb, vt_all)


def _moba_sample_kernel(pt_ref, q_ref, kn_ref, vn_ref, *refs, n_blocks, ts, bps):
    ppb = MB_BLOCK // PAGE_SIZE
    k_refs = refs[:bps * ppb]
    v_refs = refs[bps * ppb:2 * bps * ppb]
    y_ref, qbd_ref, oblk_ref, km_ref, m_ref, l_ref = refs[2 * bps * ppb:]
    step = pl.program_id(1)
    R = MB_HEADS * ts
    own = (lax.broadcasted_iota(jnp.int32, (R, MB_W), 0) // ts
           == lax.broadcasted_iota(jnp.int32, (R, MB_W), 1) // MB_HD)
    blane = lax.broadcasted_iota(jnp.int32, (R, n_blocks), 1)
    kml = lax.broadcasted_iota(jnp.int32, (MB_W, n_blocks), 1)

    @pl.when(step == 0)
    def _():
        for ref in (m_ref, l_ref, km_ref):
            ref[...] = jnp.zeros(ref.shape, F32)
        qbd_ref[...] = jnp.where(own, jnp.concatenate([q_ref[0]] * MB_HEADS, axis=0), 0.0)

    qbd = qbd_ref[...]
    qb = (qbd * (MB_HD ** -0.5)).astype(BF16)
    page = lambda ref: ref[0, 0].reshape(MB_W, PAGE_SIZE)
    m_all, l_all, km_all = m_ref[...], l_ref[...], km_ref[...]
    blocks = lambda refs, i: jnp.concatenate([page(refs[ppb * i + j]) for j in range(ppb)], axis=1)
    scores, probs = [], []
    for i in range(bps):
        kblk = blocks(k_refs, i)
        scores.append(jnp.dot(qb, kblk.astype(BF16), preferred_element_type=F32))
        kmean = jnp.sum(kblk, axis=1, keepdims=True) * (1.0 / MB_BLOCK)
        km_all = jnp.where(kml == step * bps + i, kmean, km_all)
    for i, s in enumerate(scores):
        n = step * bps + i
        m_n = jnp.max(s, axis=1, keepdims=True)
        p = jnp.exp(s - m_n)
        probs.append(p.astype(BF16))
        m_all = jnp.where(blane == n, m_n, m_all)
        l_all = jnp.where(blane == n, jnp.sum(p, axis=1, keepdims=True), l_all)
    for i, p in enumerate(probs):
        oblk_ref[step * bps + i] = _dot_nt(p, blocks(v_refs, i).astype(BF16))
    km_ref[...] = km_all
    m_ref[...] = m_all
    l_ref[...] = l_all

    @pl.when(step == n_blocks // bps - 1)
    def _():
        gate = jnp.dot(qbd, km_ref[...], preferred_element_type=F32, precision=lax.Precision.HIGHEST)
        sel = _top_blocks(gate, n_blocks, axis=1) > 0.0
        s_own = _dot_nt(qb, kn_ref[0].astype(BF16))
        tq = lax.broadcasted_iota(jnp.int32, (R, ts), 0) % ts
        tk = lax.broadcasted_iota(jnp.int32, (R, ts), 1)
        s_own = jnp.where(tk <= tq, s_own, NEG_INF)
        mb = jnp.where(sel, m_ref[...], NEG_INF)
        m_all = jnp.maximum(jnp.max(s_own, axis=1, keepdims=True), jnp.max(mb, axis=1, keepdims=True))
        w = jnp.exp(mb - m_all)
        p_own = jnp.exp(s_own - m_all)
        l = jnp.sum(p_own, axis=1, keepdims=True) + jnp.sum(w * l_ref[...], axis=1, keepdims=True)
        o = jnp.dot(p_own.astype(BF16), vn_ref[0].astype(BF16), preferred_element_type=F32)
        for nb in range(n_blocks):
            o = o + w[:, nb:nb + 1] * oblk_ref[nb]
        o = jnp.where(own, o / l, 0.0)
        y = o[0:ts]
        for h in range(1, MB_HEADS):
            y = y + o[h * ts:(h + 1) * ts]
        y_ref[0] = y.astype(BF16)


def _moba_sample(q, kn, vn, cache_kt, cache_vt, page_table, layer):
    DB, ts, _ = q.shape
    n_pages = page_table.shape[1]
    ppb = MB_BLOCK // PAGE_SIZE
    n_blocks = n_pages // ppb
    R = MB_HEADS * ts
    tok = pl.BlockSpec((1, ts, MB_W), lambda b, n, pt: (b, 0, 0))
    bps = math.gcd(n_blocks, SAMPLE_BLOCKS_PER_STEP)
    pps = bps * ppb

    def page(i):
        return pl.BlockSpec((1, 1, MB_HEADS, MB_HD, PAGE_SIZE),
                            lambda b, n, pt: (layer, pt[b * n_pages + n * pps + i], 0, 0, 0))

    pages = [page(i) for i in range(pps)]
    grid_spec = pltpu.PrefetchScalarGridSpec(
        num_scalar_prefetch=1,
        grid=(DB, n_blocks // bps),
        in_specs=[tok, tok, tok] + pages + pages,
        out_specs=tok,
        scratch_shapes=[pltpu.VMEM((R, MB_W), F32),
                        pltpu.VMEM((n_blocks, R, MB_W), F32),
                        pltpu.VMEM((MB_W, n_blocks), F32),
                        pltpu.VMEM((R, n_blocks), F32),
                        pltpu.VMEM((R, n_blocks), F32)],
    )
    return pl.pallas_call(
        functools.partial(_moba_sample_kernel, n_blocks=n_blocks, ts=ts, bps=bps),
        grid_spec=grid_spec,
        out_shape=jax.ShapeDtypeStruct((DB, ts, MB_W), BF16),
        compiler_params=_cparams(("arbitrary", "arbitrary")),
        name="moba_sample",
    )(page_table.reshape(-1), q, kn, vn, *([cache_kt] * pps), *([cache_vt] * pps))


def _gla_kernel(qg_ref, kg_ref, vg_ref, rg_ref, lg_ref, s0_ref, gn_ref, y_ref, sfin_ref, st_ref, *,
                n_chunks, valid):
    t = pl.program_id(1)
    C = GLA_CHUNK

    @pl.when(t == 0)
    def _():
        for h in range(GLA_HEADS):
            st_ref[:, h * GLA_DK:(h + 1) * GLA_DK] = s0_ref[0, h].T

    ri = lax.broadcasted_iota(jnp.int32, (C, C), 0)
    ci = lax.broadcasted_iota(jnp.int32, (C, C), 1)
    tri = (ci <= ri).astype(BF16)
    m1 = (ri >= 32) & (ci < 32)
    m2 = (ri // 32 == ci // 32) & (ri % 32 >= 16) & (ci % 32 < 16)
    m3 = (ri // 16 == ci // 16) & (ci <= ri)
    grp = lax.broadcasted_iota(jnp.int32, (C, GLA_KW), 0)

    heads = [(slice(h * GLA_DK, (h + 1) * GLA_DK), slice(h * GLA_DV, (h + 1) * GLA_DV)) for h in range(GLA_HEADS)]
    sts = [st_ref[:, kl] for kl, _ in heads]
    for g0 in range(0, n_chunks, GLA_GROUP):
        cs = range(g0, min(g0 + GLA_GROUP, n_chunks))
        rows = {c: slice(c * C, c * C + valid) for c in cs}

        def chunk(ref, c, lanes=slice(None)):
            x = ref[0, rows[c], lanes]
            return x if valid == C else jnp.concatenate([x, jnp.zeros((C - valid, x.shape[1]), x.dtype)], axis=0)

        bcum = {}
        for c in cs:
            hi, mid, lo = _split3(chunk(lg_ref, c))
            bcum[c] = (jnp.dot(tri, hi, preferred_element_type=F32) + jnp.dot(tri, mid, preferred_element_type=F32)
                       + jnp.dot(tri, lo, preferred_element_type=F32))
        scaled = {}
        for c in cs:
            b = bcum[c]
            q = chunk(qg_ref, c) * (GLA_DK ** -0.5)
            k = chunk(kg_ref, c)
            brow = lambda i: jnp.broadcast_to(b[i:i + 1, :], (C, GLA_KW))
            ref1 = brow(31)
            ref2 = jnp.where(grp < 32, brow(15), brow(47))
            ref3 = jnp.where(grp < 16, brow(7),
                             jnp.where(grp < 32, brow(23), jnp.where(grp < 48, brow(39), brow(55))))
            b_last = b[C - 1:C, :]
            scaled[c] = dict(
                qs=(q * jnp.exp(b)).astype(BF16), ks=(k * jnp.exp(b_last - b)).astype(BF16),
                q1=(q * jnp.exp(jnp.minimum(b - ref1, 0.0))).astype(BF16),
                k1=(k * jnp.exp(jnp.minimum(ref1 - b, 0.0))).astype(BF16),
                q2=(q * jnp.exp(jnp.minimum(b - ref2, 0.0))).astype(BF16),
                k2=(k * jnp.exp(jnp.minimum(ref2 - b, 0.0))).astype(BF16),
                q3=(q * jnp.exp(b - ref3)).astype(BF16), k3=(k * jnp.exp(ref3 - b)).astype(BF16),
                decay=jnp.exp(b_last))
        levels = {}
        for c in cs:
            x = scaled[c]
            for h, (kl, _) in enumerate(heads):
                levels[c, h] = (_dot_nt(x["q1"][:, kl], x["k1"][:, kl]), _dot_nt(x["q2"][:, kl], x["k2"][:, kl]),
                                _dot_nt(x["q3"][:, kl], x["k3"][:, kl]))
        intra, update = {}, {}
        for c in cs:
            for h, (kl, vl) in enumerate(heads):
                l1, l2, l3 = levels[c, h]
                a = jnp.where(m1, l1, 0.0) + jnp.where(m2, l2, 0.0) + jnp.where(m3, l3, 0.0)
                vf = chunk(vg_ref, c, vl)
                intra[c, h] = jnp.dot(a.astype(BF16), vf.astype(BF16), preferred_element_type=F32)
                update[c, h] = jnp.dot(vf.T.astype(BF16), scaled[c]["ks"][:, kl], preferred_element_type=F32)
        for c in cs:
            for h, (kl, vl) in enumerate(heads):
                o = _dot_nt(scaled[c]["qs"][:, kl], sts[h].astype(BF16)) + intra[c, h]
                sts[h] = sts[h] * scaled[c]["decay"][:, kl] + update[c, h]
                y_ref[0, rows[c], vl] = (_rms(o[:valid], gn_ref[:, vl]) * rg_ref[0, rows[c], vl]).astype(BF16)
    for (kl, _), st in zip(heads, sts):
        st_ref[:, kl] = st

    @pl.when(t == pl.num_programs(1) - 1)
    def _():
        for h in range(GLA_HEADS):
            sfin_ref[0, h] = st_ref[:, h * GLA_DK:(h + 1) * GLA_DK].T


def _gla(qg, kg, vg, rg, lg, s0, gn):
    Bn, T, _ = qg.shape
    assert T < GLA_CHUNK or T % GLA_CHUNK == 0
    TT = min(T, 512)
    kspec = pl.BlockSpec((1, TT, GLA_KW), lambda b, t: (b, t, 0))
    vspec = pl.BlockSpec((1, TT, GLA_VW), lambda b, t: (b, t, 0))
    sspec = pl.BlockSpec((1, GLA_HEADS, GLA_DK, GLA_DV), lambda b, t: (b, 0, 0, 0))
    return pl.pallas_call(
        functools.partial(_gla_kernel, n_chunks=max(1, TT // GLA_CHUNK), valid=min(TT, GLA_CHUNK)),
        grid=(Bn, T // TT),
        in_specs=[kspec, kspec, vspec, vspec, kspec, sspec, _const_spec((1, GLA_VW))],
        out_specs=[vspec, sspec],
        out_shape=[jax.ShapeDtypeStruct((Bn, T, GLA_VW), BF16),
                   jax.ShapeDtypeStruct((Bn, GLA_HEADS, GLA_DK, GLA_DV), F32)],
        scratch_shapes=[pltpu.VMEM((GLA_DV, GLA_KW), F32)],
        compiler_params=_cparams(("arbitrary", "arbitrary")),
        name="gla",
    )(qg, kg, vg, rg, lg, s0, gn)


def _merge_ffn_kernel(x_ref, yp_ref, ym_ref, yg_ref, sg_ref, wp_ref, wmb_ref, wgl_ref, wo_ref,
                      gpost_ref, gfpre_ref, gfpost_ref, w1_ref, w2_ref, o_ref):
    tm = x_ref.shape[0]
    subs = [slice(r, r + FFN_SUB_ROWS) for r in range(0, tm, FFN_SUB_ROWS)] if tm > FFN_SUB_ROWS else [slice(0, tm)]

    def dot(a, w):
        return jnp.dot(a.astype(BF16), w, preferred_element_type=F32)

    branches = [[dot(y_ref[r, :], w_ref[...]) for y_ref, w_ref in
                 ((yp_ref, wp_ref), (ym_ref, wmb_ref), (yg_ref, wgl_ref))] for r in subs]
    merged = [sum(sg_ref[r, 1024 * j:1024 * (j + 1)] * b for j, b in enumerate(bs))
              for r, bs in zip(subs, branches)]
    z = [dot(m, wo_ref[...]) for m in merged]
    x1 = [x_ref[r, :] + _rms(zi, gpost_ref[...]) for r, zi in zip(subs, z)]
    hb = [_rms(xi, gfpre_ref[...]).astype(BF16) for xi in x1]
    FC = 1024
    f = [jnp.zeros(xi.shape, F32) for xi in x1]

    def second(i, c, a):
        a = jnp.maximum(a, 0.0)
        f[i] = f[i] + dot(a * a, w2_ref[c * FC:(c + 1) * FC, :])

    pending = None
    for c in range(D_FF // FC):
        for i in range(len(subs)):
            a = jnp.dot(hb[i], w1_ref[:, c * FC:(c + 1) * FC], preferred_element_type=F32)
            if pending is not None:
                second(*pending)
            pending = (i, c, a)
    second(*pending)
    for i, r in enumerate(subs):
        o_ref[r, :] = x1[i] + _rms(f[i], gfpost_ref[...])


def _merge_ffn(x, yp, ym, yg, sg, wp, wmb, wgl, wo, gpost, gfpre, gfpost, w1, w2):
    M = x.shape[0]
    tm = min(FFN_ROW_TILE, M)
    row = lambda w: pl.BlockSpec((tm, w), lambda i: (i, 0))
    vec = _const_spec((1, D_MODEL))
    return pl.pallas_call(
        _merge_ffn_kernel,
        grid=(M // tm,),
        in_specs=[row(D_MODEL), row(POOL_W), row(MB_W), row(GLA_VW), row(3 * D_MODEL),
                  _const_spec(wp.shape), _const_spec(wmb.shape), _const_spec(wgl.shape), _const_spec(wo.shape),
                  vec, vec, vec, _const_spec(w1.shape), _const_spec(w2.shape)],
        out_specs=row(D_MODEL),
        out_shape=jax.ShapeDtypeStruct((M, D_MODEL), F32),
        compiler_params=_cparams(("arbitrary",)),
        name="merge_ffn",
    )(x, yp, ym, yg, sg, wp, wmb, wgl, wo, gpost, gfpre, gfpost, w1, w2)


def _layer(x, Bn, T, tabs, hist16, pos0, s0, sample_attend, lw, prompt_state=None):
    (wm, wa, wg, wa2, ba, gn, wgrp, pscale, wp, wmb, wgl, wo, gpre, gpost, gfpre, gfpost, w1, w2) = lw
    seq = lambda a: a.reshape(Bn, T, a.shape[-1])
    if sample_attend is None:
        depth, layer = prompt_state[:2]
        u, q, kb, km, k, v, qg, kg, vg, rg, lg, sg = _in_proj(x, gpre, wm, wa, wg, wa2, ba, tabs,
                                                             prompt_state=(T,) + tuple(prompt_state))
        ym = _moba_prompt(seq(q), km.reshape(Bn, T // MB_BLOCK, MB_W), seq(kb), v, layer)
    else:
        u, q, k, v, qg, kg, vg, rg, lg, sg = _in_proj(x, gpre, wm, wa, wg, wa2, ba, tabs)
        ym = sample_attend(seq(q), seq(k), seq(v))
        k, v = (a.reshape(Bn, T, MB_HEADS, MB_HD) for a in (k, v))
    u3 = seq(u)
    yp = _pool(u3, hist16, wgrp, pscale, pos0)
    yg, s_new = _gla(seq(qg), seq(kg), seq(vg), seq(rg), seq(lg), s0, gn)
    yg = yg.reshape(Bn * T, GLA_VW)
    x = _merge_ffn(x, yp.reshape(Bn * T, POOL_W), ym.reshape(Bn * T, MB_W), yg, sg,
                   wp, wmb, wgl, wo, gpost, gfpre, gfpost, w1, w2)
    new_hist = jnp.concatenate([hist16[:, 1:], u3], axis=1)[:, -POOL_HIST:]
    return x, k, v, new_hist, s_new


def kernel(x_prompt, x_sample, cache_k, cache_v, state_pool, state_gla, page_table, w_in, w_gla_a2, b_gla_a,
           gla_norm, w_pool_grp, pool_scale, w_pool_out, w_mb_out, w_gla_out, w_o, norm_mix_pre,
           norm_mix_post, norm_ffn_pre, norm_ffn_post, w_ff1, w_ff2):
    B, T, D = x_prompt.shape
    DB, TS, _ = x_sample.shape
    depth = w_in.shape[0]
    n_pages = page_table.shape[1]
    past_len = n_pages * PAGE_SIZE
    assert D == D_MODEL and T % MB_BLOCK == 0 and past_len % MB_BLOCK == 0
    assert past_len // MB_BLOCK >= MB_TOPK and (B * T) % ROW_TILE == 0 and MB_BLOCK == 2 * PAGE_SIZE

    tabs_p = _rope_tables(jnp.arange(T))
    reps = max(1, min(ROW_TILE, DB * TS) // TS)
    tabs_s = tuple(jnp.tile(t, (reps, 1)) for t in _rope_tables(past_len + jnp.arange(TS)))
    ckt = jnp.transpose(cache_k, (0, 1, 3, 4, 2))
    cvt = jnp.transpose(cache_v, (0, 1, 3, 4, 2))
    zrow = lambda n: jnp.zeros((n, 1, POOL_W), F32)
    hist_p = jnp.zeros((B, 16, POOL_W), F32)
    s0_p = jnp.zeros((B, GLA_HEADS, GLA_DK, GLA_DV), F32)

    hp = x_prompt.reshape(B * T, D)
    hs = x_sample.reshape(DB * TS, D)
    outs_p, outs_s = [], []
    kt_all = vt_all = None
    for l in range(depth):
        bf = lambda a: a.astype(BF16)
        vec = lambda a: a.reshape(1, -1)
        lw = (bf(w_in[l][:, :W_MAIN]), bf(w_in[l][:, W_MAIN:W_MAIN + GLA_RANK]), bf(w_in[l][:, W_MAIN + GLA_RANK:]),
              bf(w_gla_a2[l]), vec(b_gla_a[l]), vec(gla_norm[l]), bf(w_pool_grp[l]), vec(pool_scale[l]),
              bf(w_pool_out[l]), bf(w_mb_out[l]), bf(w_gla_out[l]), bf(w_o[l]),
              vec(norm_mix_pre[l]), vec(norm_mix_post[l]), vec(norm_ffn_pre[l]), vec(norm_ffn_post[l]),
              bf(w_ff1[l]), bf(w_ff2[l]))
        wvt = bf(w_in[l][:, POOL_W + 2 * MB_W:POOL_W + 3 * MB_W].T)
        hp, kt_all, vt_all, *op = _layer(hp, B, T, tabs_p, hist_p, 0, s0_p, None, lw,
                                         prompt_state=(depth, l, wvt, kt_all, vt_all))
        hist_s = jnp.concatenate([zrow(DB), state_pool[l]], axis=1)
        attend_s = functools.partial(_moba_sample, cache_kt=ckt, cache_vt=cvt, page_table=page_table, layer=l)
        hs, *os_ = _layer(hs, DB, TS, tabs_s, hist_s, past_len, state_gla[l], attend_s, lw)
        outs_p.append(op)
        outs_s.append(os_)
    stack = lambda outs, i: jnp.stack([o[i] for o in outs])
    untr = lambda a: jnp.transpose(a.reshape(depth, B, MB_HEADS, MB_HD, T), (0, 1, 4, 2, 3))
    return (hp.reshape(B, T, D), hs.reshape(DB, TS, D),
            untr(kt_all), untr(vt_all), stack(outs_p, 0), stack(outs_p, 1),
            stack(outs_s, 0), stack(outs_s, 1), stack(outs_s, 2), stack(outs_s, 3))
```

```python
import functools
import math

import jax
import jax.numpy as jnp
from jax import lax
from jax.experimental import pallas as pl
from jax.experimental.pallas import tpu as pltpu

F32 = jnp.float32
BF16 = jnp.bfloat16

D_MODEL = 1024
PAGE_SIZE = 128
POOL_GC = 128
POOL_W = 512
POOL_WINDOWS = (2, 4, 8, 16)
POOL_HIST = 15
MB_HEADS = 8
MB_HD = 64
MB_W = 512
MB_BLOCK = 256
MB_TOPK = 3
ROPE_THETA = 500000.0
ROPE_DIM = 16
GLA_HEADS = 4
GLA_DK = 64
GLA_DV = 128
GLA_KW = 256
GLA_VW = 512
GLA_RANK = 16
GLA_NORMALIZER = 16.0
GLA_CHUNK = 64
D_FF = 4096
EPS = 1e-6
W_MAIN = POOL_W + 3 * MB_W + 2 * GLA_KW + 2 * GLA_VW
NEG_INF = float("-inf")
LOG2E = 1.4426950408889634

VMEM_LIMIT_BYTES = 56 * 1024 * 1024
ROW_TILE = 512
FFN_ROW_TILE = 512
FFN_SUB_ROWS = 256
SAMPLE_BLOCKS_PER_STEP = 16
GLA_GROUP = 8


def _cparams(sem):
    return pltpu.CompilerParams(dimension_semantics=sem, vmem_limit_bytes=VMEM_LIMIT_BYTES)


def _const_spec(shape):
    nd = len(shape)
    return pl.BlockSpec(shape, lambda *_: (0,) * nd, pipeline_mode=pl.Buffered(1))


def _rms(x, g):
    ms = jnp.mean(x * x, axis=-1, keepdims=True)
    return x * lax.rsqrt(ms + EPS) * g


def _sigmoid(x):
    return 0.5 * jnp.tanh(0.5 * x) + 0.5


def _split3(x):
    hi = x.astype(BF16)
    r = x - hi.astype(F32)
    mid = r.astype(BF16)
    lo = (r - mid.astype(F32)).astype(BF16)
    return hi, mid, lo


def _in_proj_kernel(*refs, prompt, first_layer):
    x_ref, gpre_ref, wm_ref, wa_ref, wg_ref, wa2_ref, ba_ref, cos_ref, sa_ref, sb_ref = refs[:10]
    n_in = 10 + (0 if not prompt else 1 if first_layer else 3)
    wvt_ref = refs[10] if prompt else None
    outs = refs[n_in:]
    u_ref, q_ref = outs[:2]
    qg_ref, kg_ref, vg_ref, rg_ref, lg_ref, sg_ref = outs[-6:]
    h = _rms(x_ref[...], gpre_ref[...]).astype(BF16)

    def proj(w_ref, a, b):
        return jnp.dot(h, w_ref[:, a:b], preferred_element_type=F32)

    cos, sa, sb = cos_ref[...], sa_ref[...], sb_ref[...]

    def rope(col0):
        half = ROPE_DIM // 2
        x = proj(wm_ref, col0, col0 + MB_W)
        for j in range(MB_W // 128):
            xj = x[:, 128 * j:128 * (j + 1)]
            yield j, xj * cos + pltpu.roll(xj, 128 - half, 1) * sa + pltpu.roll(xj, half, 1) * sb

    for j in range(3):
        sg_ref[:, 1024 * j:1024 * (j + 1)] = _sigmoid(proj(wg_ref, 1024 * j, 1024 * (j + 1))).astype(BF16)
    r = proj(wm_ref, 3072, 3584)
    rg_ref[...] = r * _sigmoid(r)
    a = jnp.dot(h, wa_ref[...], preferred_element_type=F32)
    xg = jnp.dot(a.astype(BF16), wa2_ref[...], preferred_element_type=F32) + ba_ref[...]
    lg_ref[...] = (jnp.minimum(xg, 0.0) - jnp.log1p(jnp.exp(-jnp.abs(xg)))) * (1.0 / GLA_NORMALIZER)
    u_ref[...] = proj(wm_ref, 0, 512)
    for j, rj in rope(512):
        q_ref[:, 128 * j:128 * (j + 1)] = rj
    if prompt:
        kb_ref, km_ref, kt_ref, vt_ref = outs[2:6]
        n_blk = km_ref.shape[0]
        col_sums = [[] for _ in range(n_blk)]
        for j, rj in rope(1024):
            kb_ref[:, 128 * j:128 * (j + 1)] = rj.astype(BF16)
            kt_ref[0, 0, 128 * j:128 * (j + 1), :] = rj.T
            for i in range(n_blk):
                col_sums[i].append(jnp.sum(rj[i * MB_BLOCK:(i + 1) * MB_BLOCK], axis=0, keepdims=True))
        for i in range(n_blk):
            km_ref[i] = jnp.concatenate(col_sums[i], axis=1) * (1.0 / MB_BLOCK)
        vt_ref[0, 0] = _dot_nt(wvt_ref[...], h)
        for d in range(1, kt_ref.shape[0]):
            kt_ref[d] = jnp.zeros(kt_ref.shape[1:], F32)
            vt_ref[d] = jnp.zeros(vt_ref.shape[1:], F32)
    else:
        k_ref, v_ref = outs[2:4]
        for j, rj in rope(1024):
            k_ref[:, 128 * j:128 * (j + 1)] = rj
        v_ref[...] = proj(wm_ref, 1536, 2048)
    qg_ref[...] = proj(wm_ref, 2048, 2304)
    kg_ref[...] = proj(wm_ref, 2304, 2560)
    vg_ref[...] = proj(wm_ref, 2560, 3072)


def _in_proj(x, gpre, wm, wa, wg, wa2, ba, tabs, prompt_state=None):
    M = x.shape[0]
    tm = min(ROW_TILE, M)
    period = tabs[0].shape[0] // tm
    row = lambda w: pl.BlockSpec((tm, w), lambda i: (i, 0))
    tab = pl.BlockSpec((tm, 128), lambda i: (i % period, 0))
    f32 = lambda w: jax.ShapeDtypeStruct((M, w), F32)
    tail_w = (GLA_KW, GLA_KW, GLA_VW, GLA_VW, GLA_KW, 3 * D_MODEL)
    in_specs = [row(D_MODEL), _const_spec((1, D_MODEL)), _const_spec(wm.shape), _const_spec(wa.shape),
                _const_spec(wg.shape), _const_spec(wa2.shape), _const_spec((1, GLA_KW)), tab, tab, tab]
    args = [x, gpre, wm, wa, wg, wa2, ba, *tabs]
    aliases = {}
    if prompt_state is None:
        out_specs = [row(POOL_W), row(MB_W), row(MB_W), row(MB_W)]
        out_shape = [f32(POOL_W), f32(MB_W), f32(MB_W), f32(MB_W)]
        first = False
    else:
        seq_len, depth, layer, wvt, kt_all, vt_all = prompt_state
        assert tm % MB_BLOCK == 0 and seq_len % tm == 0
        tps = seq_len // tm
        first = layer == 0
        stacked = jax.ShapeDtypeStruct((depth, M // seq_len, MB_W, seq_len), F32)
        in_specs.append(_const_spec(wvt.shape))
        args.append(wvt)
        if first:
            st_spec = pl.BlockSpec((depth, 1, MB_W, tm), lambda i: (0, i // tps, 0, i % tps))
        else:
            st_spec = pl.BlockSpec((1, 1, MB_W, tm), lambda i: (layer, i // tps, 0, i % tps))
            in_specs += [pl.BlockSpec(memory_space=pl.ANY)] * 2
            args += [kt_all, vt_all]
            aliases = {11: 4, 12: 5}
        out_specs = [row(POOL_W), row(MB_W), row(MB_W),
                     pl.BlockSpec((tm // MB_BLOCK, 1, MB_W), lambda i: (i, 0, 0)),
                     st_spec, st_spec]
        out_shape = [f32(POOL_W), f32(MB_W), jax.ShapeDtypeStruct((M, MB_W), BF16),
                     jax.ShapeDtypeStruct((M // MB_BLOCK, 1, MB_W), F32), stacked, stacked]
    return pl.pallas_call(
        functools.partial(_in_proj_kernel, prompt=prompt_state is not None, first_layer=first),
        grid=(M // tm,),
        in_specs=in_specs,
        out_specs=out_specs + [row(w) for w in tail_w],
        out_shape=out_shape + [f32(w) for w in tail_w[:-1]] + [jax.ShapeDtypeStruct((M, tail_w[-1]), BF16)],
        input_output_aliases=aliases,
        compiler_params=_cparams(("arbitrary",)),
        name="in_proj",
    )(*args)


def _rope_tables(pos):
    half = ROPE_DIM // 2
    inv = ROPE_THETA ** (-jnp.arange(half, dtype=F32) / half)
    ang = pos.astype(F32)[:, None] * inv[None, :]
    cos, sin = jnp.cos(ang), jnp.sin(ang)
    n = pos.shape[0]
    zeros = jnp.zeros((n, MB_HD - ROPE_DIM), F32)
    zh = jnp.zeros((n, half), F32)
    c = jnp.concatenate([cos, cos, zeros + 1.0], axis=1)
    sa = jnp.concatenate([-sin, zh, zeros], axis=1)
    sb = jnp.concatenate([zh, sin, zeros], axis=1)
    two = lambda t: jnp.concatenate([t, t], axis=1)
    return two(c), two(sa), two(sb)


def _pool_kernel(u_ref, hist_ref, wgrp_ref, scale_ref, y_ref, ext_ref, *, pos0, T, R, nb):
    for s in range(nb):
        ext_ref[s, 0:16, :] = hist_ref[s]
        ext_ref[s, 16:16 + T, :] = u_ref[s]
    for c in range(T // R):
        r0 = 16 + c * R
        posp1 = lax.broadcasted_iota(jnp.int32, (R, 1), 0) + (pos0 + c * R + 1)
        for g, w in enumerate(POOL_WINDOWS):
            lanes = slice(g * POOL_GC, (g + 1) * POOL_GC)
            cnt = jnp.minimum(posp1, w).astype(F32)
            ds = []
            for s in range(nb):
                cur = ext_ref[s, r0:r0 + R, lanes]
                acc = cur
                for j in range(1, w):
                    acc = acc + ext_ref[s, r0 - j:r0 - j + R, lanes]
                ds.append(acc / cnt - cur)
            d = ds[0] if nb == 1 else jnp.concatenate(ds, axis=0)
            yg = jnp.dot(d.astype(BF16), wgrp_ref[g], preferred_element_type=F32) * scale_ref[:, lanes]
            for s in range(nb):
                y_ref[s, c * R:(c + 1) * R, lanes] = yg[s * R:(s + 1) * R].astype(BF16)


def _pool(u, hist16, wgrp, scale, pos0):
    Bn, T, _ = u.shape
    R = min(T, 256)
    nb = Bn if T < MB_BLOCK else 1
    seq = pl.BlockSpec((nb, T, POOL_W), lambda b: (b, 0, 0))
    return pl.pallas_call(
        functools.partial(_pool_kernel, pos0=pos0, T=T, R=R, nb=nb),
        grid=(Bn // nb,),
        in_specs=[seq, pl.BlockSpec((nb, 16, POOL_W), lambda b: (b, 0, 0)),
                  _const_spec(wgrp.shape), _const_spec((1, POOL_W))],
        out_specs=seq,
        out_shape=jax.ShapeDtypeStruct((Bn, T, POOL_W), BF16),
        scratch_shapes=[pltpu.VMEM((nb, 16 + T, POOL_W), F32)],
        compiler_params=_cparams(("arbitrary",)),
        name="pool",
    )(u, hist16, wgrp, scale)


def _top_blocks(gate, n_valid, axis):
    n_blocks = gate.shape[axis]
    iota = lax.broadcasted_iota(jnp.int32, gate.shape, axis)
    g = jnp.where(iota < n_valid, gate, NEG_INF)
    sel = jnp.zeros(gate.shape, F32)
    for _ in range(MB_TOPK):
        m = jnp.max(g, axis=axis, keepdims=True)
        idx = jnp.min(jnp.where(g == m, iota, n_blocks), axis=axis, keepdims=True)
        pick = (iota == idx) & (m > NEG_INF)
        sel = jnp.where(pick, 1.0, sel)
        g = jnp.where(pick, NEG_INF, g)
    return sel


def _col_max(s):
    r, c = s.shape
    return jnp.max(jnp.max(s.reshape(r // 8, 8, c), axis=0), axis=0, keepdims=True)


def _dot_nt(a, b, precision=None):
    return lax.dot_general(a, b, (((1,), (1,)), ((), ())), preferred_element_type=F32, precision=precision)


def _moba_prompt_kernel(q_ref, km_ref, k_ref, vt_ref, y_ref, s_ref, vtb_ref):
    B = MB_BLOCK
    n_blocks = km_ref.shape[1]
    qlane = lax.broadcasted_iota(jnp.int32, (B, 128), 1) // MB_HD
    kmlane = lax.broadcasted_iota(jnp.int32, (n_blocks, 128), 1) // MB_HD
    causal = lax.broadcasted_iota(jnp.int32, (B, B), 0) <= lax.broadcasted_iota(jnp.int32, (B, B), 1)
    drow = lax.broadcasted_iota(jnp.int32, (128, B), 0) // MB_HD
    vtb_ref[...] = vt_ref[0, 0].astype(BF16)

    def scores_stage(c):
        q = q_ref[0, c * B:(c + 1) * B, :]
        qh = [jnp.where(qlane == hh, q, 0.0) for hh in range(2)]
        qs = [(x * (MB_HD ** -0.5 * LOG2E)).astype(BF16) for x in qh]
        raw = {(hh, j): _dot_nt(k_ref[0, j * B:(j + 1) * B, :], qs[hh])
               for hh in range(2) for j in range(c + 1)}
        maxima = []
        for hh in range(2):
            if c > 0:
                gate_t = _dot_nt(jnp.where(kmlane == hh, km_ref[0], 0.0), qh[hh], precision=lax.Precision.HIGHEST)
                sel = jnp.where(_top_blocks(gate_t, c, axis=0) > 0.0, 0.0, NEG_INF)
            m = None
            for j in range(c + 1):
                s = jnp.where(causal, raw[hh, j], NEG_INF) if j == c else raw[hh, j] + sel[j:j + 1, :]
                s_ref[2 * (c % 2) + hh, j] = s
                mj = _col_max(s)
                m = mj if m is None else jnp.maximum(m, mj)
            maxima.append(m)
        return maxima

    def values_stage(c, maxima):
        outs = []
        for hh in range(2):
            l = jnp.zeros((1, B), F32)
            probs = []
            for j in range(c + 1):
                p = jnp.exp2(s_ref[2 * (c % 2) + hh, j] - maxima[hh])
                l = l + jnp.sum(p, axis=0, keepdims=True)
                probs.append(p.astype(BF16))
            acc = jnp.dot(vtb_ref[:, :(c + 1) * B], jnp.concatenate(probs, axis=0), preferred_element_type=F32)
            outs.append(acc / l)
        y_ref[0, c * B:(c + 1) * B, :] = jnp.where(drow == 0, outs[0], outs[1]).T.astype(BF16)

    pending = None
    for c in range(n_blocks):
        maxima = scores_stage(c)
        if pending is not None:
            values_stage(*pending)
        pending = (c, maxima)
    values_stage(*pending)


def _moba_prompt(q, km, kb, vt_all, layer):
    Bn, T, _ = q.shape
    n_blocks = T // MB_BLOCK
    seq = pl.BlockSpec((1, T, 128), lambda b, hp: (b, 0, hp))
    return pl.pallas_call(
        _moba_prompt_kernel,
        grid=(Bn, MB_W // 128),
        in_specs=[seq,
                  pl.BlockSpec((1, n_blocks, 128), lambda b, hp: (b, 0, hp)),
                  seq,
                  pl.BlockSpec((1, 1, 128, T), lambda b, hp: (layer, b, hp, 0))],
        out_specs=seq,
        out_shape=jax.ShapeDtypeStruct((Bn, T, MB_W), BF16),
        scratch_shapes=[pltpu.VMEM((4, n_blocks, MB_BLOCK, MB_BLOCK), F32), pltpu.VMEM((128, T), BF16)],
        compiler_params=_cparams(("arbitrary", "arbitrary")),
        name="moba_prompt",
    )(q, km, kb, vt_all)


def _moba_sample_kernel(pt_ref, q_ref, kn_ref, vn_ref, *refs, n_blocks, ts, bps):
    ppb = MB_BLOCK // PAGE_SIZE
    k_refs = refs[:bps * ppb]
    v_refs = refs[bps * ppb:2 * bps * ppb]
    y_ref, qbd_ref, oblk_ref, km_ref, m_ref, l_ref = refs[2 * bps * ppb:]
    step = pl.program_id(1)
    R = MB_HEADS * ts
    own = (lax.broadcasted_iota(jnp.int32, (R, MB_W), 0) // ts
           == lax.broadcasted_iota(jnp.int32, (R, MB_W), 1) // MB_HD)
    blane = lax.broadcasted_iota(jnp.int32, (R, n_blocks), 1)
    kml = lax.broadcasted_iota(jnp.int32, (MB_W, n_blocks), 1)

    @pl.when(step == 0)
    def _():
        for ref in (m_ref, l_ref, km_ref):
            ref[...] = jnp.zeros(ref.shape, F32)
        qbd_ref[...] = jnp.where(own, jnp.concatenate([q_ref[0]] * MB_HEADS, axis=0), 0.0)

    qbd = qbd_ref[...]
    qb = (qbd * (MB_HD ** -0.5)).astype(BF16)
    page = lambda ref: ref[0, 0].reshape(MB_W, PAGE_SIZE)
    m_all, l_all, km_all = m_ref[...], l_ref[...], km_ref[...]
    blocks = lambda refs, i: jnp.concatenate([page(refs[ppb * i + j]) for j in range(ppb)], axis=1)
    scores, probs = [], []
    for i in range(bps):
        kblk = blocks(k_refs, i)
        scores.append(jnp.dot(qb, kblk.astype(BF16), preferred_element_type=F32))
        kmean = jnp.sum(kblk, axis=1, keepdims=True) * (1.0 / MB_BLOCK)
        km_all = jnp.where(kml == step * bps + i, kmean, km_all)
    for i, s in enumerate(scores):
        n = step * bps + i
        m_n = jnp.max(s, axis=1, keepdims=True)
        p = jnp.exp(s - m_n)
        probs.append(p.astype(BF16))
        m_all = jnp.where(blane == n, m_n, m_all)
        l_all = jnp.where(blane == n, jnp.sum(p, axis=1, keepdims=True), l_all)
    for i, p in enumerate(probs):
        oblk_ref[step * bps + i] = _dot_nt(p, blocks(v_refs, i).astype(BF16))
    km_ref[...] = km_all
    m_ref[...] = m_all
    l_ref[...] = l_all

    @pl.when(step == n_blocks // bps - 1)
    def _():
        gate = jnp.dot(qbd, km_ref[...], preferred_element_type=F32, precision=lax.Precision.HIGHEST)
        sel = _top_blocks(gate, n_blocks, axis=1) > 0.0
        s_own = _dot_nt(qb, kn_ref[0].astype(BF16))
        tq = lax.broadcasted_iota(jnp.int32, (R, ts), 0) % ts
        tk = lax.broadcasted_iota(jnp.int32, (R, ts), 1)
        s_own = jnp.where(tk <= tq, s_own, NEG_INF)
        mb = jnp.where(sel, m_ref[...], NEG_INF)
        m_all = jnp.maximum(jnp.max(s_own, axis=1, keepdims=True), jnp.max(mb, axis=1, keepdims=True))
        w = jnp.exp(mb - m_all)
        p_own = jnp.exp(s_own - m_all)
        l = jnp.sum(p_own, axis=1, keepdims=True) + jnp.sum(w * l_ref[...], axis=1, keepdims=True)
        o = jnp.dot(p_own.astype(BF16), vn_ref[0].astype(BF16), preferred_element_type=F32)
        for nb in range(n_blocks):
            o = o + w[:, nb:nb + 1] * oblk_ref[nb]
        o = jnp.where(own, o / l, 0.0)
        y = o[0:ts]
        for h in range(1, MB_HEADS):
            y = y + o[h * ts:(h + 1) * ts]
        y_ref[0] = y.astype(BF16)


def _moba_sample(q, kn, vn, cache_kt, cache_vt, page_table, layer):
    DB, ts, _ = q.shape
    n_pages = page_table.shape[1]
    ppb = MB_BLOCK // PAGE_SIZE
    n_blocks = n_pages // ppb
    R = MB_HEADS * ts
    tok = pl.BlockSpec((1, ts, MB_W), lambda b, n, pt: (b, 0, 0))
    bps = math.gcd(n_blocks, SAMPLE_BLOCKS_PER_STEP)
    pps = bps * ppb

    def page(i):
        return pl.BlockSpec((1, 1, MB_HEADS, MB_HD, PAGE_SIZE),
                            lambda b, n, pt: (layer, pt[b * n_pages + n * pps + i], 0, 0, 0))

    pages = [page(i) for i in range(pps)]
    grid_spec = pltpu.PrefetchScalarGridSpec(
        num_scalar_prefetch=1,
        grid=(DB, n_blocks // bps),
        in_specs=[tok, tok, tok] + pages + pages,
        out_specs=tok,
        scratch_shapes=[pltpu.VMEM((R, MB_W), F32),
                        pltpu.VMEM((n_blocks, R, MB_W), F32),
                        pltpu.VMEM((MB_W, n_blocks), F32),
                        pltpu.VMEM((R, n_blocks), F32),
                        pltpu.VMEM((R, n_blocks), F32)],
    )
    return pl.pallas_call(
        functools.partial(_moba_sample_kernel, n_blocks=n_blocks, ts=ts, bps=bps),
        grid_spec=grid_spec,
        out_shape=jax.ShapeDtypeStruct((DB, ts, MB_W), BF16),
        compiler_params=_cparams(("arbitrary", "arbitrary")),
        name="moba_sample",
    )(page_table.reshape(-1), q, kn, vn, *([cache_kt] * pps), *([cache_vt] * pps))


def _gla_kernel(qg_ref, kg_ref, vg_ref, rg_ref, lg_ref, s0_ref, gn_ref, y_ref, sfin_ref, st_ref, *,
                n_chunks, valid):
    t = pl.program_id(1)
    C = GLA_CHUNK

    @pl.when(t == 0)
    def _():
        for h in range(GLA_HEADS):
            st_ref[:, h * GLA_DK:(h + 1) * GLA_DK] = s0_ref[0, h].T

    ri = lax.broadcasted_iota(jnp.int32, (C, C), 0)
    ci = lax.broadcasted_iota(jnp.int32, (C, C), 1)
    tri = (ci <= ri).astype(BF16)
    m1 = (ri >= 32) & (ci < 32)
    m2 = (ri // 32 == ci // 32) & (ri % 32 >= 16) & (ci % 32 < 16)
    m3 = (ri // 16 == ci // 16) & (ci <= ri)
    grp = lax.broadcasted_iota(jnp.int32, (C, GLA_KW), 0)

    heads = [(slice(h * GLA_DK, (h + 1) * GLA_DK), slice(h * GLA_DV, (h + 1) * GLA_DV)) for h in range(GLA_HEADS)]
    sts = [st_ref[:, kl] for kl, _ in heads]
    for g0 in range(0, n_chunks, GLA_GROUP):
        cs = range(g0, min(g0 + GLA_GROUP, n_chunks))
        rows = {c: slice(c * C, c * C + valid) for c in cs}

        def chunk(ref, c, lanes=slice(None)):
            x = ref[0, rows[c], lanes]
            return x if valid == C else jnp.concatenate([x, jnp.zeros((C - valid, x.shape[1]), x.dtype)], axis=0)

        bcum = {}
        for c in cs:
            hi, mid, lo = _split3(chunk(lg_ref, c))
            bcum[c] = (jnp.dot(tri, hi, preferred_element_type=F32) + jnp.dot(tri, mid, preferred_element_type=F32)
                       + jnp.dot(tri, lo, preferred_element_type=F32))
        scaled = {}
        for c in cs:
            b = bcum[c]
            q = chunk(qg_ref, c) * (GLA_DK ** -0.5)
            k = chunk(kg_ref, c)
            brow = lambda i: jnp.broadcast_to(b[i:i + 1, :], (C, GLA_KW))
            ref1 = brow(31)
            ref2 = jnp.where(grp < 32, brow(15), brow(47))
            ref3 = jnp.where(grp < 16, brow(7),
                             jnp.where(grp < 32, brow(23), jnp.where(grp < 48, brow(39), brow(55))))
            b_last = b[C - 1:C, :]
            scaled[c] = dict(
                qs=(q * jnp.exp(b)).astype(BF16), ks=(k * jnp.exp(b_last - b)).astype(BF16),
                q1=(q * jnp.exp(jnp.minimum(b - ref1, 0.0))).astype(BF16),
                k1=(k * jnp.exp(jnp.minimum(ref1 - b, 0.0))).astype(BF16),
                q2=(q * jnp.exp(jnp.minimum(b - ref2, 0.0))).astype(BF16),
                k2=(k * jnp.exp(jnp.minimum(ref2 - b, 0.0))).astype(BF16),
                q3=(q * jnp.exp(b - ref3)).astype(BF16), k3=(k * jnp.exp(ref3 - b)).astype(BF16),
                decay=jnp.exp(b_last))
        levels = {}
        for c in cs:
            x = scaled[c]
            for h, (kl, _) in enumerate(heads):
                levels[c, h] = (_dot_nt(x["q1"][:, kl], x["k1"][:, kl]), _dot_nt(x["q2"][:, kl], x["k2"][:, kl]),
                                _dot_nt(x["q3"][:, kl], x["k3"][:, kl]))
        intra, update = {}, {}
        for c in cs:
            for h, (kl, vl) in enumerate(heads):
                l1, l2, l3 = levels[c, h]
                a = jnp.where(m1, l1, 0.0) + jnp.where(m2, l2, 0.0) + jnp.where(m3, l3, 0.0)
                vf = chunk(vg_ref, c, vl)
                intra[c, h] = jnp.dot(a.astype(BF16), vf.astype(BF16), preferred_element_type=F32)
                update[c, h] = jnp.dot(vf.T.astype(BF16), scaled[c]["ks"][:, kl], preferred_element_type=F32)
        for c in cs:
            for h, (kl, vl) in enumerate(heads):
                o = _dot_nt(scaled[c]["qs"][:, kl], sts[h].astype(BF16)) + intra[c, h]
                sts[h] = sts[h] * scaled[c]["decay"][:, kl] + update[c, h]
                y_ref[0, rows[c], vl] = (_rms(o[:valid], gn_ref[:, vl]) * rg_ref[0, rows[c], vl]).astype(BF16)
    for (kl, _), st in zip(heads, sts):
        st_ref[:, kl] = st

    @pl.when(t == pl.num_programs(1) - 1)
    def _():
        for h in range(GLA_HEADS):
            sfin_ref[0, h] = st_ref[:, h * GLA_DK:(h + 1) * GLA_DK].T


def _gla(qg, kg, vg, rg, lg, s0, gn):
    Bn, T, _ = qg.shape
    assert T < GLA_CHUNK or T % GLA_CHUNK == 0
    TT = min(T, 512)
    kspec = pl.BlockSpec((1, TT, GLA_KW), lambda b, t: (b, t, 0))
    vspec = pl.BlockSpec((1, TT, GLA_VW), lambda b, t: (b, t, 0))
    sspec = pl.BlockSpec((1, GLA_HEADS, GLA_DK, GLA_DV), lambda b, t: (b, 0, 0, 0))
    return pl.pallas_call(
        functools.partial(_gla_kernel, n_chunks=max(1, TT // GLA_CHUNK), valid=min(TT, GLA_CHUNK)),
        grid=(Bn, T // TT),
        in_specs=[kspec, kspec, vspec, vspec, kspec, sspec, _const_spec((1, GLA_VW))],
        out_specs=[vspec, sspec],
        out_shape=[jax.ShapeDtypeStruct((Bn, T, GLA_VW), BF16),
                   jax.ShapeDtypeStruct((Bn, GLA_HEADS, GLA_DK, GLA_DV), F32)],
        scratch_shapes=[pltpu.VMEM((GLA_DV, GLA_KW), F32)],
        compiler_params=_cparams(("arbitrary", "arbitrary")),
        name="gla",
    )(qg, kg, vg, rg, lg, s0, gn)


def _merge_ffn_kernel(x_ref, yp_ref, ym_ref, yg_ref, sg_ref, wp_ref, wmb_ref, wgl_ref, wo_ref,
                      gpost_ref, gfpre_ref, gfpost_ref, w1_ref, w2_ref, o_ref):
    tm = x_ref.shape[0]
    subs = [slice(r, r + FFN_SUB_ROWS) for r in range(0, tm, FFN_SUB_ROWS)] if tm > FFN_SUB_ROWS else [slice(0, tm)]

    def dot(a, w):
        return jnp.dot(a.astype(BF16), w, preferred_element_type=F32)

    branches = [[dot(y_ref[r, :], w_ref[...]) for y_ref, w_ref in
                 ((yp_ref, wp_ref), (ym_ref, wmb_ref), (yg_ref, wgl_ref))] for r in subs]
    merged = [sum(sg_ref[r, 1024 * j:1024 * (j + 1)] * b for j, b in enumerate(bs))
              for r, bs in zip(subs, branches)]
    z = [dot(m, wo_ref[...]) for m in merged]
    x1 = [x_ref[r, :] + _rms(zi, gpost_ref[...]) for r, zi in zip(subs, z)]
    hb = [_rms(xi, gfpre_ref[...]).astype(BF16) for xi in x1]
    FC = 1024
    f = [jnp.zeros(xi.shape, F32) for xi in x1]

    def second(i, c, a):
        a = jnp.maximum(a, 0.0)
        f[i] = f[i] + dot(a * a, w2_ref[c * FC:(c + 1) * FC, :])

    pending = None
    for c in range(D_FF // FC):
        for i in range(len(subs)):
            a = jnp.dot(hb[i], w1_ref[:, c * FC:(c + 1) * FC], preferred_element_type=F32)
            if pending is not None:
                second(*pending)
            pending = (i, c, a)
    second(*pending)
    for i, r in enumerate(subs):
        o_ref[r, :] = x1[i] + _rms(f[i], gfpost_ref[...])


def _merge_ffn(x, yp, ym, yg, sg, wp, wmb, wgl, wo, gpost, gfpre, gfpost, w1, w2):
    M = x.shape[0]
    tm = min(FFN_ROW_TILE, M)
    row = lambda w: pl.BlockSpec((tm, w), lambda i: (i, 0))
    vec = _const_spec((1, D_MODEL))
    return pl.pallas_call(
        _merge_ffn_kernel,
        grid=(M // tm,),
        in_specs=[row(D_MODEL), row(POOL_W), row(MB_W), row(GLA_VW), row(3 * D_MODEL),
                  _const_spec(wp.shape), _const_spec(wmb.shape), _const_spec(wgl.shape), _const_spec(wo.shape),
                  vec, vec, vec, _const_spec(w1.shape), _const_spec(w2.shape)],
        out_specs=row(D_MODEL),
        out_shape=jax.ShapeDtypeStruct((M, D_MODEL), F32),
        compiler_params=_cparams(("arbitrary",)),
        name="merge_ffn",
    )(x, yp, ym, yg, sg, wp, wmb, wgl, wo, gpost, gfpre, gfpost, w1, w2)


def _layer(x, Bn, T, tabs, hist16, pos0, s0, sample_attend, lw, prompt_state=None):
    (wm, wa, wg, wa2, ba, gn, wgrp, pscale, wp, wmb, wgl, wo, gpre, gpost, gfpre, gfpost, w1, w2) = lw
    seq = lambda a: a.reshape(Bn, T, a.shape[-1])
    if sample_attend is None:
        depth, layer = prompt_state[:2]
        u, q, kb, km, k, v, qg, kg, vg, rg, lg, sg = _in_proj(x, gpre, wm, wa, wg, wa2, ba, tabs,
                                                             prompt_state=(T,) + tuple(prompt_state))
        ym = _moba_prompt(seq(q), km.reshape(Bn, T // MB_BLOCK, MB_W), seq(kb), v, layer)
    else:
        u, q, k, v, qg, kg, vg, rg, lg, sg = _in_proj(x, gpre, wm, wa, wg, wa2, ba, tabs)
        ym = sample_attend(seq(q), seq(k), seq(v))
        k, v = (a.reshape(Bn, T, MB_HEADS, MB_HD) for a in (k, v))
    u3 = seq(u)
    yp = _pool(u3, hist16, wgrp, pscale, pos0)
    yg, s_new = _gla(seq(qg), seq(kg), seq(vg), seq(rg), seq(lg), s0, gn)
    yg = yg.reshape(Bn * T, GLA_VW)
    x = _merge_ffn(x, yp.reshape(Bn * T, POOL_W), ym.reshape(Bn * T, MB_W), yg, sg,
                   wp, wmb, wgl, wo, gpost, gfpre, gfpost, w1, w2)
    new_hist = jnp.concatenate([hist16[:, 1:], u3], axis=1)[:, -POOL_HIST:]
    return x, k, v, new_hist, s_new


def kernel(x_prompt, x_sample, cache_k, cache_v, state_pool, state_gla, page_table, w_in, w_gla_a2, b_gla_a,
           gla_norm, w_pool_grp, pool_scale, w_pool_out, w_mb_out, w_gla_out, w_o, norm_mix_pre,
           norm_mix_post, norm_ffn_pre, norm_ffn_post, w_ff1, w_ff2):
    B, T, D = x_prompt.shape
    DB, TS, _ = x_sample.shape
    depth = w_in.shape[0]
    n_pages = page_table.shape[1]
    past_len = n_pages * PAGE_SIZE
    assert D == D_MODEL and T % MB_BLOCK == 0 and past_len % MB_BLOCK == 0
    assert past_len // MB_BLOCK >= MB_TOPK and (B * T) % ROW_TILE == 0 and MB_BLOCK == 2 * PAGE_SIZE

    tabs_p = _rope_tables(jnp.arange(T))
    reps = max(1, min(ROW_TILE, DB * TS) // TS)
    tabs_s = tuple(jnp.tile(t, (reps, 1)) for t in _rope_tables(past_len + jnp.arange(TS)))
    ckt = jnp.transpose(cache_k, (0, 1, 3, 4, 2))
    cvt = jnp.transpose(cache_v, (0, 1, 3, 4, 2))
    zrow = lambda n: jnp.zeros((n, 1, POOL_W), F32)
    hist_p = jnp.zeros((B, 16, POOL_W), F32)
    s0_p = jnp.zeros((B, GLA_HEADS, GLA_DK, GLA_DV), F32)

    hp = x_prompt.reshape(B * T, D)
    hs = x_sample.reshape(DB * TS, D)
    outs_p, outs_s = [], []
    kt_all = vt_all = None
    for l in range(depth):
        bf = lambda a: a.astype(BF16)
        vec = lambda a: a.reshape(1, -1)
        lw = (bf(w_in[l][:, :W_MAIN]), bf(w_in[l][:, W_MAIN:W_MAIN + GLA_RANK]), bf(w_in[l][:, W_MAIN + GLA_RANK:]),
              bf(w_gla_a2[l]), vec(b_gla_a[l]), vec(gla_norm[l]), bf(w_pool_grp[l]), vec(pool_scale[l]),
              bf(w_pool_out[l]), bf(w_mb_out[l]), bf(w_gla_out[l]), bf(w_o[l]),
              vec(norm_mix_pre[l]), vec(norm_mix_post[l]), vec(norm_ffn_pre[l]), vec(norm_ffn_post[l]),
              bf(w_ff1[l]), bf(w_ff2[l]))
        wvt = bf(w_in[l][:, POOL_W + 2 * MB_W:POOL_W + 3 * MB_W].T)
        hp, kt_all, vt_all, *op = _layer(hp, B, T, tabs_p, hist_p, 0, s0_p, None, lw,
                                         prompt_state=(depth, l, wvt, kt_all, vt_all))
        hist_s = jnp.concatenate([zrow(DB), state_pool[l]], axis=1)
        attend_s = functools.partial(_moba_sample, cache_kt=ckt, cache_vt=cvt, page_table=page_table, layer=l)
        hs, *os_ = _layer(hs, DB, TS, tabs_s, hist_s, past_len, state_gla[l], attend_s, lw)
        outs_p.append(op)
        outs_s.append(os_)
    stack = lambda outs, i: jnp.stack([o[i] for o in outs])
    untr = lambda a: jnp.transpose(a.reshape(depth, B, MB_HEADS, MB_HD, T), (0, 1, 4, 2, 3))
    return (hp.reshape(B, T, D), hs.reshape(DB, TS, D),
            untr(kt_all), untr(vt_all), stack(outs_p, 0), stack(outs_p, 1),
            stack(outs_s, 0), stack(outs_s, 1), stack(outs_s, 2), stack(outs_s, 3))
```

```python
import functools
import math

import jax
import jax.numpy as jnp
from jax import lax
from jax.experimental import pallas as pl
from jax.experimental.pallas import tpu as pltpu

F32 = jnp.float32
BF16 = jnp.bfloat16

D_MODEL = 1024
PAGE_SIZE = 128
POOL_GC = 128
POOL_W = 512
POOL_WINDOWS = (2, 4, 8, 16)
POOL_HIST = 15
MB_HEADS = 8
MB_HD = 64
MB_W = 512
MB_BLOCK = 256
MB_TOPK = 3
ROPE_THETA = 500000.0
ROPE_DIM = 16
GLA_HEADS = 4
GLA_DK = 64
GLA_DV = 128
GLA_KW = 256
GLA_VW = 512
GLA_RANK = 16
GLA_NORMALIZER = 16.0
GLA_CHUNK = 64
D_FF = 4096
EPS = 1e-6
W_MAIN = POOL_W + 3 * MB_W + 2 * GLA_KW + 2 * GLA_VW
NEG_INF = float("-inf")
LOG2E = 1.4426950408889634

VMEM_LIMIT_BYTES = 56 * 1024 * 1024
ROW_TILE = 512
FFN_ROW_TILE = 512
FFN_SUB_ROWS = 256
SAMPLE_BLOCKS_PER_STEP = 16
GLA_GROUP = 8


def _cparams(sem):
    return pltpu.CompilerParams(dimension_semantics=sem, vmem_limit_bytes=VMEM_LIMIT_BYTES)


def _const_spec(shape):
    nd = len(shape)
    return pl.BlockSpec(shape, lambda *_: (0,) * nd, pipeline_mode=pl.Buffered(1))


def _rms(x, g):
    ms = jnp.mean(x * x, axis=-1, keepdims=True)
    return x * lax.rsqrt(ms + EPS) * g


def _sigmoid(x):
    return 0.5 * jnp.tanh(0.5 * x) + 0.5


def _split3(x):
    hi = x.astype(BF16)
    r = x - hi.astype(F32)
    mid = r.astype(BF16)
    lo = (r - mid.astype(F32)).astype(BF16)
    return hi, mid, lo


def _in_proj_kernel(*refs, prompt, first_layer):
    x_ref, gpre_ref, wm_ref, wa_ref, wg_ref, wa2_ref, ba_ref, cos_ref, sa_ref, sb_ref = refs[:10]
    n_in = 10 + (0 if not prompt else 1 if first_layer else 3)
    wvt_ref = refs[10] if prompt else None
    outs = refs[n_in:]
    u_ref, q_ref = outs[:2]
    qg_ref, kg_ref, vg_ref, rg_ref, lg_ref, sg_ref = outs[-6:]
    h = _rms(x_ref[...], gpre_ref[...]).astype(BF16)

    def proj(w_ref, a, b):
        return jnp.dot(h, w_ref[:, a:b], preferred_element_type=F32)

    cos, sa, sb = cos_ref[...], sa_ref[...], sb_ref[...]

    def rope(col0):
        half = ROPE_DIM // 2
        x = proj(wm_ref, col0, col0 + MB_W)
        for j in range(MB_W // 128):
            xj = x[:, 128 * j:128 * (j + 1)]
            yield j, xj * cos + pltpu.roll(xj, 128 - half, 1) * sa + pltpu.roll(xj, half, 1) * sb

    for j in range(3):
        sg_ref[:, 1024 * j:1024 * (j + 1)] = _sigmoid(proj(wg_ref, 1024 * j, 1024 * (j + 1))).astype(BF16)
    r = proj(wm_ref, 3072, 3584)
    rg_ref[...] = r * _sigmoid(r)
    a = jnp.dot(h, wa_ref[...], preferred_element_type=F32)
    xg = jnp.dot(a.astype(BF16), wa2_ref[...], preferred_element_type=F32) + ba_ref[...]
    lg_ref[...] = (jnp.minimum(xg, 0.0) - jnp.log1p(jnp.exp(-jnp.abs(xg)))) * (1.0 / GLA_NORMALIZER)
    u_ref[...] = proj(wm_ref, 0, 512)
    for j, rj in rope(512):
        q_ref[:, 128 * j:128 * (j + 1)] = rj
    if prompt:
        kb_ref, km_ref, kt_ref, vt_ref = outs[2:6]
        n_blk = km_ref.shape[0]
        col_sums = [[] for _ in range(n_blk)]
        for j, rj in rope(1024):
            kb_ref[:, 128 * j:128 * (j + 1)] = rj.astype(BF16)
            kt_ref[0, 0, 128 * j:128 * (j + 1), :] = rj.T
            for i in range(n_blk):
                col_sums[i].append(jnp.sum(rj[i * MB_BLOCK:(i + 1) * MB_BLOCK], axis=0, keepdims=True))
        for i in range(n_blk):
            km_ref[i] = jnp.concatenate(col_sums[i], axis=1) * (1.0 / MB_BLOCK)
        vt_ref[0, 0] = _dot_nt(wvt_ref[...], h)
        for d in range(1, kt_ref.shape[0]):
            kt_ref[d] = jnp.zeros(kt_ref.shape[1:], F32)
            vt_ref[d] = jnp.zeros(vt_ref.shape[1:], F32)
    else:
        k_ref, v_ref = outs[2:4]
        for j, rj in rope(1024):
            k_ref[:, 128 * j:128 * (j + 1)] = rj
        v_ref[...] = proj(wm_ref, 1536, 2048)
    qg_ref[...] = proj(wm_ref, 2048, 2304)
    kg_ref[...] = proj(wm_ref, 2304, 2560)
    vg_ref[...] = proj(wm_ref, 2560, 3072)


def _in_proj(x, gpre, wm, wa, wg, wa2, ba, tabs, prompt_state=None):
    M = x.shape[0]
    tm = min(ROW_TILE, M)
    period = tabs[0].shape[0] // tm
    row = lambda w: pl.BlockSpec((tm, w), lambda i: (i, 0))
    tab = pl.BlockSpec((tm, 128), lambda i: (i % period, 0))
    f32 = lambda w: jax.ShapeDtypeStruct((M, w), F32)
    tail_w = (GLA_KW, GLA_KW, GLA_VW, GLA_VW, GLA_KW, 3 * D_MODEL)
    in_specs = [row(D_MODEL), _const_spec((1, D_MODEL)), _const_spec(wm.shape), _const_spec(wa.shape),
                _const_spec(wg.shape), _const_spec(wa2.shape), _const_spec((1, GLA_KW)), tab, tab, tab]
    args = [x, gpre, wm, wa, wg, wa2, ba, *tabs]
    aliases = {}
    if prompt_state is None:
        out_specs = [row(POOL_W), row(MB_W), row(MB_W), row(MB_W)]
        out_shape = [f32(POOL_W), f32(MB_W), f32(MB_W), f32(MB_W)]
        first = False
    else:
        seq_len, depth, layer, wvt, kt_all, vt_all = prompt_state
        assert tm % MB_BLOCK == 0 and seq_len % tm == 0
        tps = seq_len // tm
        first = layer == 0
        stacked = jax.ShapeDtypeStruct((depth, M // seq_len, MB_W, seq_len), F32)
        in_specs.append(_const_spec(wvt.shape))
        args.append(wvt)
        if first:
            st_spec = pl.BlockSpec((depth, 1, MB_W, tm), lambda i: (0, i // tps, 0, i % tps))
        else:
            st_spec = pl.BlockSpec((1, 1, MB_W, tm), lambda i: (layer, i // tps, 0, i % tps))
            in_specs += [pl.BlockSpec(memory_space=pl.ANY)] * 2
            args += [kt_all, vt_all]
            aliases = {11: 4, 12: 5}
        out_specs = [row(POOL_W), row(MB_W), row(MB_W),
                     pl.BlockSpec((tm // MB_BLOCK, 1, MB_W), lambda i: (i, 0, 0)),
                     st_spec, st_spec]
        out_shape = [f32(POOL_W), f32(MB_W), jax.ShapeDtypeStruct((M, MB_W), BF16),
                     jax.ShapeDtypeStruct((M // MB_BLOCK, 1, MB_W), F32), stacked, stacked]
    return pl.pallas_call(
        functools.partial(_in_proj_kernel, prompt=prompt_state is not None, first_layer=first),
        grid=(M // tm,),
        in_specs=in_specs,
        out_specs=out_specs + [row(w) for w in tail_w],
        out_shape=out_shape + [f32(w) for w in tail_w[:-1]] + [jax.ShapeDtypeStruct((M, tail_w[-1]), BF16)],
        input_output_aliases=aliases,
        compiler_params=_cparams(("arbitrary",)),
        name="in_proj",
    )(*args)


def _rope_tables(pos):
    half = ROPE_DIM // 2
    inv = ROPE_THETA ** (-jnp.arange(half, dtype=F32) / half)
    ang = pos.astype(F32)[:, None] * inv[None, :]
    cos, sin = jnp.cos(ang), jnp.sin(ang)
    n = pos.shape[0]
    zeros = jnp.zeros((n, MB_HD - ROPE_DIM), F32)
    zh = jnp.zeros((n, half), F32)
    c = jnp.concatenate([cos, cos, zeros + 1.0], axis=1)
    sa = jnp.concatenate([-sin, zh, zeros], axis=1)
    sb = jnp.concatenate([zh, sin, zeros], axis=1)
    two = lambda t: jnp.concatenate([t, t], axis=1)
    return two(c), two(sa), two(sb)


def _pool_kernel(u_ref, hist_ref, wgrp_ref, scale_ref, y_ref, ext_ref, *, pos0, T, R, nb):
    for s in range(nb):
        ext_ref[s, 0:16, :] = hist_ref[s]
        ext_ref[s, 16:16 + T, :] = u_ref[s]
    for c in range(T // R):
        r0 = 16 + c * R
        posp1 = lax.broadcasted_iota(jnp.int32, (R, 1), 0) + (pos0 + c * R + 1)
        for g, w in enumerate(POOL_WINDOWS):
            lanes = slice(g * POOL_GC, (g + 1) * POOL_GC)
            cnt = jnp.minimum(posp1, w).astype(F32)
            ds = []
            for s in range(nb):
                cur = ext_ref[s, r0:r0 + R, lanes]
                acc = cur
                for j in range(1, w):
                    acc = acc + ext_ref[s, r0 - j:r0 - j + R, lanes]
                ds.append(acc / cnt - cur)
            d = ds[0] if nb == 1 else jnp.concatenate(ds, axis=0)
            yg = jnp.dot(d.astype(BF16), wgrp_ref[g], preferred_element_type=F32) * scale_ref[:, lanes]
            for s in range(nb):
                y_ref[s, c * R:(c + 1) * R, lanes] = yg[s * R:(s + 1) * R].astype(BF16)


def _pool(u, hist16, wgrp, scale, pos0):
    Bn, T, _ = u.shape
    R = min(T, 256)
    nb = Bn if T < MB_BLOCK else 1
    seq = pl.BlockSpec((nb, T, POOL_W), lambda b: (b, 0, 0))
    return pl.pallas_call(
        functools.partial(_pool_kernel, pos0=pos0, T=T, R=R, nb=nb),
        grid=(Bn // nb,),
        in_specs=[seq, pl.BlockSpec((nb, 16, POOL_W), lambda b: (b, 0, 0)),
                  _const_spec(wgrp.shape), _const_spec((1, POOL_W))],
        out_specs=seq,
        out_shape=jax.ShapeDtypeStruct((Bn, T, POOL_W), BF16),
        scratch_shapes=[pltpu.VMEM((nb, 16 + T, POOL_W), F32)],
        compiler_params=_cparams(("arbitrary",)),
        name="pool",
    )(u, hist16, wgrp, scale)


def _top_blocks(gate, n_valid, axis):
    n_blocks = gate.shape[axis]
    iota = lax.broadcasted_iota(jnp.int32, gate.shape, axis)
    g = jnp.where(iota < n_valid, gate, NEG_INF)
    sel = jnp.zeros(gate.shape, F32)
    for _ in range(MB_TOPK):
        m = jnp.max(g, axis=axis, keepdims=True)
        idx = jnp.min(jnp.where(g == m, iota, n_blocks), axis=axis, keepdims=True)
        pick = (iota == idx) & (m > NEG_INF)
        sel = jnp.where(pick, 1.0, sel)
        g = jnp.where(pick, NEG_INF, g)
    return sel


def _col_max(s):
    r, c = s.shape
    return jnp.max(jnp.max(s.reshape(r // 8, 8, c), axis=0), axis=0, keepdims=True)


def _dot_nt(a, b, precision=None):
    return lax.dot_general(a, b, (((1,), (1,)), ((), ())), preferred_element_type=F32, precision=precision)


def _moba_prompt_kernel(q_ref, km_ref, k_ref, vt_ref, y_ref, s_ref, vtb_ref):
    B = MB_BLOCK
    n_blocks = km_ref.shape[1]
    qlane = lax.broadcasted_iota(jnp.int32, (B, 128), 1) // MB_HD
    kmlane = lax.broadcasted_iota(jnp.int32, (n_blocks, 128), 1) // MB_HD
    causal = lax.broadcasted_iota(jnp.int32, (B, B), 0) <= lax.broadcasted_iota(jnp.int32, (B, B), 1)
    drow = lax.broadcasted_iota(jnp.int32, (128, B), 0) // MB_HD
    vtb_ref[0:128, :] = vt_ref[0, 0].astype(BF16)
    vtb_ref[128:, :] = jnp.ones((vtb_ref.shape[0] - 128, vtb_ref.shape[1]), BF16)

    def scores_stage(c):
        q = q_ref[0, c * B:(c + 1) * B, :]
        qh = [jnp.where(qlane == hh, q, 0.0) for hh in range(2)]
        qs = [(x * (MB_HD ** -0.5 * LOG2E)).astype(BF16) for x in qh]
        raw = {(hh, j): _dot_nt(k_ref[0, j * B:(j + 1) * B, :], qs[hh])
               for hh in range(2) for j in range(c + 1)}
        maxima = []
        for hh in range(2):
            if c > 0:
                gate_t = _dot_nt(jnp.where(kmlane == hh, km_ref[0], 0.0), qh[hh], precision=lax.Precision.HIGHEST)
                sel = jnp.where(_top_blocks(gate_t, c, axis=0) > 0.0, 0.0, NEG_INF)
            m = None
            for j in range(c + 1):
                s = jnp.where(causal, raw[hh, j], NEG_INF) if j == c else raw[hh, j] + sel[j:j + 1, :]
                s_ref[2 * (c % 2) + hh, j] = s
                mj = _col_max(s)
                m = mj if m is None else jnp.maximum(m, mj)
            maxima.append(m)
        return maxima

    def values_stage(c, maxima):
        outs = []
        for hh in range(2):
            probs = [jnp.exp2(s_ref[2 * (c % 2) + hh, j] - maxima[hh]).astype(BF16)
                     for j in range(c + 1)]
            acc = jnp.dot(vtb_ref[:, :(c + 1) * B], jnp.concatenate(probs, axis=0), preferred_element_type=F32)
            outs.append(acc[:128] / acc[128:129])
        y_ref[0, c * B:(c + 1) * B, :] = jnp.where(drow == 0, outs[0], outs[1]).T.astype(BF16)

    pending = None
    for c in range(n_blocks):
        maxima = scores_stage(c)
        if pending is not None:
            values_stage(*pending)
        pending = (c, maxima)
    values_stage(*pending)


def _moba_prompt(q, km, kb, vt_all, layer):
    Bn, T, _ = q.shape
    n_blocks = T // MB_BLOCK
    seq = pl.BlockSpec((1, T, 128), lambda b, hp: (b, 0, hp))
    return pl.pallas_call(
        _moba_prompt_kernel,
        grid=(Bn, MB_W // 128),
        in_specs=[seq,
                  pl.BlockSpec((1, n_blocks, 128), lambda b, hp: (b, 0, hp)),
                  seq,
                  pl.BlockSpec((1, 1, 128, T), lambda b, hp: (layer, b, hp, 0))],
        out_specs=seq,
        out_shape=jax.ShapeDtypeStruct((Bn, T, MB_W), BF16),
        scratch_shapes=[pltpu.VMEM((4, n_blocks, MB_BLOCK, MB_BLOCK), F32), pltpu.VMEM((128 + 16, T), BF16)],
        compiler_params=_cparams(("arbitrary", "arbitrary")),
        name="moba_prompt",
    )(q, km, kb, vt_all)


def _moba_sample_kernel(pt_ref, q_ref, kn_ref, vn_ref, *refs, n_blocks, ts, bps):
    ppb = MB_BLOCK // PAGE_SIZE
    k_refs = refs[:bps * ppb]
    v_refs = refs[bps * ppb:2 * bps * ppb]
    y_ref, qbd_ref, oblk_ref, km_ref, m_ref, l_ref = refs[2 * bps * ppb:]
    step = pl.program_id(1)
    R = MB_HEADS * ts
    own = (lax.broadcasted_iota(jnp.int32, (R, MB_W), 0) // ts
           == lax.broadcasted_iota(jnp.int32, (R, MB_W), 1) // MB_HD)
    blane = lax.broadcasted_iota(jnp.int32, (R, n_blocks), 1)
    kml = lax.broadcasted_iota(jnp.int32, (MB_W, n_blocks), 1)

    @pl.when(step == 0)
    def _():
        for ref in (m_ref, l_ref, km_ref):
            ref[...] = jnp.zeros(ref.shape, F32)
        qbd_ref[...] = jnp.where(own, jnp.concatenate([q_ref[0]] * MB_HEADS, axis=0), 0.0)

    qbd = qbd_ref[...]
    qb = (qbd * (MB_HD ** -0.5)).astype(BF16)
    page = lambda ref: ref[0, 0].reshape(MB_W, PAGE_SIZE)
    m_all, l_all, km_all = m_ref[...], l_ref[...], km_ref[...]
    blocks = lambda refs, i: jnp.concatenate([page(refs[ppb * i + j]) for j in range(ppb)], axis=1)
    scores, probs = [], []
    for i in range(bps):
        kblk = blocks(k_refs, i)
        scores.append(jnp.dot(qb, kblk.astype(BF16), preferred_element_type=F32))
        kmean = jnp.sum(kblk, axis=1, keepdims=True) * (1.0 / MB_BLOCK)
        km_all = jnp.where(kml == step * bps + i, kmean, km_all)
    for i, s in enumerate(scores):
        n = step * bps + i
        m_n = jnp.max(s, axis=1, keepdims=True)
        p = jnp.exp(s - m_n)
        probs.append(p.astype(BF16))
        m_all = jnp.where(blane == n, m_n, m_all)
        l_all = jnp.where(blane == n, jnp.sum(p, axis=1, keepdims=True), l_all)
    for i, p in enumerate(probs):
        oblk_ref[step * bps + i] = _dot_nt(p, blocks(v_refs, i).astype(BF16))
    km_ref[...] = km_all
    m_ref[...] = m_all
    l_ref[...] = l_all

    @pl.when(step == n_blocks // bps - 1)
    def _():
        gate = jnp.dot(qbd, km_ref[...], preferred_element_type=F32, precision=lax.Precision.HIGHEST)
        sel = _top_blocks(gate, n_blocks, axis=1) > 0.0
        s_own = _dot_nt(qb, kn_ref[0].astype(BF16))
        tq = lax.broadcasted_iota(jnp.int32, (R, ts), 0) % ts
        tk = lax.broadcasted_iota(jnp.int32, (R, ts), 1)
        s_own = jnp.where(tk <= tq, s_own, NEG_INF)
        mb = jnp.where(sel, m_ref[...], NEG_INF)
        m_all = jnp.maximum(jnp.max(s_own, axis=1, keepdims=True), jnp.max(mb, axis=1, keepdims=True))
        w = jnp.exp(mb - m_all)
        p_own = jnp.exp(s_own - m_all)
        l = jnp.sum(p_own, axis=1, keepdims=True) + jnp.sum(w * l_ref[...], axis=1, keepdims=True)
        o = jnp.dot(p_own.astype(BF16), vn_ref[0].astype(BF16), preferred_element_type=F32)
        for nb in range(n_blocks):
            o = o + w[:, nb:nb + 1] * oblk_ref[nb]
        o = jnp.where(own, o / l, 0.0)
        y = o[0:ts]
        for h in range(1, MB_HEADS):
            y = y + o[h * ts:(h + 1) * ts]
        y_ref[0] = y.astype(BF16)


def _moba_sample(q, kn, vn, cache_kt, cache_vt, page_table, layer):
    DB, ts, _ = q.shape
    n_pages = page_table.shape[1]
    ppb = MB_BLOCK // PAGE_SIZE
    n_blocks = n_pages // ppb
    R = MB_HEADS * ts
    tok = pl.BlockSpec((1, ts, MB_W), lambda b, n, pt: (b, 0, 0))
    bps = math.gcd(n_blocks, SAMPLE_BLOCKS_PER_STEP)
    pps = bps * ppb

    def page(i):
        return pl.BlockSpec((1, 1, MB_HEADS, MB_HD, PAGE_SIZE),
                            lambda b, n, pt: (layer, pt[b * n_pages + n * pps + i], 0, 0, 0))

    pages = [page(i) for i in range(pps)]
    grid_spec = pltpu.PrefetchScalarGridSpec(
        num_scalar_prefetch=1,
        grid=(DB, n_blocks // bps),
        in_specs=[tok, tok, tok] + pages + pages,
        out_specs=tok,
        scratch_shapes=[pltpu.VMEM((R, MB_W), F32),
                        pltpu.VMEM((n_blocks, R, MB_W), F32),
                        pltpu.VMEM((MB_W, n_blocks), F32),
                        pltpu.VMEM((R, n_blocks), F32),
                        pltpu.VMEM((R, n_blocks), F32)],
    )
    return pl.pallas_call(
        functools.partial(_moba_sample_kernel, n_blocks=n_blocks, ts=ts, bps=bps),
        grid_spec=grid_spec,
        out_shape=jax.ShapeDtypeStruct((DB, ts, MB_W), BF16),
        compiler_params=_cparams(("arbitrary", "arbitrary")),
        name="moba_sample",
    )(page_table.reshape(-1), q, kn, vn, *([cache_kt] * pps), *([cache_vt] * pps))


def _gla_kernel(qg_ref, kg_ref, vg_ref, rg_ref, lg_ref, s0_ref, gn_ref, y_ref, sfin_ref, st_ref, *,
                n_chunks, valid):
    t = pl.program_id(1)
    C = GLA_CHUNK

    @pl.when(t == 0)
    def _():
        for h in range(GLA_HEADS):
            st_ref[:, h * GLA_DK:(h + 1) * GLA_DK] = s0_ref[0, h].T

    ri = lax.broadcasted_iota(jnp.int32, (C, C), 0)
    ci = lax.broadcasted_iota(jnp.int32, (C, C), 1)
    tri = (ci <= ri).astype(BF16)
    m1 = (ri >= 32) & (ci < 32)
    m2 = (ri // 32 == ci // 32) & (ri % 32 >= 16) & (ci % 32 < 16)
    m3 = (ri // 16 == ci // 16) & (ci <= ri)
    grp = lax.broadcasted_iota(jnp.int32, (C, GLA_KW), 0)

    heads = [(slice(h * GLA_DK, (h + 1) * GLA_DK), slice(h * GLA_DV, (h + 1) * GLA_DV)) for h in range(GLA_HEADS)]
    sts = [st_ref[:, kl] for kl, _ in heads]
    for g0 in range(0, n_chunks, GLA_GROUP):
        cs = range(g0, min(g0 + GLA_GROUP, n_chunks))
        rows = {c: slice(c * C, c * C + valid) for c in cs}

        def chunk(ref, c, lanes=slice(None)):
            x = ref[0, rows[c], lanes]
            return x if valid == C else jnp.concatenate([x, jnp.zeros((C - valid, x.shape[1]), x.dtype)], axis=0)

        bcum = {}
        for c in cs:
            hi, mid, lo = _split3(chunk(lg_ref, c))
            bcum[c] = (jnp.dot(tri, hi, preferred_element_type=F32) + jnp.dot(tri, mid, preferred_element_type=F32)
                       + jnp.dot(tri, lo, preferred_element_type=F32))
        scaled = {}
        for c in cs:
            b = bcum[c]
            q = chunk(qg_ref, c) * (GLA_DK ** -0.5)
            k = chunk(kg_ref, c)
            brow = lambda i: jnp.broadcast_to(b[i:i + 1, :], (C, GLA_KW))
            ref1 = brow(31)
            ref2 = jnp.where(grp < 32, brow(15), brow(47))
            ref3 = jnp.where(grp < 16, brow(7),
                             jnp.where(grp < 32, brow(23), jnp.where(grp < 48, brow(39), brow(55))))
            b_last = b[C - 1:C, :]
            scaled[c] = dict(
                qs=(q * jnp.exp(b)).astype(BF16), ks=(k * jnp.exp(b_last - b)).astype(BF16),
                q1=(q * jnp.exp(jnp.minimum(b - ref1, 0.0))).astype(BF16),
                k1=(k * jnp.exp(jnp.minimum(ref1 - b, 0.0))).astype(BF16),
                q2=(q * jnp.exp(jnp.minimum(b - ref2, 0.0))).astype(BF16),
                k2=(k * jnp.exp(jnp.minimum(ref2 - b, 0.0))).astype(BF16),
                q3=(q * jnp.exp(b - ref3)).astype(BF16), k3=(k * jnp.exp(ref3 - b)).astype(BF16),
                decay=jnp.exp(b_last))
        levels = {}
        for c in cs:
            x = scaled[c]
            for h, (kl, _) in enumerate(heads):
                levels[c, h] = (_dot_nt(x["q1"][:, kl], x["k1"][:, kl]), _dot_nt(x["q2"][:, kl], x["k2"][:, kl]),
                                _dot_nt(x["q3"][:, kl], x["k3"][:, kl]))
        intra, update = {}, {}
        for c in cs:
            for h, (kl, vl) in enumerate(heads):
                l1, l2, l3 = levels[c, h]
                a = jnp.where(m1, l1, 0.0) + jnp.where(m2, l2, 0.0) + jnp.where(m3, l3, 0.0)
                vf = chunk(vg_ref, c, vl)
                intra[c, h] = jnp.dot(a.astype(BF16), vf.astype(BF16), preferred_element_type=F32)
                update[c, h] = jnp.dot(vf.T.astype(BF16), scaled[c]["ks"][:, kl], preferred_element_type=F32)
        for c in cs:
            for h, (kl, vl) in enumerate(heads):
                o = _dot_nt(scaled[c]["qs"][:, kl], sts[h].astype(BF16)) + intra[c, h]
                sts[h] = sts[h] * scaled[c]["decay"][:, kl] + update[c, h]
                y_ref[0, rows[c], vl] = (_rms(o[:valid], gn_ref[:, vl]) * rg_ref[0, rows[c], vl]).astype(BF16)
    for (kl, _), st in zip(heads, sts):
        st_ref[:, kl] = st

    @pl.when(t == pl.num_programs(1) - 1)
    def _():
        for h in range(GLA_HEADS):
            sfin_ref[0, h] = st_ref[:, h * GLA_DK:(h + 1) * GLA_DK].T


def _gla(qg, kg, vg, rg, lg, s0, gn):
    Bn, T, _ = qg.shape
    assert T < GLA_CHUNK or T % GLA_CHUNK == 0
    TT = min(T, 512)
    kspec = pl.BlockSpec((1, TT, GLA_KW), lambda b, t: (b, t, 0))
    vspec = pl.BlockSpec((1, TT, GLA_VW), lambda b, t: (b, t, 0))
    sspec = pl.BlockSpec((1, GLA_HEADS, GLA_DK, GLA_DV), lambda b, t: (b, 0, 0, 0))
    return pl.pallas_call(
        functools.partial(_gla_kernel, n_chunks=max(1, TT // GLA_CHUNK), valid=min(TT, GLA_CHUNK)),
        grid=(Bn, T // TT),
        in_specs=[kspec, kspec, vspec, vspec, kspec, sspec, _const_spec((1, GLA_VW))],
        out_specs=[vspec, sspec],
        out_shape=[jax.ShapeDtypeStruct((Bn, T, GLA_VW), BF16),
                   jax.ShapeDtypeStruct((Bn, GLA_HEADS, GLA_DK, GLA_DV), F32)],
        scratch_shapes=[pltpu.VMEM((GLA_DV, GLA_KW), F32)],
        compiler_params=_cparams(("arbitrary", "arbitrary")),
        name="gla",
    )(qg, kg, vg, rg, lg, s0, gn)


def _merge_ffn_kernel(x_ref, yp_ref, ym_ref, yg_ref, sg_ref, wp_ref, wmb_ref, wgl_ref, wo_ref,
                      gpost_ref, gfpre_ref, gfpost_ref, w1_ref, w2_ref, o_ref):
    tm = x_ref.shape[0]
    subs = [slice(r, r + FFN_SUB_ROWS) for r in range(0, tm, FFN_SUB_ROWS)] if tm > FFN_SUB_ROWS else [slice(0, tm)]

    def dot(a, w):
        return jnp.dot(a.astype(BF16), w, preferred_element_type=F32)

    branches = [[dot(y_ref[r, :], w_ref[...]) for y_ref, w_ref in
                 ((yp_ref, wp_ref), (ym_ref, wmb_ref), (yg_ref, wgl_ref))] for r in subs]
    merged = [sum(sg_ref[r, 1024 * j:1024 * (j + 1)] * b for j, b in enumerate(bs))
              for r, bs in zip(subs, branches)]
    z = [dot(m, wo_ref[...]) for m in merged]
    x1 = [x_ref[r, :] + _rms(zi, gpost_ref[...]) for r, zi in zip(subs, z)]
    hb = [_rms(xi, gfpre_ref[...]).astype(BF16) for xi in x1]
    FC = 1024
    f = [jnp.zeros(xi.shape, F32) for xi in x1]

    def second(i, c, a):
        a = jnp.maximum(a, 0.0)
        f[i] = f[i] + dot(a * a, w2_ref[c * FC:(c + 1) * FC, :])

    pending = None
    for c in range(D_FF // FC):
        for i in range(len(subs)):
            a = jnp.dot(hb[i], w1_ref[:, c * FC:(c + 1) * FC], preferred_element_type=F32)
            if pending is not None:
                second(*pending)
            pending = (i, c, a)
    second(*pending)
    for i, r in enumerate(subs):
        o_ref[r, :] = x1[i] + _rms(f[i], gfpost_ref[...])


def _merge_ffn(x, yp, ym, yg, sg, wp, wmb, wgl, wo, gpost, gfpre, gfpost, w1, w2):
    M = x.shape[0]
    tm = min(FFN_ROW_TILE, M)
    row = lambda w: pl.BlockSpec((tm, w), lambda i: (i, 0))
    vec = _const_spec((1, D_MODEL))
    return pl.pallas_call(
        _merge_ffn_kernel,
        grid=(M // tm,),
        in_specs=[row(D_MODEL), row(POOL_W), row(MB_W), row(GLA_VW), row(3 * D_MODEL),
                  _const_spec(wp.shape), _const_spec(wmb.shape), _const_spec(wgl.shape), _const_spec(wo.shape),
                  vec, vec, vec, _const_spec(w1.shape), _const_spec(w2.shape)],
        out_specs=row(D_MODEL),
        out_shape=jax.ShapeDtypeStruct((M, D_MODEL), F32),
        compiler_params=_cparams(("arbitrary",)),
        name="merge_ffn",
    )(x, yp, ym, yg, sg, wp, wmb, wgl, wo, gpost, gfpre, gfpost, w1, w2)


def _layer(x, Bn, T, tabs, hist16, pos0, s0, sample_attend, lw, prompt_state=None):
    (wm, wa, wg, wa2, ba, gn, wgrp, pscale, wp, wmb, wgl, wo, gpre, gpost, gfpre, gfpost, w1, w2) = lw
    seq = lambda a: a.reshape(Bn, T, a.shape[-1])
    if sample_attend is None:
        depth, layer = prompt_state[:2]
        u, q, kb, km, k, v, qg, kg, vg, rg, lg, sg = _in_proj(x, gpre, wm, wa, wg, wa2, ba, tabs,
                                                             prompt_state=(T,) + tuple(prompt_state))
        ym = _moba_prompt(seq(q), km.reshape(Bn, T // MB_BLOCK, MB_W), seq(kb), v, layer)
    else:
        u, q, k, v, qg, kg, vg, rg, lg, sg = _in_proj(x, gpre, wm, wa, wg, wa2, ba, tabs)
        ym = sample_attend(seq(q), seq(k), seq(v))
        k, v = (a.reshape(Bn, T, MB_HEADS, MB_HD) for a in (k, v))
    u3 = seq(u)
    yp = _pool(u3, hist16, wgrp, pscale, pos0)
    yg, s_new = _gla(seq(qg), seq(kg), seq(vg), seq(rg), seq(lg), s0, gn)
    yg = yg.reshape(Bn * T, GLA_VW)
    x = _merge_ffn(x, yp.reshape(Bn * T, POOL_W), ym.reshape(Bn * T, MB_W), yg, sg,
                   wp, wmb, wgl, wo, gpost, gfpre, gfpost, w1, w2)
    new_hist = jnp.concatenate([hist16[:, 1:], u3], axis=1)[:, -POOL_HIST:]
    return x, k, v, new_hist, s_new


def kernel(x_prompt, x_sample, cache_k, cache_v, state_pool, state_gla, page_table, w_in, w_gla_a2, b_gla_a,
           gla_norm, w_pool_grp, pool_scale, w_pool_out, w_mb_out, w_gla_out, w_o, norm_mix_pre,
           norm_mix_post, norm_ffn_pre, norm_ffn_post, w_ff1, w_ff2):
    B, T, D = x_prompt.shape
    DB, TS, _ = x_sample.shape
    depth = w_in.shape[0]
    n_pages = page_table.shape[1]
    past_len = n_pages * PAGE_SIZE
    assert D == D_MODEL and T % MB_BLOCK == 0 and past_len % MB_BLOCK == 0
    assert past_len // MB_BLOCK >= MB_TOPK and (B * T) % ROW_TILE == 0 and MB_BLOCK == 2 * PAGE_SIZE

    tabs_p = _rope_tables(jnp.arange(T))
    reps = max(1, min(ROW_TILE, DB * TS) // TS)
    tabs_s = tuple(jnp.tile(t, (reps, 1)) for t in _rope_tables(past_len + jnp.arange(TS)))
    ckt = jnp.transpose(cache_k, (0, 1, 3, 4, 2))
    cvt = jnp.transpose(cache_v, (0, 1, 3, 4, 2))
    zrow = lambda n: jnp.zeros((n, 1, POOL_W), F32)
    hist_p = jnp.zeros((B, 16, POOL_W), F32)
    s0_p = jnp.zeros((B, GLA_HEADS, GLA_DK, GLA_DV), F32)

    hp = x_prompt.reshape(B * T, D)
    hs = x_sample.reshape(DB * TS, D)
    outs_p, outs_s = [], []
    kt_all = vt_all = None
    for l in range(depth):
        bf = lambda a: a.astype(BF16)
        vec = lambda a: a.reshape(1, -1)
        lw = (bf(w_in[l][:, :W_MAIN]), bf(w_in[l][:, W_MAIN:W_MAIN + GLA_RANK]), bf(w_in[l][:, W_MAIN + GLA_RANK:]),
              bf(w_gla_a2[l]), vec(b_gla_a[l]), vec(gla_norm[l]), bf(w_pool_grp[l]), vec(pool_scale[l]),
              bf(w_pool_out[l]), bf(w_mb_out[l]), bf(w_gla_out[l]), bf(w_o[l]),
              vec(norm_mix_pre[l]), vec(norm_mix_post[l]), vec(norm_ffn_pre[l]), vec(norm_ffn_post[l]),
              bf(w_ff1[l]), bf(w_ff2[l]))
        wvt = bf(w_in[l][:, POOL_W + 2 * MB_W:POOL_W + 3 * MB_W].T)
        hp, kt_all, vt_all, *op = _layer(hp, B, T, tabs_p, hist_p, 0, s0_p, None, lw,
                                         prompt_state=(depth, l, wvt, kt_all, vt_all))
        hist_s = jnp.concatenate([zrow(DB), state_pool[l]], axis=1)
        attend_s = functools.partial(_moba_sample, cache_kt=ckt, cache_vt=cvt, page_table=page_table, layer=l)
        hs, *os_ = _layer(hs, DB, TS, tabs_s, hist_s, past_len, state_gla[l], attend_s, lw)
        outs_p.append(op)
        outs_s.append(os_)
    stack = lambda outs, i: jnp.stack([o[i] for o in outs])
    untr = lambda a: jnp.transpose(a.reshape(depth, B, MB_HEADS, MB_HD, T), (0, 1, 4, 2, 3))
    return (hp.reshape(B, T, D), hs.reshape(DB, TS, D),
            untr(kt_all), untr(vt_all), stack(outs_p, 0), stack(outs_p, 1),
            stack(outs_s, 0), stack(outs_s, 1), stack(outs_s, 2), stack(outs_s, 3))
```

```python
import functools
import math

import jax
import jax.numpy as jnp
from jax import lax
from jax.experimental import pallas as pl
from jax.experimental.pallas import tpu as pltpu

F32 = jnp.float32
BF16 = jnp.bfloat16

D_MODEL = 1024
PAGE_SIZE = 128
POOL_GC = 128
POOL_W = 512
POOL_WINDOWS = (2, 4, 8, 16)
POOL_HIST = 15
MB_HEADS = 8
MB_HD = 64
MB_W = 512
MB_BLOCK = 256
MB_TOPK = 3
ROPE_THETA = 500000.0
ROPE_DIM = 16
GLA_HEADS = 4
GLA_DK = 64
GLA_DV = 128
GLA_KW = 256
GLA_VW = 512
GLA_RANK = 16
GLA_NORMALIZER = 16.0
GLA_CHUNK = 64
D_FF = 4096
EPS = 1e-6
W_MAIN = POOL_W + 3 * MB_W + 2 * GLA_KW + 2 * GLA_VW
NEG_INF = float("-inf")
LOG2E = 1.4426950408889634

VMEM_LIMIT_BYTES = 56 * 1024 * 1024
ROW_TILE = 512
FFN_ROW_TILE = 512
FFN_SUB_ROWS = 256
SAMPLE_BLOCKS_PER_STEP = 16
GLA_GROUP = 8


def _cparams(sem):
    return pltpu.CompilerParams(dimension_semantics=sem, vmem_limit_bytes=VMEM_LIMIT_BYTES)


def _const_spec(shape):
    nd = len(shape)
    return pl.BlockSpec(shape, lambda *_: (0,) * nd, pipeline_mode=pl.Buffered(1))


def _layer_spec(stacked, layer, cols=None, col_block=0):
    _, k, n = stacked.shape
    return pl.BlockSpec((None, k, n if cols is None else cols), lambda *_: (layer, 0, col_block),
                        pipeline_mode=pl.Buffered(1))


def _rms(x, g):
    ms = jnp.mean(x * x, axis=-1, keepdims=True)
    return x * lax.rsqrt(ms + EPS) * g


def _sigmoid(x):
    return 0.5 * jnp.tanh(0.5 * x) + 0.5


def _split3(x):
    hi = x.astype(BF16)
    r = x - hi.astype(F32)
    mid = r.astype(BF16)
    lo = (r - mid.astype(F32)).astype(BF16)
    return hi, mid, lo


def _in_proj_kernel(*refs, prompt, first_layer):
    x_ref, gpre_ref, wm_ref, wa_ref, wg_ref, wa2_ref, ba_ref, cos_ref, sa_ref, sb_ref = refs[:10]
    n_in = 10 + (0 if not prompt else 1 if first_layer else 3)
    wvt_ref = refs[10] if prompt else None
    outs = refs[n_in:]
    u_ref, q_ref = outs[:2]
    qg_ref, kg_ref, vg_ref, rg_ref, lg_ref, sg_ref = outs[-6:]
    h = _rms(x_ref[...], gpre_ref[...]).astype(BF16)

    def proj(w_ref, a, b):
        return jnp.dot(h, w_ref[:, a:b], preferred_element_type=F32)

    cos, sa, sb = cos_ref[...], sa_ref[...], sb_ref[...]

    def rope(col0):
        half = ROPE_DIM // 2
        x = proj(wm_ref, col0, col0 + MB_W)
        for j in range(MB_W // 128):
            xj = x[:, 128 * j:128 * (j + 1)]
            yield j, xj * cos + pltpu.roll(xj, 128 - half, 1) * sa + pltpu.roll(xj, half, 1) * sb

    for j in range(3):
        sg_ref[:, 1024 * j:1024 * (j + 1)] = _sigmoid(proj(wg_ref, 1024 * j, 1024 * (j + 1))).astype(BF16)
    r = proj(wm_ref, 3072, 3584)
    rg_ref[...] = r * _sigmoid(r)
    a = jnp.dot(h, wa_ref[:, :GLA_RANK], preferred_element_type=F32)
    xg = jnp.dot(a.astype(BF16), wa2_ref[...], preferred_element_type=F32) + ba_ref[...]
    lg_ref[...] = (jnp.minimum(xg, 0.0) - jnp.log1p(jnp.exp(-jnp.abs(xg)))) * (1.0 / GLA_NORMALIZER)
    u_ref[...] = proj(wm_ref, 0, 512)
    for j, rj in rope(512):
        q_ref[:, 128 * j:128 * (j + 1)] = rj
    if prompt:
        kb_ref, km_ref, kt_ref, vt_ref = outs[2:6]
        n_blk = km_ref.shape[0]
        col_sums = [[] for _ in range(n_blk)]
        for j, rj in rope(1024):
            kb_ref[:, 128 * j:128 * (j + 1)] = rj.astype(BF16)
            kt_ref[0, 0, 128 * j:128 * (j + 1), :] = rj.T
            for i in range(n_blk):
                col_sums[i].append(jnp.sum(rj[i * MB_BLOCK:(i + 1) * MB_BLOCK], axis=0, keepdims=True))
        for i in range(n_blk):
            km_ref[i] = jnp.concatenate(col_sums[i], axis=1) * (1.0 / MB_BLOCK)
        vt_ref[0, 0] = _dot_nt(wvt_ref[...], h)
        for d in range(1, kt_ref.shape[0]):
            kt_ref[d] = jnp.zeros(kt_ref.shape[1:], F32)
            vt_ref[d] = jnp.zeros(vt_ref.shape[1:], F32)
    else:
        k_ref, v_ref = outs[2:4]
        for j, rj in rope(1024):
            k_ref[:, 128 * j:128 * (j + 1)] = rj
        v_ref[...] = proj(wm_ref, 1536, 2048)
    qg_ref[...] = proj(wm_ref, 2048, 2304)
    kg_ref[...] = proj(wm_ref, 2304, 2560)
    vg_ref[...] = proj(wm_ref, 2560, 3072)


def _in_proj(x, gpre, w_in, layer, wg, wa2, ba, tabs, prompt_state=None):
    M = x.shape[0]
    tm = min(ROW_TILE, M)
    period = tabs[0].shape[0] // tm
    row = lambda w: pl.BlockSpec((tm, w), lambda i: (i, 0))
    tab = pl.BlockSpec((tm, 128), lambda i: (i % period, 0))
    f32 = lambda w: jax.ShapeDtypeStruct((M, w), F32)
    tail_w = (GLA_KW, GLA_KW, GLA_VW, GLA_VW, GLA_KW, 3 * D_MODEL)
    assert W_MAIN % 128 == 0
    in_specs = [row(D_MODEL), _const_spec((1, D_MODEL)), _layer_spec(w_in, layer, cols=W_MAIN),
                _layer_spec(w_in, layer, cols=128, col_block=W_MAIN // 128),
                _const_spec(wg.shape), _const_spec(wa2.shape), _const_spec((1, GLA_KW)), tab, tab, tab]
    args = [x, gpre, w_in, w_in, wg, wa2, ba, *tabs]
    aliases = {}
    if prompt_state is None:
        out_specs = [row(POOL_W), row(MB_W), row(MB_W), row(MB_W)]
        out_shape = [f32(POOL_W), f32(MB_W), f32(MB_W), f32(MB_W)]
        first = False
    else:
        seq_len, depth, layer, wvt, kt_all, vt_all = prompt_state
        assert tm % MB_BLOCK == 0 and seq_len % tm == 0
        tps = seq_len // tm
        first = layer == 0
        stacked = jax.ShapeDtypeStruct((depth, M // seq_len, MB_W, seq_len), F32)
        in_specs.append(_const_spec(wvt.shape))
        args.append(wvt)
        if first:
            st_spec = pl.BlockSpec((depth, 1, MB_W, tm), lambda i: (0, i // tps, 0, i % tps))
        else:
            st_spec = pl.BlockSpec((1, 1, MB_W, tm), lambda i: (layer, i // tps, 0, i % tps))
            in_specs += [pl.BlockSpec(memory_space=pl.ANY)] * 2
            args += [kt_all, vt_all]
            aliases = {11: 4, 12: 5}
        out_specs = [row(POOL_W), row(MB_W), row(MB_W),
                     pl.BlockSpec((tm // MB_BLOCK, 1, MB_W), lambda i: (i, 0, 0)),
                     st_spec, st_spec]
        out_shape = [f32(POOL_W), f32(MB_W), jax.ShapeDtypeStruct((M, MB_W), BF16),
                     jax.ShapeDtypeStruct((M // MB_BLOCK, 1, MB_W), F32), stacked, stacked]
    return pl.pallas_call(
        functools.partial(_in_proj_kernel, prompt=prompt_state is not None, first_layer=first),
        grid=(M // tm,),
        in_specs=in_specs,
        out_specs=out_specs + [row(w) for w in tail_w],
        out_shape=out_shape + [f32(w) for w in tail_w[:-1]] + [jax.ShapeDtypeStruct((M, tail_w[-1]), BF16)],
        input_output_aliases=aliases,
        compiler_params=_cparams(("arbitrary",)),
        name="in_proj",
    )(*args)


def _rope_tables(pos):
    half = ROPE_DIM // 2
    inv = ROPE_THETA ** (-jnp.arange(half, dtype=F32) / half)
    ang = pos.astype(F32)[:, None] * inv[None, :]
    cos, sin = jnp.cos(ang), jnp.sin(ang)
    n = pos.shape[0]
    zeros = jnp.zeros((n, MB_HD - ROPE_DIM), F32)
    zh = jnp.zeros((n, half), F32)
    c = jnp.concatenate([cos, cos, zeros + 1.0], axis=1)
    sa = jnp.concatenate([-sin, zh, zeros], axis=1)
    sb = jnp.concatenate([zh, sin, zeros], axis=1)
    two = lambda t: jnp.concatenate([t, t], axis=1)
    return two(c), two(sa), two(sb)


def _pool_kernel(u_ref, hist_ref, wgrp_ref, scale_ref, y_ref, ext_ref, *, pos0, T, R, nb):
    for s in range(nb):
        ext_ref[s, 0:16, :] = hist_ref[s]
        ext_ref[s, 16:16 + T, :] = u_ref[s]
    for c in range(T // R):
        r0 = 16 + c * R
        posp1 = lax.broadcasted_iota(jnp.int32, (R, 1), 0) + (pos0 + c * R + 1)
        for g, w in enumerate(POOL_WINDOWS):
            lanes = slice(g * POOL_GC, (g + 1) * POOL_GC)
            cnt = jnp.minimum(posp1, w).astype(F32)
            ds = []
            for s in range(nb):
                cur = ext_ref[s, r0:r0 + R, lanes]
                acc = cur
                for j in range(1, w):
                    acc = acc + ext_ref[s, r0 - j:r0 - j + R, lanes]
                ds.append(acc / cnt - cur)
            d = ds[0] if nb == 1 else jnp.concatenate(ds, axis=0)
            yg = jnp.dot(d.astype(BF16), wgrp_ref[g], preferred_element_type=F32) * scale_ref[:, lanes]
            for s in range(nb):
                y_ref[s, c * R:(c + 1) * R, lanes] = yg[s * R:(s + 1) * R].astype(BF16)


def _pool(u, hist16, wgrp, scale, pos0):
    Bn, T, _ = u.shape
    R = min(T, 256)
    nb = Bn if T < MB_BLOCK else 1
    seq = pl.BlockSpec((nb, T, POOL_W), lambda b: (b, 0, 0))
    return pl.pallas_call(
        functools.partial(_pool_kernel, pos0=pos0, T=T, R=R, nb=nb),
        grid=(Bn // nb,),
        in_specs=[seq, pl.BlockSpec((nb, 16, POOL_W), lambda b: (b, 0, 0)),
                  _const_spec(wgrp.shape), _const_spec((1, POOL_W))],
        out_specs=seq,
        out_shape=jax.ShapeDtypeStruct((Bn, T, POOL_W), BF16),
        scratch_shapes=[pltpu.VMEM((nb, 16 + T, POOL_W), F32)],
        compiler_params=_cparams(("arbitrary",)),
        name="pool",
    )(u, hist16, wgrp, scale)


def _top_blocks(gate, n_valid, axis):
    n_blocks = gate.shape[axis]
    iota = lax.broadcasted_iota(jnp.int32, gate.shape, axis)
    g = jnp.where(iota < n_valid, gate, NEG_INF)
    sel = jnp.zeros(gate.shape, F32)
    for _ in range(MB_TOPK):
        m = jnp.max(g, axis=axis, keepdims=True)
        idx = jnp.min(jnp.where(g == m, iota, n_blocks), axis=axis, keepdims=True)
        pick = (iota == idx) & (m > NEG_INF)
        sel = jnp.where(pick, 1.0, sel)
        g = jnp.where(pick, NEG_INF, g)
    return sel


def _col_max(s):
    r, c = s.shape
    return jnp.max(jnp.max(s.reshape(r // 8, 8, c), axis=0), axis=0, keepdims=True)


def _dot_nt(a, b, precision=None):
    return lax.dot_general(a, b, (((1,), (1,)), ((), ())), preferred_element_type=F32, precision=precision)


def _moba_prompt_kernel(q_ref, km_ref, k_ref, vt_ref, y_ref, s_ref, vtb_ref):
    B = MB_BLOCK
    n_blocks = km_ref.shape[1]
    qlane = lax.broadcasted_iota(jnp.int32, (B, 128), 1) // MB_HD
    kmlane = lax.broadcasted_iota(jnp.int32, (n_blocks, 128), 1) // MB_HD
    causal = lax.broadcasted_iota(jnp.int32, (B, B), 0) <= lax.broadcasted_iota(jnp.int32, (B, B), 1)
    drow = lax.broadcasted_iota(jnp.int32, (128, B), 0) // MB_HD
    vtb_ref[0:128, :] = vt_ref[0, 0].astype(BF16)
    vtb_ref[128:, :] = jnp.ones((vtb_ref.shape[0] - 128, vtb_ref.shape[1]), BF16)

    def scores_stage(c):
        q = q_ref[0, c * B:(c + 1) * B, :]
        qh = [jnp.where(qlane == hh, q, 0.0) for hh in range(2)]
        qs = [(x * (MB_HD ** -0.5 * LOG2E)).astype(BF16) for x in qh]
        raw = {(hh, j): _dot_nt(k_ref[0, j * B:(j + 1) * B, :], qs[hh])
               for hh in range(2) for j in range(c + 1)}
        maxima = []
        for hh in range(2):
            if c > 0:
                gate_t = _dot_nt(jnp.where(kmlane == hh, km_ref[0], 0.0), qh[hh], precision=lax.Precision.HIGHEST)
                sel = jnp.where(_top_blocks(gate_t, c, axis=0) > 0.0, 0.0, NEG_INF)
            m = None
            for j in range(c + 1):
                s = jnp.where(causal, raw[hh, j], NEG_INF) if j == c else raw[hh, j] + sel[j:j + 1, :]
                s_ref[2 * (c % 2) + hh, j] = s
                mj = _col_max(s)
                m = mj if m is None else jnp.maximum(m, mj)
            maxima.append(m)
        return maxima

    def values_stage(c, maxima):
        outs = []
        for hh in range(2):
            probs = [jnp.exp2(s_ref[2 * (c % 2) + hh, j] - maxima[hh]).astype(BF16)
                     for j in range(c + 1)]
            acc = jnp.dot(vtb_ref[:, :(c + 1) * B], jnp.concatenate(probs, axis=0), preferred_element_type=F32)
            outs.append(acc[:128] / acc[128:129])
        y_ref[0, c * B:(c + 1) * B, :] = jnp.where(drow == 0, outs[0], outs[1]).T.astype(BF16)

    pending = None
    for c in range(n_blocks):
        maxima = scores_stage(c)
        if pending is not None:
            values_stage(*pending)
        pending = (c, maxima)
    values_stage(*pending)


def _moba_prompt(q, km, kb, vt_all, layer):
    Bn, T, _ = q.shape
    n_blocks = T // MB_BLOCK
    seq = pl.BlockSpec((1, T, 128), lambda b, hp: (b, 0, hp))
    return pl.pallas_call(
        _moba_prompt_kernel,
        grid=(Bn, MB_W // 128),
        in_specs=[seq,
                  pl.BlockSpec((1, n_blocks, 128), lambda b, hp: (b, 0, hp)),
                  seq,
                  pl.BlockSpec((1, 1, 128, T), lambda b, hp: (layer, b, hp, 0))],
        out_specs=seq,
        out_shape=jax.ShapeDtypeStruct((Bn, T, MB_W), BF16),
        scratch_shapes=[pltpu.VMEM((4, n_blocks, MB_BLOCK, MB_BLOCK), F32), pltpu.VMEM((128 + 16, T), BF16)],
        compiler_params=_cparams(("arbitrary", "arbitrary")),
        name="moba_prompt",
    )(q, km, kb, vt_all)


def _moba_sample_kernel(pt_ref, q_ref, kn_ref, vn_ref, *refs, n_blocks, ts, bps):
    ppb = MB_BLOCK // PAGE_SIZE
    k_refs = refs[:bps * ppb]
    v_refs = refs[bps * ppb:2 * bps * ppb]
    y_ref, qbd_ref, oblk_ref, km_ref, m_ref, l_ref = refs[2 * bps * ppb:]
    step = pl.program_id(1)
    R = MB_HEADS * ts
    own = (lax.broadcasted_iota(jnp.int32, (R, MB_W), 0) // ts
           == lax.broadcasted_iota(jnp.int32, (R, MB_W), 1) // MB_HD)
    blane = lax.broadcasted_iota(jnp.int32, (R, n_blocks), 1)
    kml = lax.broadcasted_iota(jnp.int32, (MB_W, n_blocks), 1)

    @pl.when(step == 0)
    def _():
        for ref in (m_ref, l_ref, km_ref):
            ref[...] = jnp.zeros(ref.shape, F32)
        qbd_ref[...] = jnp.where(own, jnp.concatenate([q_ref[0]] * MB_HEADS, axis=0), 0.0)

    qbd = qbd_ref[...]
    qb = (qbd * (MB_HD ** -0.5)).astype(BF16)
    page = lambda ref: ref[0, 0].reshape(MB_W, PAGE_SIZE)
    m_all, l_all, km_all = m_ref[...], l_ref[...], km_ref[...]
    blocks = lambda refs, i: jnp.concatenate([page(refs[ppb * i + j]) for j in range(ppb)], axis=1)
    scores, probs = [], []
    for i in range(bps):
        kblk = blocks(k_refs, i)
        scores.append(jnp.dot(qb, kblk.astype(BF16), preferred_element_type=F32))
        kmean = jnp.sum(kblk, axis=1, keepdims=True) * (1.0 / MB_BLOCK)
        km_all = jnp.where(kml == step * bps + i, kmean, km_all)
    for i, s in enumerate(scores):
        n = step * bps + i
        m_n = jnp.max(s, axis=1, keepdims=True)
        p = jnp.exp(s - m_n)
        probs.append(p.astype(BF16))
        m_all = jnp.where(blane == n, m_n, m_all)
        l_all = jnp.where(blane == n, jnp.sum(p, axis=1, keepdims=True), l_all)
    for i, p in enumerate(probs):
        oblk_ref[step * bps + i] = _dot_nt(p, blocks(v_refs, i).astype(BF16))
    km_ref[...] = km_all
    m_ref[...] = m_all
    l_ref[...] = l_all

    @pl.when(step == n_blocks // bps - 1)
    def _():
        gate = jnp.dot(qbd, km_ref[...], preferred_element_type=F32, precision=lax.Precision.HIGHEST)
        sel = _top_blocks(gate, n_blocks, axis=1) > 0.0
        s_own = _dot_nt(qb, kn_ref[0].astype(BF16))
        tq = lax.broadcasted_iota(jnp.int32, (R, ts), 0) % ts
        tk = lax.broadcasted_iota(jnp.int32, (R, ts), 1)
        s_own = jnp.where(tk <= tq, s_own, NEG_INF)
        mb = jnp.where(sel, m_ref[...], NEG_INF)
        m_all = jnp.maximum(jnp.max(s_own, axis=1, keepdims=True), jnp.max(mb, axis=1, keepdims=True))
        w = jnp.exp(mb - m_all)
        p_own = jnp.exp(s_own - m_all)
        l = jnp.sum(p_own, axis=1, keepdims=True) + jnp.sum(w * l_ref[...], axis=1, keepdims=True)
        o = jnp.dot(p_own.astype(BF16), vn_ref[0].astype(BF16), preferred_element_type=F32)
        for nb in range(n_blocks):
            o = o + w[:, nb:nb + 1] * oblk_ref[nb]
        o = jnp.where(own, o / l, 0.0)
        y = o[0:ts]
        for h in range(1, MB_HEADS):
            y = y + o[h * ts:(h + 1) * ts]
        y_ref[0] = y.astype(BF16)


def _moba_sample(q, kn, vn, cache_kt, cache_vt, page_table, layer):
    DB, ts, _ = q.shape
    n_pages = page_table.shape[1]
    ppb = MB_BLOCK // PAGE_SIZE
    n_blocks = n_pages // ppb
    R = MB_HEADS * ts
    tok = pl.BlockSpec((1, ts, MB_W), lambda b, n, pt: (b, 0, 0))
    bps = math.gcd(n_blocks, SAMPLE_BLOCKS_PER_STEP)
    pps = bps * ppb

    def page(i):
        return pl.BlockSpec((1, 1, MB_HEADS, MB_HD, PAGE_SIZE),
                            lambda b, n, pt: (layer, pt[b * n_pages + n * pps + i], 0, 0, 0))

    pages = [page(i) for i in range(pps)]
    grid_spec = pltpu.PrefetchScalarGridSpec(
        num_scalar_prefetch=1,
        grid=(DB, n_blocks // bps),
        in_specs=[tok, tok, tok] + pages + pages,
        out_specs=tok,
        scratch_shapes=[pltpu.VMEM((R, MB_W), F32),
                        pltpu.VMEM((n_blocks, R, MB_W), F32),
                        pltpu.VMEM((MB_W, n_blocks), F32),
                        pltpu.VMEM((R, n_blocks), F32),
                        pltpu.VMEM((R, n_blocks), F32)],
    )
    return pl.pallas_call(
        functools.partial(_moba_sample_kernel, n_blocks=n_blocks, ts=ts, bps=bps),
        grid_spec=grid_spec,
        out_shape=jax.ShapeDtypeStruct((DB, ts, MB_W), BF16),
        compiler_params=_cparams(("arbitrary", "arbitrary")),
        name="moba_sample",
    )(page_table.reshape(-1), q, kn, vn, *([cache_kt] * pps), *([cache_vt] * pps))


def _gla_kernel(qg_ref, kg_ref, vg_ref, rg_ref, lg_ref, s0_ref, gn_ref, y_ref, sfin_ref, st_ref, *,
                n_chunks, valid):
    t = pl.program_id(1)
    C = GLA_CHUNK

    @pl.when(t == 0)
    def _():
        for h in range(GLA_HEADS):
            st_ref[:, h * GLA_DK:(h + 1) * GLA_DK] = s0_ref[0, h].T

    ri = lax.broadcasted_iota(jnp.int32, (C, C), 0)
    ci = lax.broadcasted_iota(jnp.int32, (C, C), 1)
    tri = (ci <= ri).astype(BF16)
    m1 = (ri >= 32) & (ci < 32)
    m2 = (ri // 32 == ci // 32) & (ri % 32 >= 16) & (ci % 32 < 16)
    m3 = (ri // 16 == ci // 16) & (ci <= ri)
    grp = lax.broadcasted_iota(jnp.int32, (C, GLA_KW), 0)

    heads = [(slice(h * GLA_DK, (h + 1) * GLA_DK), slice(h * GLA_DV, (h + 1) * GLA_DV)) for h in range(GLA_HEADS)]
    sts = [st_ref[:, kl] for kl, _ in heads]
    for g0 in range(0, n_chunks, GLA_GROUP):
        cs = range(g0, min(g0 + GLA_GROUP, n_chunks))
        rows = {c: slice(c * C, c * C + valid) for c in cs}

        def chunk(ref, c, lanes=slice(None)):
            x = ref[0, rows[c], lanes]
            return x if valid == C else jnp.concatenate([x, jnp.zeros((C - valid, x.shape[1]), x.dtype)], axis=0)

        bcum = {}
        for c in cs:
            hi, mid, lo = _split3(chunk(lg_ref, c))
            bcum[c] = (jnp.dot(tri, hi, preferred_element_type=F32) + jnp.dot(tri, mid, preferred_element_type=F32)
                       + jnp.dot(tri, lo, preferred_element_type=F32))
        scaled = {}
        for c in cs:
            b = bcum[c]
            q = chunk(qg_ref, c) * (GLA_DK ** -0.5)
            k = chunk(kg_ref, c)
            brow = lambda i: jnp.broadcast_to(b[i:i + 1, :], (C, GLA_KW))
            ref1 = brow(31)
            ref2 = jnp.where(grp < 32, brow(15), brow(47))
            ref3 = jnp.where(grp < 16, brow(7),
                             jnp.where(grp < 32, brow(23), jnp.where(grp < 48, brow(39), brow(55))))
            b_last = b[C - 1:C, :]
            scaled[c] = dict(
                qs=(q * jnp.exp(b)).astype(BF16), ks=(k * jnp.exp(b_last - b)).astype(BF16),
                q1=(q * jnp.exp(jnp.minimum(b - ref1, 0.0))).astype(BF16),
                k1=(k * jnp.exp(jnp.minimum(ref1 - b, 0.0))).astype(BF16),
                q2=(q * jnp.exp(jnp.minimum(b - ref2, 0.0))).astype(BF16),
                k2=(k * jnp.exp(jnp.minimum(ref2 - b, 0.0))).astype(BF16),
                q3=(q * jnp.exp(b - ref3)).astype(BF16), k3=(k * jnp.exp(ref3 - b)).astype(BF16),
                decay=jnp.exp(b_last))
        levels = {}
        for c in cs:
            x = scaled[c]
            for h, (kl, _) in enumerate(heads):
                levels[c, h] = (_dot_nt(x["q1"][:, kl], x["k1"][:, kl]), _dot_nt(x["q2"][:, kl], x["k2"][:, kl]),
                                _dot_nt(x["q3"][:, kl], x["k3"][:, kl]))
        intra, update = {}, {}
        for c in cs:
            for h, (kl, vl) in enumerate(heads):
                l1, l2, l3 = levels[c, h]
                a = jnp.where(m1, l1, 0.0) + jnp.where(m2, l2, 0.0) + jnp.where(m3, l3, 0.0)
                vf = chunk(vg_ref, c, vl)
                intra[c, h] = jnp.dot(a.astype(BF16), vf.astype(BF16), preferred_element_type=F32)
                update[c, h] = jnp.dot(vf.T.astype(BF16), scaled[c]["ks"][:, kl], preferred_element_type=F32)
        for c in cs:
            for h, (kl, vl) in enumerate(heads):
                o = _dot_nt(scaled[c]["qs"][:, kl], sts[h].astype(BF16)) + intra[c, h]
                sts[h] = sts[h] * scaled[c]["decay"][:, kl] + update[c, h]
                y_ref[0, rows[c], vl] = (_rms(o[:valid], gn_ref[:, vl]) * rg_ref[0, rows[c], vl]).astype(BF16)
    for (kl, _), st in zip(heads, sts):
        st_ref[:, kl] = st

    @pl.when(t == pl.num_programs(1) - 1)
    def _():
        for h in range(GLA_HEADS):
            sfin_ref[0, h] = st_ref[:, h * GLA_DK:(h + 1) * GLA_DK].T


def _gla(qg, kg, vg, rg, lg, s0, gn):
    Bn, T, _ = qg.shape
    assert T < GLA_CHUNK or T % GLA_CHUNK == 0
    TT = min(T, 512)
    kspec = pl.BlockSpec((1, TT, GLA_KW), lambda b, t: (b, t, 0))
    vspec = pl.BlockSpec((1, TT, GLA_VW), lambda b, t: (b, t, 0))
    sspec = pl.BlockSpec((1, GLA_HEADS, GLA_DK, GLA_DV), lambda b, t: (b, 0, 0, 0))
    return pl.pallas_call(
        functools.partial(_gla_kernel, n_chunks=max(1, TT // GLA_CHUNK), valid=min(TT, GLA_CHUNK)),
        grid=(Bn, T // TT),
        in_specs=[kspec, kspec, vspec, vspec, kspec, sspec, _const_spec((1, GLA_VW))],
        out_specs=[vspec, sspec],
        out_shape=[jax.ShapeDtypeStruct((Bn, T, GLA_VW), BF16),
                   jax.ShapeDtypeStruct((Bn, GLA_HEADS, GLA_DK, GLA_DV), F32)],
        scratch_shapes=[pltpu.VMEM((GLA_DV, GLA_KW), F32)],
        compiler_params=_cparams(("arbitrary", "arbitrary")),
        name="gla",
    )(qg, kg, vg, rg, lg, s0, gn)


def _merge_ffn_kernel(x_ref, yp_ref, ym_ref, yg_ref, sg_ref, wp_ref, wmb_ref, wgl_ref, wo_ref,
                      gpost_ref, gfpre_ref, gfpost_ref, w1_ref, w2_ref, o_ref):
    tm = x_ref.shape[0]
    subs = [slice(r, r + FFN_SUB_ROWS) for r in range(0, tm, FFN_SUB_ROWS)] if tm > FFN_SUB_ROWS else [slice(0, tm)]

    def dot(a, w):
        return jnp.dot(a.astype(BF16), w, preferred_element_type=F32)

    branches = [[dot(y_ref[r, :], w_ref[...]) for y_ref, w_ref in
                 ((yp_ref, wp_ref), (ym_ref, wmb_ref), (yg_ref, wgl_ref))] for r in subs]
    merged = [sum(sg_ref[r, 1024 * j:1024 * (j + 1)] * b for j, b in enumerate(bs))
              for r, bs in zip(subs, branches)]
    z = [dot(m, wo_ref[...]) for m in merged]
    x1 = [x_ref[r, :] + _rms(zi, gpost_ref[...]) for r, zi in zip(subs, z)]
    hb = [_rms(xi, gfpre_ref[...]).astype(BF16) for xi in x1]
    FC = 1024
    f = [jnp.zeros(xi.shape, F32) for xi in x1]

    def second(i, c, a):
        a = jnp.maximum(a, 0.0)
        f[i] = f[i] + dot(a * a, w2_ref[c * FC:(c + 1) * FC, :])

    pending = None
    for c in range(D_FF // FC):
        for i in range(len(subs)):
            a = jnp.dot(hb[i], w1_ref[:, c * FC:(c + 1) * FC], preferred_element_type=F32)
            if pending is not None:
                second(*pending)
            pending = (i, c, a)
    second(*pending)
    for i, r in enumerate(subs):
        o_ref[r, :] = x1[i] + _rms(f[i], gfpost_ref[...])


def _merge_ffn(x, yp, ym, yg, sg, layer, wp, wmb, wgl, wo, gpost, gfpre, gfpost, w1, w2):
    M = x.shape[0]
    tm = min(FFN_ROW_TILE, M)
    row = lambda w: pl.BlockSpec((tm, w), lambda i: (i, 0))
    vec = _const_spec((1, D_MODEL))
    return pl.pallas_call(
        _merge_ffn_kernel,
        grid=(M // tm,),
        in_specs=[row(D_MODEL), row(POOL_W), row(MB_W), row(GLA_VW), row(3 * D_MODEL),
                  *(_layer_spec(w, layer) for w in (wp, wmb, wgl, wo)),
                  vec, vec, vec, _layer_spec(w1, layer), _layer_spec(w2, layer)],
        out_specs=row(D_MODEL),
        out_shape=jax.ShapeDtypeStruct((M, D_MODEL), F32),
        compiler_params=_cparams(("arbitrary",)),
        name="merge_ffn",
    )(x, yp, ym, yg, sg, wp, wmb, wgl, wo, gpost, gfpre, gfpost, w1, w2)


def _layer(x, Bn, T, tabs, hist16, pos0, s0, sample_attend, layer, lw, prompt_state=None):
    (w_in, wg, wa2, ba, gn, wgrp, pscale, wp, wmb, wgl, wo, gpre, gpost, gfpre, gfpost, w1, w2) = lw
    seq = lambda a: a.reshape(Bn, T, a.shape[-1])
    if sample_attend is None:
        u, q, kb, km, k, v, qg, kg, vg, rg, lg, sg = _in_proj(x, gpre, w_in, layer, wg, wa2, ba, tabs,
                                                             prompt_state=(T,) + tuple(prompt_state))
        ym = _moba_prompt(seq(q), km.reshape(Bn, T // MB_BLOCK, MB_W), seq(kb), v, layer)
    else:
        u, q, k, v, qg, kg, vg, rg, lg, sg = _in_proj(x, gpre, w_in, layer, wg, wa2, ba, tabs)
        ym = sample_attend(seq(q), seq(k), seq(v))
        k, v = (a.reshape(Bn, T, MB_HEADS, MB_HD) for a in (k, v))
    u3 = seq(u)
    yp = _pool(u3, hist16, wgrp, pscale, pos0)
    yg, s_new = _gla(seq(qg), seq(kg), seq(vg), seq(rg), seq(lg), s0, gn)
    yg = yg.reshape(Bn * T, GLA_VW)
    x = _merge_ffn(x, yp.reshape(Bn * T, POOL_W), ym.reshape(Bn * T, MB_W), yg, sg, layer,
                   wp, wmb, wgl, wo, gpost, gfpre, gfpost, w1, w2)
    new_hist = jnp.concatenate([hist16[:, 1:], u3], axis=1)[:, -POOL_HIST:]
    return x, k, v, new_hist, s_new


def kernel(x_prompt, x_sample, cache_k, cache_v, state_pool, state_gla, page_table, w_in, w_gla_a2, b_gla_a,
           gla_norm, w_pool_grp, pool_scale, w_pool_out, w_mb_out, w_gla_out, w_o, norm_mix_pre,
           norm_mix_post, norm_ffn_pre, norm_ffn_post, w_ff1, w_ff2):
    B, T, D = x_prompt.shape
    DB, TS, _ = x_sample.shape
    depth = w_in.shape[0]
    n_pages = page_table.shape[1]
    past_len = n_pages * PAGE_SIZE
    assert D == D_MODEL and T % MB_BLOCK == 0 and past_len % MB_BLOCK == 0
    assert past_len // MB_BLOCK >= MB_TOPK and (B * T) % ROW_TILE == 0 and MB_BLOCK == 2 * PAGE_SIZE

    tabs_p = _rope_tables(jnp.arange(T))
    reps = max(1, min(ROW_TILE, DB * TS) // TS)
    tabs_s = tuple(jnp.tile(t, (reps, 1)) for t in _rope_tables(past_len + jnp.arange(TS)))
    ckt = jnp.transpose(cache_k, (0, 1, 3, 4, 2))
    cvt = jnp.transpose(cache_v, (0, 1, 3, 4, 2))
    zrow = lambda n: jnp.zeros((n, 1, POOL_W), F32)
    hist_p = jnp.zeros((B, 16, POOL_W), F32)
    s0_p = jnp.zeros((B, GLA_HEADS, GLA_DK, GLA_DV), F32)

    hp = x_prompt.reshape(B * T, D)
    hs = x_sample.reshape(DB * TS, D)
    outs_p, outs_s = [], []
    kt_all = vt_all = None
    bf = lambda a: a.astype(BF16)
    vec = lambda a: a.reshape(1, -1)
    w_in_b, wp_b, wmb_b, wgl_b, wo_b, w1_b, w2_b = (bf(w) for w in (w_in, w_pool_out, w_mb_out, w_gla_out,
                                                                    w_o, w_ff1, w_ff2))
    for l in range(depth):
        lw = (w_in_b, w_in_b[l][:, W_MAIN + GLA_RANK:],
              bf(w_gla_a2[l]), vec(b_gla_a[l]), vec(gla_norm[l]), bf(w_pool_grp[l]), vec(pool_scale[l]),
              wp_b, wmb_b, wgl_b, wo_b,
              vec(norm_mix_pre[l]), vec(norm_mix_post[l]), vec(norm_ffn_pre[l]), vec(norm_ffn_post[l]),
              w1_b, w2_b)
        wvt = w_in_b[l][:, POOL_W + 2 * MB_W:POOL_W + 3 * MB_W].T
        hp, kt_all, vt_all, *op = _layer(hp, B, T, tabs_p, hist_p, 0, s0_p, None, l, lw,
                                         prompt_state=(depth, l, wvt, kt_all, vt_all))
        hist_s = jnp.concatenate([zrow(DB), state_pool[l]], axis=1)
        attend_s = functools.partial(_moba_sample, cache_kt=ckt, cache_vt=cvt, page_table=page_table, layer=l)
        hs, *os_ = _layer(hs, DB, TS, tabs_s, hist_s, past_len, state_gla[l], attend_s, l, lw)
        outs_p.append(op)
        outs_s.append(os_)
    stack = lambda outs, i: jnp.stack([o[i] for o in outs])
    untr = lambda a: jnp.transpose(a.reshape(depth, B, MB_HEADS, MB_HD, T), (0, 1, 4, 2, 3))
    return (hp.reshape(B, T, D), hs.reshape(DB, TS, D),
            untr(kt_all), untr(vt_all), stack(outs_p, 0), stack(outs_p, 1),
            stack(outs_s, 0), stack(outs_s, 1), stack(outs_s, 2), stack(outs_s, 3))
```

```python
import functools
import math

import jax
import jax.numpy as jnp
from jax import lax
from jax.experimental import pallas as pl
from jax.experimental.pallas import tpu as pltpu

F32 = jnp.float32
BF16 = jnp.bfloat16

D_MODEL = 1024
PAGE_SIZE = 128
POOL_GC = 128
POOL_W = 512
POOL_WINDOWS = (2, 4, 8, 16)
POOL_HIST = 15
MB_HEADS = 8
MB_HD = 64
MB_W = 512
MB_BLOCK = 256
MB_TOPK = 3
ROPE_THETA = 500000.0
ROPE_DIM = 16
GLA_HEADS = 4
GLA_DK = 64
GLA_DV = 128
GLA_KW = 256
GLA_VW = 512
GLA_RANK = 16
GLA_NORMALIZER = 16.0
GLA_CHUNK = 64
D_FF = 4096
EPS = 1e-6
COL_U = 0
COL_Q = COL_U + POOL_W
COL_K = COL_Q + MB_W
COL_V = COL_K + MB_W
COL_QG = COL_V + MB_W
COL_KG = COL_QG + GLA_KW
COL_VG = COL_KG + GLA_KW
COL_RG = COL_VG + GLA_VW
W_MAIN = COL_RG + GLA_VW
NEG_INF = float("-inf")
LOG2E = 1.4426950408889634

VMEM_LIMIT_BYTES = 56 * 1024 * 1024
ROW_TILE = 512
FFN_ROW_TILE = 512
FFN_SUB_ROWS = 256
SAMPLE_BLOCKS_PER_STEP = 16
GLA_GROUP = 8


def _cparams(sem):
    return pltpu.CompilerParams(dimension_semantics=sem, vmem_limit_bytes=VMEM_LIMIT_BYTES)


def _const_spec(shape):
    nd = len(shape)
    return pl.BlockSpec(shape, lambda *_: (0,) * nd, pipeline_mode=pl.Buffered(1))


def _layer_spec(stacked, layer, cols=None, col_block=0):
    _, k, n = stacked.shape
    return pl.BlockSpec((None, k, n if cols is None else cols), lambda *_: (layer, 0, col_block),
                        pipeline_mode=pl.Buffered(1))


def _rms(x, g):
    ms = jnp.mean(x * x, axis=-1, keepdims=True)
    return x * lax.rsqrt(ms + EPS) * g


def _sigmoid(x):
    return 0.5 * jnp.tanh(0.5 * x) + 0.5


def _split3(x):
    hi = x.astype(BF16)
    r = x - hi.astype(F32)
    mid = r.astype(BF16)
    lo = (r - mid.astype(F32)).astype(BF16)
    return hi, mid, lo


def _in_proj_kernel(*refs, prompt, first_layer):
    x_ref, gpre_ref, wm_ref, wa_ref, wg_ref, wa2_ref, ba_ref, cos_ref, sa_ref, sb_ref = refs[:10]
    n_in = 10 + (0 if not prompt else 1 if first_layer else 3)
    wvt_ref = refs[10] if prompt else None
    outs = refs[n_in:]
    u_ref, q_ref = outs[:2]
    qg_ref, kg_ref, vg_ref, rg_ref, lg_ref, sg_ref = outs[-6:]
    h = _rms(x_ref[...], gpre_ref[...]).astype(BF16)

    def proj(w_ref, a, b):
        return jnp.dot(h, w_ref[:, a:b], preferred_element_type=F32)

    cos, sa, sb = cos_ref[...], sa_ref[...], sb_ref[...]

    def rope(col0):
        half = ROPE_DIM // 2
        x = proj(wm_ref, col0, col0 + MB_W)
        for j in range(MB_W // 128):
            xj = x[:, 128 * j:128 * (j + 1)]
            yield j, xj * cos + pltpu.roll(xj, 128 - half, 1) * sa + pltpu.roll(xj, half, 1) * sb

    for j in range(3):
        sg_ref[:, 1024 * j:1024 * (j + 1)] = _sigmoid(proj(wg_ref, 1024 * j, 1024 * (j + 1))).astype(BF16)
    r = proj(wm_ref, COL_RG, W_MAIN)
    rg_ref[...] = r * _sigmoid(r)
    a = jnp.dot(h, wa_ref[:, :GLA_RANK], preferred_element_type=F32)
    xg = jnp.dot(a.astype(BF16), wa2_ref[...], preferred_element_type=F32) + ba_ref[...]
    lg_ref[...] = (jnp.minimum(xg, 0.0) - jnp.log1p(jnp.exp(-jnp.abs(xg)))) * (1.0 / GLA_NORMALIZER)
    u_ref[...] = proj(wm_ref, COL_U, COL_Q)
    for j, rj in rope(COL_Q):
        q_ref[:, 128 * j:128 * (j + 1)] = rj
    if prompt:
        kb_ref, km_ref, kt_ref, vt_ref = outs[2:6]
        n_blk = km_ref.shape[0]
        col_sums = [[] for _ in range(n_blk)]
        for j, rj in rope(COL_K):
            kb_ref[:, 128 * j:128 * (j + 1)] = rj.astype(BF16)
            kt_ref[0, 0, 128 * j:128 * (j + 1), :] = rj.T
            for i in range(n_blk):
                col_sums[i].append(jnp.sum(rj[i * MB_BLOCK:(i + 1) * MB_BLOCK], axis=0, keepdims=True))
        for i in range(n_blk):
            km_ref[i] = jnp.concatenate(col_sums[i], axis=1) * (1.0 / MB_BLOCK)
        vt_ref[0, 0] = _dot_nt(wvt_ref[...], h)
        for d in range(1, kt_ref.shape[0]):
            kt_ref[d] = jnp.zeros(kt_ref.shape[1:], F32)
            vt_ref[d] = jnp.zeros(vt_ref.shape[1:], F32)
    else:
        k_ref, v_ref = outs[2:4]
        for j, rj in rope(COL_K):
            k_ref[:, 128 * j:128 * (j + 1)] = rj
        v_ref[...] = proj(wm_ref, COL_V, COL_QG)
    qg_ref[...] = proj(wm_ref, COL_QG, COL_KG)
    kg_ref[...] = proj(wm_ref, COL_KG, COL_VG)
    vg_ref[...] = proj(wm_ref, COL_VG, COL_RG)


def _in_proj(x, gpre, w_in, layer, wg, wa2, ba, tabs, prompt_state=None):
    M = x.shape[0]
    tm = min(ROW_TILE, M)
    period = tabs[0].shape[0] // tm
    row = lambda w: pl.BlockSpec((tm, w), lambda i: (i, 0))
    tab = pl.BlockSpec((tm, 128), lambda i: (i % period, 0))
    f32 = lambda w: jax.ShapeDtypeStruct((M, w), F32)
    tail_w = (GLA_KW, GLA_KW, GLA_VW, GLA_VW, GLA_KW, 3 * D_MODEL)
    assert W_MAIN % 128 == 0
    in_specs = [row(D_MODEL), _const_spec((1, D_MODEL)), _layer_spec(w_in, layer, cols=W_MAIN),
                _layer_spec(w_in, layer, cols=128, col_block=W_MAIN // 128),
                _const_spec(wg.shape), _const_spec(wa2.shape), _const_spec((1, GLA_KW)), tab, tab, tab]
    args = [x, gpre, w_in, w_in, wg, wa2, ba, *tabs]
    aliases = {}
    if prompt_state is None:
        out_specs = [row(POOL_W), row(MB_W), row(MB_W), row(MB_W)]
        out_shape = [f32(POOL_W), f32(MB_W), f32(MB_W), f32(MB_W)]
        first = False
    else:
        seq_len, depth, layer, wvt, kt_all, vt_all = prompt_state
        assert tm % MB_BLOCK == 0 and seq_len % tm == 0
        tps = seq_len // tm
        first = layer == 0
        stacked = jax.ShapeDtypeStruct((depth, M // seq_len, MB_W, seq_len), F32)
        in_specs.append(_const_spec(wvt.shape))
        args.append(wvt)
        if first:
            st_spec = pl.BlockSpec((depth, 1, MB_W, tm), lambda i: (0, i // tps, 0, i % tps))
        else:
            st_spec = pl.BlockSpec((1, 1, MB_W, tm), lambda i: (layer, i // tps, 0, i % tps))
            in_specs += [pl.BlockSpec(memory_space=pl.ANY)] * 2
            args += [kt_all, vt_all]
            aliases = {11: 4, 12: 5}
        out_specs = [row(POOL_W), row(MB_W), row(MB_W),
                     pl.BlockSpec((tm // MB_BLOCK, 1, MB_W), lambda i: (i, 0, 0)),
                     st_spec, st_spec]
        out_shape = [f32(POOL_W), f32(MB_W), jax.ShapeDtypeStruct((M, MB_W), BF16),
                     jax.ShapeDtypeStruct((M // MB_BLOCK, 1, MB_W), F32), stacked, stacked]
    return pl.pallas_call(
        functools.partial(_in_proj_kernel, prompt=prompt_state is not None, first_layer=first),
        grid=(M // tm,),
        in_specs=in_specs,
        out_specs=out_specs + [row(w) for w in tail_w],
        out_shape=out_shape + [f32(w) for w in tail_w[:-1]] + [jax.ShapeDtypeStruct((M, tail_w[-1]), BF16)],
        input_output_aliases=aliases,
        compiler_params=_cparams(("arbitrary",)),
        name="in_proj",
    )(*args)


def _rope_tables(pos):
    half = ROPE_DIM // 2
    inv = ROPE_THETA ** (-jnp.arange(half, dtype=F32) / half)
    ang = pos.astype(F32)[:, None] * inv[None, :]
    cos, sin = jnp.cos(ang), jnp.sin(ang)
    n = pos.shape[0]
    zeros = jnp.zeros((n, MB_HD - ROPE_DIM), F32)
    zh = jnp.zeros((n, half), F32)
    c = jnp.concatenate([cos, cos, zeros + 1.0], axis=1)
    sa = jnp.concatenate([-sin, zh, zeros], axis=1)
    sb = jnp.concatenate([zh, sin, zeros], axis=1)
    two = lambda t: jnp.concatenate([t, t], axis=1)
    return two(c), two(sa), two(sb)


def _pool_kernel(u_ref, hist_ref, wgrp_ref, scale_ref, y_ref, ext_ref, *, pos0, T, R, nb):
    for s in range(nb):
        ext_ref[s, 0:16, :] = hist_ref[s]
        ext_ref[s, 16:16 + T, :] = u_ref[s]
    for c in range(T // R):
        r0 = 16 + c * R
        posp1 = lax.broadcasted_iota(jnp.int32, (R, 1), 0) + (pos0 + c * R + 1)
        for g, w in enumerate(POOL_WINDOWS):
            lanes = slice(g * POOL_GC, (g + 1) * POOL_GC)
            cnt = jnp.minimum(posp1, w).astype(F32)
            ds = []
            for s in range(nb):
                cur = ext_ref[s, r0:r0 + R, lanes]
                acc = cur
                for j in range(1, w):
                    acc = acc + ext_ref[s, r0 - j:r0 - j + R, lanes]
                ds.append(acc / cnt - cur)
            d = ds[0] if nb == 1 else jnp.concatenate(ds, axis=0)
            yg = jnp.dot(d.astype(BF16), wgrp_ref[g], preferred_element_type=F32) * scale_ref[:, lanes]
            for s in range(nb):
                y_ref[s, c * R:(c + 1) * R, lanes] = yg[s * R:(s + 1) * R].astype(BF16)


def _pool(u, hist16, wgrp, scale, pos0):
    Bn, T, _ = u.shape
    R = min(T, 256)
    nb = Bn if T < MB_BLOCK else 1
    seq = pl.BlockSpec((nb, T, POOL_W), lambda b: (b, 0, 0))
    return pl.pallas_call(
        functools.partial(_pool_kernel, pos0=pos0, T=T, R=R, nb=nb),
        grid=(Bn // nb,),
        in_specs=[seq, pl.BlockSpec((nb, 16, POOL_W), lambda b: (b, 0, 0)),
                  _const_spec(wgrp.shape), _const_spec((1, POOL_W))],
        out_specs=seq,
        out_shape=jax.ShapeDtypeStruct((Bn, T, POOL_W), BF16),
        scratch_shapes=[pltpu.VMEM((nb, 16 + T, POOL_W), F32)],
        compiler_params=_cparams(("arbitrary",)),
        name="pool",
    )(u, hist16, wgrp, scale)


def _top_blocks(gate, n_valid, axis):
    n_blocks = gate.shape[axis]
    iota = lax.broadcasted_iota(jnp.int32, gate.shape, axis)
    g = jnp.where(iota < n_valid, gate, NEG_INF)
    sel = jnp.zeros(gate.shape, F32)
    for _ in range(MB_TOPK):
        m = jnp.max(g, axis=axis, keepdims=True)
        idx = jnp.min(jnp.where(g == m, iota, n_blocks), axis=axis, keepdims=True)
        pick = (iota == idx) & (m > NEG_INF)
        sel = jnp.where(pick, 1.0, sel)
        g = jnp.where(pick, NEG_INF, g)
    return sel


def _col_max(s):
    r, c = s.shape
    return jnp.max(jnp.max(s.reshape(r // 8, 8, c), axis=0), axis=0, keepdims=True)


def _dot_nt(a, b, precision=None):
    return lax.dot_general(a, b, (((1,), (1,)), ((), ())), preferred_element_type=F32, precision=precision)


def _moba_prompt_kernel(q_ref, km_ref, k_ref, vt_ref, y_ref, s_ref, vtb_ref):
    B = MB_BLOCK
    n_blocks = km_ref.shape[1]
    qlane = lax.broadcasted_iota(jnp.int32, (B, 128), 1) // MB_HD
    kmlane = lax.broadcasted_iota(jnp.int32, (n_blocks, 128), 1) // MB_HD
    causal = lax.broadcasted_iota(jnp.int32, (B, B), 0) <= lax.broadcasted_iota(jnp.int32, (B, B), 1)
    drow = lax.broadcasted_iota(jnp.int32, (128, B), 0) // MB_HD
    vtb_ref[0:128, :] = vt_ref[0, 0].astype(BF16)
    vtb_ref[128:, :] = jnp.ones((vtb_ref.shape[0] - 128, vtb_ref.shape[1]), BF16)

    def scores_stage(c):
        q = q_ref[0, c * B:(c + 1) * B, :]
        qh = [jnp.where(qlane == hh, q, 0.0) for hh in range(2)]
        qs = [(x * (MB_HD ** -0.5 * LOG2E)).astype(BF16) for x in qh]
        raw = {(hh, j): _dot_nt(k_ref[0, j * B:(j + 1) * B, :], qs[hh])
               for hh in range(2) for j in range(c + 1)}
        maxima = []
        for hh in range(2):
            if c > 0:
                gate_t = _dot_nt(jnp.where(kmlane == hh, km_ref[0], 0.0), qh[hh], precision=lax.Precision.HIGHEST)
                sel = jnp.where(_top_blocks(gate_t, c, axis=0) > 0.0, 0.0, NEG_INF)
            m = None
            for j in range(c + 1):
                s = jnp.where(causal, raw[hh, j], NEG_INF) if j == c else raw[hh, j] + sel[j:j + 1, :]
                s_ref[2 * (c % 2) + hh, j] = s
                mj = _col_max(s)
                m = mj if m is None else jnp.maximum(m, mj)
            maxima.append(m)
        return maxima

    def values_stage(c, maxima):
        outs = []
        for hh in range(2):
            probs = [jnp.exp2(s_ref[2 * (c % 2) + hh, j] - maxima[hh]).astype(BF16)
                     for j in range(c + 1)]
            acc = jnp.dot(vtb_ref[:, :(c + 1) * B], jnp.concatenate(probs, axis=0), preferred_element_type=F32)
            outs.append(acc[:128] / acc[128:129])
        y_ref[0, c * B:(c + 1) * B, :] = jnp.where(drow == 0, outs[0], outs[1]).T.astype(BF16)

    pending = None
    for c in range(n_blocks):
        maxima = scores_stage(c)
        if pending is not None:
            values_stage(*pending)
        pending = (c, maxima)
    values_stage(*pending)


def _moba_prompt(q, km, kb, vt_all, layer):
    Bn, T, _ = q.shape
    n_blocks = T // MB_BLOCK
    seq = pl.BlockSpec((1, T, 128), lambda b, hp: (b, 0, hp))
    return pl.pallas_call(
        _moba_prompt_kernel,
        grid=(Bn, MB_W // 128),
        in_specs=[seq,
                  pl.BlockSpec((1, n_blocks, 128), lambda b, hp: (b, 0, hp)),
                  seq,
                  pl.BlockSpec((1, 1, 128, T), lambda b, hp: (layer, b, hp, 0))],
        out_specs=seq,
        out_shape=jax.ShapeDtypeStruct((Bn, T, MB_W), BF16),
        scratch_shapes=[pltpu.VMEM((4, n_blocks, MB_BLOCK, MB_BLOCK), F32), pltpu.VMEM((128 + 16, T), BF16)],
        compiler_params=_cparams(("arbitrary", "arbitrary")),
        name="moba_prompt",
    )(q, km, kb, vt_all)


def _moba_sample_kernel(pt_ref, q_ref, kn_ref, vn_ref, *refs, n_blocks, ts, bps):
    ppb = MB_BLOCK // PAGE_SIZE
    k_refs = refs[:bps * ppb]
    v_refs = refs[bps * ppb:2 * bps * ppb]
    y_ref, qbd_ref, oblk_ref, km_ref, m_ref, l_ref = refs[2 * bps * ppb:]
    step = pl.program_id(1)
    R = MB_HEADS * ts
    own = (lax.broadcasted_iota(jnp.int32, (R, MB_W), 0) // ts
           == lax.broadcasted_iota(jnp.int32, (R, MB_W), 1) // MB_HD)
    blane = lax.broadcasted_iota(jnp.int32, (R, n_blocks), 1)
    kml = lax.broadcasted_iota(jnp.int32, (MB_W, n_blocks), 1)

    @pl.when(step == 0)
    def _():
        for ref in (m_ref, l_ref, km_ref):
            ref[...] = jnp.zeros(ref.shape, F32)
        qbd_ref[...] = jnp.where(own, jnp.concatenate([q_ref[0]] * MB_HEADS, axis=0), 0.0)

    qbd = qbd_ref[...]
    qb = (qbd * (MB_HD ** -0.5)).astype(BF16)
    page = lambda ref: ref[0, 0].reshape(MB_W, PAGE_SIZE)
    m_all, l_all, km_all = m_ref[...], l_ref[...], km_ref[...]
    blocks = lambda refs, i: jnp.concatenate([page(refs[ppb * i + j]) for j in range(ppb)], axis=1)
    scores, probs = [], []
    for i in range(bps):
        kblk = blocks(k_refs, i)
        scores.append(jnp.dot(qb, kblk.astype(BF16), preferred_element_type=F32))
        kmean = jnp.sum(kblk, axis=1, keepdims=True) * (1.0 / MB_BLOCK)
        km_all = jnp.where(kml == step * bps + i, kmean, km_all)
    for i, s in enumerate(scores):
        n = step * bps + i
        m_n = jnp.max(s, axis=1, keepdims=True)
        p = jnp.exp(s - m_n)
        probs.append(p.astype(BF16))
        m_all = jnp.where(blane == n, m_n, m_all)
        l_all = jnp.where(blane == n, jnp.sum(p, axis=1, keepdims=True), l_all)
    for i, p in enumerate(probs):
        oblk_ref[step * bps + i] = _dot_nt(p, blocks(v_refs, i).astype(BF16))
    km_ref[...] = km_all
    m_ref[...] = m_all
    l_ref[...] = l_all

    @pl.when(step == n_blocks // bps - 1)
    def _():
        gate = jnp.dot(qbd, km_ref[...], preferred_element_type=F32, precision=lax.Precision.HIGHEST)
        sel = _top_blocks(gate, n_blocks, axis=1) > 0.0
        s_own = _dot_nt(qb, kn_ref[0].astype(BF16))
        tq = lax.broadcasted_iota(jnp.int32, (R, ts), 0) % ts
        tk = lax.broadcasted_iota(jnp.int32, (R, ts), 1)
        s_own = jnp.where(tk <= tq, s_own, NEG_INF)
        mb = jnp.where(sel, m_ref[...], NEG_INF)
        m_all = jnp.maximum(jnp.max(s_own, axis=1, keepdims=True), jnp.max(mb, axis=1, keepdims=True))
        w = jnp.exp(mb - m_all)
        p_own = jnp.exp(s_own - m_all)
        l = jnp.sum(p_own, axis=1, keepdims=True) + jnp.sum(w * l_ref[...], axis=1, keepdims=True)
        o = jnp.dot(p_own.astype(BF16), vn_ref[0].astype(BF16), preferred_element_type=F32)
        for nb in range(n_blocks):
            o = o + w[:, nb:nb + 1] * oblk_ref[nb]
        o = jnp.where(own, o / l, 0.0)
        y = o[0:ts]
        for h in range(1, MB_HEADS):
            y = y + o[h * ts:(h + 1) * ts]
        y_ref[0] = y.astype(BF16)


def _moba_sample(q, kn, vn, cache_kt, cache_vt, page_table, layer):
    DB, ts, _ = q.shape
    n_pages = page_table.shape[1]
    ppb = MB_BLOCK // PAGE_SIZE
    n_blocks = n_pages // ppb
    R = MB_HEADS * ts
    tok = pl.BlockSpec((1, ts, MB_W), lambda b, n, pt: (b, 0, 0))
    bps = math.gcd(n_blocks, SAMPLE_BLOCKS_PER_STEP)
    pps = bps * ppb

    def page(i):
        return pl.BlockSpec((1, 1, MB_HEADS, MB_HD, PAGE_SIZE),
                            lambda b, n, pt: (layer, pt[b * n_pages + n * pps + i], 0, 0, 0))

    pages = [page(i) for i in range(pps)]
    grid_spec = pltpu.PrefetchScalarGridSpec(
        num_scalar_prefetch=1,
        grid=(DB, n_blocks // bps),
        in_specs=[tok, tok, tok] + pages + pages,
        out_specs=tok,
        scratch_shapes=[pltpu.VMEM((R, MB_W), F32),
                        pltpu.VMEM((n_blocks, R, MB_W), F32),
                        pltpu.VMEM((MB_W, n_blocks), F32),
                        pltpu.VMEM((R, n_blocks), F32),
                        pltpu.VMEM((R, n_blocks), F32)],
    )
    return pl.pallas_call(
        functools.partial(_moba_sample_kernel, n_blocks=n_blocks, ts=ts, bps=bps),
        grid_spec=grid_spec,
        out_shape=jax.ShapeDtypeStruct((DB, ts, MB_W), BF16),
        compiler_params=_cparams(("arbitrary", "arbitrary")),
        name="moba_sample",
    )(page_table.reshape(-1), q, kn, vn, *([cache_kt] * pps), *([cache_vt] * pps))


def _gla_kernel(qg_ref, kg_ref, vg_ref, rg_ref, lg_ref, s0_ref, gn_ref, y_ref, sfin_ref, st_ref, *,
                n_chunks, valid):
    t = pl.program_id(1)
    C = GLA_CHUNK

    @pl.when(t == 0)
    def _():
        for h in range(GLA_HEADS):
            st_ref[:, h * GLA_DK:(h + 1) * GLA_DK] = s0_ref[0, h].T

    ri = lax.broadcasted_iota(jnp.int32, (C, C), 0)
    ci = lax.broadcasted_iota(jnp.int32, (C, C), 1)
    tri = (ci <= ri).astype(BF16)
    m1 = (ri >= 32) & (ci < 32)
    m2 = (ri // 32 == ci // 32) & (ri % 32 >= 16) & (ci % 32 < 16)
    m3 = (ri // 16 == ci // 16) & (ci <= ri)
    grp = lax.broadcasted_iota(jnp.int32, (C, GLA_KW), 0)

    heads = [(slice(h * GLA_DK, (h + 1) * GLA_DK), slice(h * GLA_DV, (h + 1) * GLA_DV)) for h in range(GLA_HEADS)]
    sts = [st_ref[:, kl] for kl, _ in heads]
    for g0 in range(0, n_chunks, GLA_GROUP):
        cs = range(g0, min(g0 + GLA_GROUP, n_chunks))
        rows = {c: slice(c * C, c * C + valid) for c in cs}

        def chunk(ref, c, lanes=slice(None)):
            x = ref[0, rows[c], lanes]
            return x if valid == C else jnp.concatenate([x, jnp.zeros((C - valid, x.shape[1]), x.dtype)], axis=0)

        bcum = {}
        for c in cs:
            hi, mid, lo = _split3(chunk(lg_ref, c))
            bcum[c] = (jnp.dot(tri, hi, preferred_element_type=F32) + jnp.dot(tri, mid, preferred_element_type=F32)
                       + jnp.dot(tri, lo, preferred_element_type=F32))
        scaled = {}
        for c in cs:
            b = bcum[c]
            q = chunk(qg_ref, c) * (GLA_DK ** -0.5)
            k = chunk(kg_ref, c)
            brow = lambda i: jnp.broadcast_to(b[i:i + 1, :], (C, GLA_KW))
            ref1 = brow(31)
            ref2 = jnp.where(grp < 32, brow(15), brow(47))
            ref3 = jnp.where(grp < 16, brow(7),
                             jnp.where(grp < 32, brow(23), jnp.where(grp < 48, brow(39), brow(55))))
            b_last = b[C - 1:C, :]
            scaled[c] = dict(
                qs=(q * jnp.exp(b)).astype(BF16), ks=(k * jnp.exp(b_last - b)).astype(BF16),
                q1=(q * jnp.exp(jnp.minimum(b - ref1, 0.0))).astype(BF16),
                k1=(k * jnp.exp(jnp.minimum(ref1 - b, 0.0))).astype(BF16),
                q2=(q * jnp.exp(jnp.minimum(b - ref2, 0.0))).astype(BF16),
                k2=(k * jnp.exp(jnp.minimum(ref2 - b, 0.0))).astype(BF16),
                q3=(q * jnp.exp(b - ref3)).astype(BF16), k3=(k * jnp.exp(ref3 - b)).astype(BF16),
                decay=jnp.exp(b_last))
        levels = {}
        for c in cs:
            x = scaled[c]
            for h, (kl, _) in enumerate(heads):
                levels[c, h] = (_dot_nt(x["q1"][:, kl], x["k1"][:, kl]), _dot_nt(x["q2"][:, kl], x["k2"][:, kl]),
                                _dot_nt(x["q3"][:, kl], x["k3"][:, kl]))
        intra, update = {}, {}
        for c in cs:
            for h, (kl, vl) in enumerate(heads):
                l1, l2, l3 = levels[c, h]
                a = jnp.where(m1, l1, 0.0) + jnp.where(m2, l2, 0.0) + jnp.where(m3, l3, 0.0)
                vf = chunk(vg_ref, c, vl)
                intra[c, h] = jnp.dot(a.astype(BF16), vf.astype(BF16), preferred_element_type=F32)
                update[c, h] = jnp.dot(vf.T.astype(BF16), scaled[c]["ks"][:, kl], preferred_element_type=F32)
        for c in cs:
            for h, (kl, vl) in enumerate(heads):
                o = _dot_nt(scaled[c]["qs"][:, kl], sts[h].astype(BF16)) + intra[c, h]
                sts[h] = sts[h] * scaled[c]["decay"][:, kl] + update[c, h]
                y_ref[0, rows[c], vl] = (_rms(o[:valid], gn_ref[:, vl]) * rg_ref[0, rows[c], vl]).astype(BF16)
    for (kl, _), st in zip(heads, sts):
        st_ref[:, kl] = st

    @pl.when(t == pl.num_programs(1) - 1)
    def _():
        for h in range(GLA_HEADS):
            sfin_ref[0, h] = st_ref[:, h * GLA_DK:(h + 1) * GLA_DK].T


def _gla(qg, kg, vg, rg, lg, s0, gn):
    Bn, T, _ = qg.shape
    assert T < GLA_CHUNK or T % GLA_CHUNK == 0
    TT = min(T, 512)
    kspec = pl.BlockSpec((1, TT, GLA_KW), lambda b, t: (b, t, 0))
    vspec = pl.BlockSpec((1, TT, GLA_VW), lambda b, t: (b, t, 0))
    sspec = pl.BlockSpec((1, GLA_HEADS, GLA_DK, GLA_DV), lambda b, t: (b, 0, 0, 0))
    return pl.pallas_call(
        functools.partial(_gla_kernel, n_chunks=max(1, TT // GLA_CHUNK), valid=min(TT, GLA_CHUNK)),
        grid=(Bn, T // TT),
        in_specs=[kspec, kspec, vspec, vspec, kspec, sspec, _const_spec((1, GLA_VW))],
        out_specs=[vspec, sspec],
        out_shape=[jax.ShapeDtypeStruct((Bn, T, GLA_VW), BF16),
                   jax.ShapeDtypeStruct((Bn, GLA_HEADS, GLA_DK, GLA_DV), F32)],
        scratch_shapes=[pltpu.VMEM((GLA_DV, GLA_KW), F32)],
        compiler_params=_cparams(("arbitrary", "arbitrary")),
        name="gla",
    )(qg, kg, vg, rg, lg, s0, gn)


def _merge_ffn_kernel(x_ref, yp_ref, ym_ref, yg_ref, sg_ref, wp_ref, wmb_ref, wgl_ref, wo_ref,
                      gpost_ref, gfpre_ref, gfpost_ref, w1_ref, w2_ref, o_ref):
    tm = x_ref.shape[0]
    subs = [slice(r, r + FFN_SUB_ROWS) for r in range(0, tm, FFN_SUB_ROWS)] if tm > FFN_SUB_ROWS else [slice(0, tm)]

    def dot(a, w):
        return jnp.dot(a.astype(BF16), w, preferred_element_type=F32)

    branches = [[dot(y_ref[r, :], w_ref[...]) for y_ref, w_ref in
                 ((yp_ref, wp_ref), (ym_ref, wmb_ref), (yg_ref, wgl_ref))] for r in subs]
    merged = [sum(sg_ref[r, 1024 * j:1024 * (j + 1)] * b for j, b in enumerate(bs))
              for r, bs in zip(subs, branches)]
    z = [dot(m, wo_ref[...]) for m in merged]
    x1 = [x_ref[r, :] + _rms(zi, gpost_ref[...]) for r, zi in zip(subs, z)]
    hb = [_rms(xi, gfpre_ref[...]).astype(BF16) for xi in x1]
    FC = 1024
    f = [jnp.zeros(xi.shape, F32) for xi in x1]

    def second(i, c, a):
        a = jnp.maximum(a, 0.0)
        f[i] = f[i] + dot(a * a, w2_ref[c * FC:(c + 1) * FC, :])

    pending = None
    for c in range(D_FF // FC):
        for i in range(len(subs)):
            a = jnp.dot(hb[i], w1_ref[:, c * FC:(c + 1) * FC], preferred_element_type=F32)
            if pending is not None:
                second(*pending)
            pending = (i, c, a)
    second(*pending)
    for i, r in enumerate(subs):
        o_ref[r, :] = x1[i] + _rms(f[i], gfpost_ref[...])


def _merge_ffn(x, yp, ym, yg, sg, layer, wp, wmb, wgl, wo, gpost, gfpre, gfpost, w1, w2):
    M = x.shape[0]
    tm = min(FFN_ROW_TILE, M)
    row = lambda w: pl.BlockSpec((tm, w), lambda i: (i, 0))
    vec = _const_spec((1, D_MODEL))
    return pl.pallas_call(
        _merge_ffn_kernel,
        grid=(M // tm,),
        in_specs=[row(D_MODEL), row(POOL_W), row(MB_W), row(GLA_VW), row(3 * D_MODEL),
                  *(_layer_spec(w, layer) for w in (wp, wmb, wgl, wo)),
                  vec, vec, vec, _layer_spec(w1, layer), _layer_spec(w2, layer)],
        out_specs=row(D_MODEL),
        out_shape=jax.ShapeDtypeStruct((M, D_MODEL), F32),
        compiler_params=_cparams(("arbitrary",)),
        name="merge_ffn",
    )(x, yp, ym, yg, sg, wp, wmb, wgl, wo, gpost, gfpre, gfpost, w1, w2)


def _layer(x, Bn, T, tabs, hist16, pos0, s0, sample_attend, layer, lw, prompt_state=None):
    (w_in, wg, wa2, ba, gn, wgrp, pscale, wp, wmb, wgl, wo, gpre, gpost, gfpre, gfpost, w1, w2) = lw
    seq = lambda a: a.reshape(Bn, T, a.shape[-1])
    if sample_attend is None:
        u, q, kb, km, k, v, qg, kg, vg, rg, lg, sg = _in_proj(x, gpre, w_in, layer, wg, wa2, ba, tabs,
                                                             prompt_state=(T,) + tuple(prompt_state))
        ym = _moba_prompt(seq(q), km.reshape(Bn, T // MB_BLOCK, MB_W), seq(kb), v, layer)
    else:
        u, q, k, v, qg, kg, vg, rg, lg, sg = _in_proj(x, gpre, w_in, layer, wg, wa2, ba, tabs)
        ym = sample_attend(seq(q), seq(k), seq(v))
        k, v = (a.reshape(Bn, T, MB_HEADS, MB_HD) for a in (k, v))
    u3 = seq(u)
    yp = _pool(u3, hist16, wgrp, pscale, pos0)
    yg, s_new = _gla(seq(qg), seq(kg), seq(vg), seq(rg), seq(lg), s0, gn)
    yg = yg.reshape(Bn * T, GLA_VW)
    x = _merge_ffn(x, yp.reshape(Bn * T, POOL_W), ym.reshape(Bn * T, MB_W), yg, sg, layer,
                   wp, wmb, wgl, wo, gpost, gfpre, gfpost, w1, w2)
    new_hist = jnp.concatenate([hist16[:, 1:], u3], axis=1)[:, -POOL_HIST:]
    return x, k, v, new_hist, s_new


def kernel(x_prompt, x_sample, cache_k, cache_v, state_pool, state_gla, page_table, w_in, w_gla_a2, b_gla_a,
           gla_norm, w_pool_grp, pool_scale, w_pool_out, w_mb_out, w_gla_out, w_o, norm_mix_pre,
           norm_mix_post, norm_ffn_pre, norm_ffn_post, w_ff1, w_ff2):
    B, T, D = x_prompt.shape
    DB, TS, _ = x_sample.shape
    depth = w_in.shape[0]
    n_pages = page_table.shape[1]
    past_len = n_pages * PAGE_SIZE
    assert D == D_MODEL and T % MB_BLOCK == 0 and past_len % MB_BLOCK == 0
    assert past_len // MB_BLOCK >= MB_TOPK and (B * T) % ROW_TILE == 0 and MB_BLOCK == 2 * PAGE_SIZE

    tabs_p = _rope_tables(jnp.arange(T))
    reps = max(1, min(ROW_TILE, DB * TS) // TS)
    tabs_s = tuple(jnp.tile(t, (reps, 1)) for t in _rope_tables(past_len + jnp.arange(TS)))
    ckt = jnp.transpose(cache_k, (0, 1, 3, 4, 2))
    cvt = jnp.transpose(cache_v, (0, 1, 3, 4, 2))
    zrow = lambda n: jnp.zeros((n, 1, POOL_W), F32)
    hist_p = jnp.zeros((B, 16, POOL_W), F32)
    s0_p = jnp.zeros((B, GLA_HEADS, GLA_DK, GLA_DV), F32)

    hp = x_prompt.reshape(B * T, D)
    hs = x_sample.reshape(DB * TS, D)
    outs_p, outs_s = [], []
    kt_all = vt_all = None
    bf = lambda a: a.astype(BF16)
    vec = lambda a: a.reshape(1, -1)
    w_in_b, wp_b, wmb_b, wgl_b, wo_b, w1_b, w2_b = (bf(w) for w in (w_in, w_pool_out, w_mb_out, w_gla_out,
                                                                    w_o, w_ff1, w_ff2))
    for l in range(depth):
        lw = (w_in_b, w_in_b[l][:, W_MAIN + GLA_RANK:],
              bf(w_gla_a2[l]), vec(b_gla_a[l]), vec(gla_norm[l]), bf(w_pool_grp[l]), vec(pool_scale[l]),
              wp_b, wmb_b, wgl_b, wo_b,
              vec(norm_mix_pre[l]), vec(norm_mix_post[l]), vec(norm_ffn_pre[l]), vec(norm_ffn_post[l]),
              w1_b, w2_b)
        wvt = w_in_b[l][:, COL_V:COL_QG].T
        hp, kt_all, vt_all, *op = _layer(hp, B, T, tabs_p, hist_p, 0, s0_p, None, l, lw,
                                         prompt_state=(depth, l, wvt, kt_all, vt_all))
        hist_s = jnp.concatenate([zrow(DB), state_pool[l]], axis=1)
        attend_s = functools.partial(_moba_sample, cache_kt=ckt, cache_vt=cvt, page_table=page_table, layer=l)
        hs, *os_ = _layer(hs, DB, TS, tabs_s, hist_s, past_len, state_gla[l], attend_s, l, lw)
        outs_p.append(op)
        outs_s.append(os_)
    stack = lambda outs, i: jnp.stack([o[i] for o in outs])
    untr = lambda a: jnp.transpose(a.reshape(depth, B, MB_HEADS, MB_HD, T), (0, 1, 4, 2, 3))
    return (hp.reshape(B, T, D), hs.reshape(DB, TS, D),
            untr(kt_all), untr(vt_all), stack(outs_p, 0), stack(outs_p, 1),
            stack(outs_s, 0), stack(outs_s, 1), stack(outs_s, 2), stack(outs_s, 3))
```

```python
import functools
import math

import jax
import jax.numpy as jnp
from jax import lax
from jax.experimental import pallas as pl
from jax.experimental.pallas import tpu as pltpu

F32 = jnp.float32
BF16 = jnp.bfloat16

D_MODEL = 1024
PAGE_SIZE = 128
POOL_GC = 128
POOL_W = 512
POOL_WINDOWS = (2, 4, 8, 16)
POOL_HIST = 15
MB_HEADS = 8
MB_HD = 64
MB_W = 512
MB_BLOCK = 256
MB_TOPK = 3
ROPE_THETA = 500000.0
ROPE_DIM = 16
GLA_HEADS = 4
GLA_DK = 64
GLA_DV = 128
GLA_KW = 256
GLA_VW = 512
GLA_RANK = 16
GLA_NORMALIZER = 16.0
GLA_CHUNK = 64
D_FF = 4096
EPS = 1e-6
COL_U = 0
COL_Q = COL_U + POOL_W
COL_K = COL_Q + MB_W
COL_V = COL_K + MB_W
COL_QG = COL_V + MB_W
COL_KG = COL_QG + GLA_KW
COL_VG = COL_KG + GLA_KW
COL_RG = COL_VG + GLA_VW
W_MAIN = COL_RG + GLA_VW
NEG_INF = float("-inf")
LOG2E = 1.4426950408889634

VMEM_LIMIT_BYTES = 56 * 1024 * 1024
ROW_TILE = 512
FFN_ROW_TILE = 512
FFN_SUB_ROWS = 256
SAMPLE_BLOCKS_PER_STEP = 16
GLA_SEQ_TILE = 512
PROMPT_PIPE = 2
GLA_GROUP = 8


def _cparams(sem):
    return pltpu.CompilerParams(dimension_semantics=sem, vmem_limit_bytes=VMEM_LIMIT_BYTES)


def _const_spec(shape):
    nd = len(shape)
    return pl.BlockSpec(shape, lambda *_: (0,) * nd, pipeline_mode=pl.Buffered(1))


def _layer_spec(stacked, layer, cols=None, col_block=0):
    _, k, n = stacked.shape
    return pl.BlockSpec((None, k, n if cols is None else cols), lambda *_: (layer, 0, col_block),
                        pipeline_mode=pl.Buffered(1))


def _rms(x, g):
    ms = jnp.mean(x * x, axis=-1, keepdims=True)
    return x * lax.rsqrt(ms + EPS) * g


def _sigmoid(x):
    return 0.5 * jnp.tanh(0.5 * x) + 0.5


def _split3(x):
    hi = x.astype(BF16)
    r = x - hi.astype(F32)
    mid = r.astype(BF16)
    lo = (r - mid.astype(F32)).astype(BF16)
    return hi, mid, lo


def _in_proj_kernel(*refs, prompt, first_layer):
    x_ref, gpre_ref, wm_ref, wa_ref, wg_ref, wa2_ref, ba_ref, cos_ref, sa_ref, sb_ref = refs[:10]
    n_in = 10 + (0 if not prompt else 1 if first_layer else 3)
    wvt_ref = refs[10] if prompt else None
    outs = refs[n_in:]
    u_ref, q_ref = outs[:2]
    qg_ref, kg_ref, vg_ref, rg_ref, lg_ref, sg_ref = outs[-6:]
    h = _rms(x_ref[...], gpre_ref[...]).astype(BF16)

    def proj(w_ref, a, b):
        return jnp.dot(h, w_ref[:, a:b], preferred_element_type=F32)

    cos, sa, sb = cos_ref[...], sa_ref[...], sb_ref[...]

    def rope(col0):
        half = ROPE_DIM // 2
        x = proj(wm_ref, col0, col0 + MB_W)
        for j in range(MB_W // 128):
            xj = x[:, 128 * j:128 * (j + 1)]
            yield j, xj * cos + pltpu.roll(xj, 128 - half, 1) * sa + pltpu.roll(xj, half, 1) * sb

    for j in range(3):
        sg_ref[:, 1024 * j:1024 * (j + 1)] = _sigmoid(proj(wg_ref, 1024 * j, 1024 * (j + 1))).astype(BF16)
    r = proj(wm_ref, COL_RG, W_MAIN)
    rg_ref[...] = r * _sigmoid(r)
    a = jnp.dot(h, wa_ref[:, :GLA_RANK], preferred_element_type=F32)
    xg = jnp.dot(a.astype(BF16), wa2_ref[...], preferred_element_type=F32) + ba_ref[...]
    lg_ref[...] = (jnp.minimum(xg, 0.0) - jnp.log1p(jnp.exp(-jnp.abs(xg)))) * (1.0 / GLA_NORMALIZER)
    u_ref[...] = proj(wm_ref, COL_U, COL_Q)
    for j, rj in rope(COL_Q):
        q_ref[:, 128 * j:128 * (j + 1)] = rj
    if prompt:
        kb_ref, km_ref, kt_ref, vt_ref = outs[2:6]
        n_blk = km_ref.shape[0]
        col_sums = [[] for _ in range(n_blk)]
        for j, rj in rope(COL_K):
            kb_ref[:, 128 * j:128 * (j + 1)] = rj.astype(BF16)
            kt_ref[0, 0, 128 * j:128 * (j + 1), :] = rj.T
            for i in range(n_blk):
                col_sums[i].append(jnp.sum(rj[i * MB_BLOCK:(i + 1) * MB_BLOCK], axis=0, keepdims=True))
        for i in range(n_blk):
            km_ref[i] = jnp.concatenate(col_sums[i], axis=1) * (1.0 / MB_BLOCK)
        vt_ref[0, 0] = _dot_nt(wvt_ref[...], h)
        for d in range(1, kt_ref.shape[0]):
            kt_ref[d] = jnp.zeros(kt_ref.shape[1:], F32)
            vt_ref[d] = jnp.zeros(vt_ref.shape[1:], F32)
    else:
        k_ref, v_ref = outs[2:4]
        for j, rj in rope(COL_K):
            k_ref[:, 128 * j:128 * (j + 1)] = rj
        v_ref[...] = proj(wm_ref, COL_V, COL_QG)
    qg_ref[...] = proj(wm_ref, COL_QG, COL_KG)
    kg_ref[...] = proj(wm_ref, COL_KG, COL_VG)
    vg_ref[...] = proj(wm_ref, COL_VG, COL_RG)


def _in_proj(x, gpre, w_in, layer, wg, wa2, ba, tabs, prompt_state=None):
    M = x.shape[0]
    tm = min(ROW_TILE, M)
    period = tabs[0].shape[0] // tm
    row = lambda w: pl.BlockSpec((tm, w), lambda i: (i, 0))
    tab = pl.BlockSpec((tm, 128), lambda i: (i % period, 0))
    f32 = lambda w: jax.ShapeDtypeStruct((M, w), F32)
    tail_w = (GLA_KW, GLA_KW, GLA_VW, GLA_VW, GLA_KW, 3 * D_MODEL)
    assert W_MAIN % 128 == 0
    in_specs = [row(D_MODEL), _const_spec((1, D_MODEL)), _layer_spec(w_in, layer, cols=W_MAIN),
                _layer_spec(w_in, layer, cols=128, col_block=W_MAIN // 128),
                _const_spec(wg.shape), _const_spec(wa2.shape), _const_spec((1, GLA_KW)), tab, tab, tab]
    args = [x, gpre, w_in, w_in, wg, wa2, ba, *tabs]
    aliases = {}
    if prompt_state is None:
        out_specs = [row(POOL_W), row(MB_W), row(MB_W), row(MB_W)]
        out_shape = [f32(POOL_W), f32(MB_W), f32(MB_W), f32(MB_W)]
        first = False
    else:
        seq_len, depth, layer, wvt, kt_all, vt_all = prompt_state
        assert tm % MB_BLOCK == 0 and seq_len % tm == 0
        tps = seq_len // tm
        first = layer == 0
        stacked = jax.ShapeDtypeStruct((depth, M // seq_len, MB_W, seq_len), F32)
        in_specs.append(_const_spec(wvt.shape))
        args.append(wvt)
        if first:
            st_spec = pl.BlockSpec((depth, 1, MB_W, tm), lambda i: (0, i // tps, 0, i % tps))
        else:
            st_spec = pl.BlockSpec((1, 1, MB_W, tm), lambda i: (layer, i // tps, 0, i % tps))
            in_specs += [pl.BlockSpec(memory_space=pl.ANY)] * 2
            args += [kt_all, vt_all]
            aliases = {11: 4, 12: 5}
        out_specs = [row(POOL_W), row(MB_W), row(MB_W),
                     pl.BlockSpec((tm // MB_BLOCK, 1, MB_W), lambda i: (i, 0, 0)),
                     st_spec, st_spec]
        out_shape = [f32(POOL_W), f32(MB_W), jax.ShapeDtypeStruct((M, MB_W), BF16),
                     jax.ShapeDtypeStruct((M // MB_BLOCK, 1, MB_W), F32), stacked, stacked]
    return pl.pallas_call(
        functools.partial(_in_proj_kernel, prompt=prompt_state is not None, first_layer=first),
        grid=(M // tm,),
        in_specs=in_specs,
        out_specs=out_specs + [row(w) for w in tail_w],
        out_shape=out_shape + [f32(w) for w in tail_w[:-1]] + [jax.ShapeDtypeStruct((M, tail_w[-1]), BF16)],
        input_output_aliases=aliases,
        compiler_params=_cparams(("arbitrary",)),
        name="in_proj",
    )(*args)


def _rope_tables(pos):
    half = ROPE_DIM // 2
    inv = ROPE_THETA ** (-jnp.arange(half, dtype=F32) / half)
    ang = pos.astype(F32)[:, None] * inv[None, :]
    cos, sin = jnp.cos(ang), jnp.sin(ang)
    n = pos.shape[0]
    zeros = jnp.zeros((n, MB_HD - ROPE_DIM), F32)
    zh = jnp.zeros((n, half), F32)
    c = jnp.concatenate([cos, cos, zeros + 1.0], axis=1)
    sa = jnp.concatenate([-sin, zh, zeros], axis=1)
    sb = jnp.concatenate([zh, sin, zeros], axis=1)
    two = lambda t: jnp.concatenate([t, t], axis=1)
    return two(c), two(sa), two(sb)


def _pool_kernel(u_ref, hist_ref, wgrp_ref, scale_ref, y_ref, ext_ref, *, pos0, T, R, nb):
    for s in range(nb):
        ext_ref[s, 0:16, :] = hist_ref[s]
        ext_ref[s, 16:16 + T, :] = u_ref[s]
    for c in range(T // R):
        r0 = 16 + c * R
        posp1 = lax.broadcasted_iota(jnp.int32, (R, 1), 0) + (pos0 + c * R + 1)
        for g, w in enumerate(POOL_WINDOWS):
            lanes = slice(g * POOL_GC, (g + 1) * POOL_GC)
            cnt = jnp.minimum(posp1, w).astype(F32)
            ds = []
            for s in range(nb):
                cur = ext_ref[s, r0:r0 + R, lanes]
                acc = cur
                for j in range(1, w):
                    acc = acc + ext_ref[s, r0 - j:r0 - j + R, lanes]
                ds.append(acc / cnt - cur)
            d = ds[0] if nb == 1 else jnp.concatenate(ds, axis=0)
            yg = jnp.dot(d.astype(BF16), wgrp_ref[g], preferred_element_type=F32) * scale_ref[:, lanes]
            for s in range(nb):
                y_ref[s, c * R:(c + 1) * R, lanes] = yg[s * R:(s + 1) * R].astype(BF16)


def _pool(u, hist16, wgrp, scale, pos0):
    Bn, T, _ = u.shape
    R = min(T, 256)
    nb = Bn if T < MB_BLOCK else 1
    seq = pl.BlockSpec((nb, T, POOL_W), lambda b: (b, 0, 0))
    return pl.pallas_call(
        functools.partial(_pool_kernel, pos0=pos0, T=T, R=R, nb=nb),
        grid=(Bn // nb,),
        in_specs=[seq, pl.BlockSpec((nb, 16, POOL_W), lambda b: (b, 0, 0)),
                  _const_spec(wgrp.shape), _const_spec((1, POOL_W))],
        out_specs=seq,
        out_shape=jax.ShapeDtypeStruct((Bn, T, POOL_W), BF16),
        scratch_shapes=[pltpu.VMEM((nb, 16 + T, POOL_W), F32)],
        compiler_params=_cparams(("arbitrary",)),
        name="pool",
    )(u, hist16, wgrp, scale)


def _top_blocks(gate, n_valid, axis):
    n_blocks = gate.shape[axis]
    iota = lax.broadcasted_iota(jnp.int32, gate.shape, axis)
    g = jnp.where(iota < n_valid, gate, NEG_INF)
    sel = jnp.zeros(gate.shape, F32)
    for _ in range(MB_TOPK):
        m = jnp.max(g, axis=axis, keepdims=True)
        idx = jnp.min(jnp.where(g == m, iota, n_blocks), axis=axis, keepdims=True)
        pick = (iota == idx) & (m > NEG_INF)
        sel = jnp.where(pick, 1.0, sel)
        g = jnp.where(pick, NEG_INF, g)
    return sel


def _col_max(s):
    r, c = s.shape
    return jnp.max(jnp.max(s.reshape(r // 8, 8, c), axis=0), axis=0, keepdims=True)


def _dot_nt(a, b, precision=None):
    return lax.dot_general(a, b, (((1,), (1,)), ((), ())), preferred_element_type=F32, precision=precision)


def _moba_prompt_kernel(q_ref, km_ref, k_ref, vt_ref, y_ref, s_ref, vtb_ref):
    B = MB_BLOCK
    n_blocks = km_ref.shape[1]
    qlane = lax.broadcasted_iota(jnp.int32, (B, 128), 1) // MB_HD
    kmlane = lax.broadcasted_iota(jnp.int32, (n_blocks, 128), 1) // MB_HD
    causal = lax.broadcasted_iota(jnp.int32, (B, B), 0) <= lax.broadcasted_iota(jnp.int32, (B, B), 1)
    drow = lax.broadcasted_iota(jnp.int32, (128, B), 0) // MB_HD
    vtb_ref[0:128, :] = vt_ref[0, 0].astype(BF16)
    vtb_ref[128:, :] = jnp.ones((vtb_ref.shape[0] - 128, vtb_ref.shape[1]), BF16)

    def scores_stage(c):
        q = q_ref[0, c * B:(c + 1) * B, :]
        qh = [jnp.where(qlane == hh, q, 0.0) for hh in range(2)]
        qs = [(x * (MB_HD ** -0.5 * LOG2E)).astype(BF16) for x in qh]
        raw = {(hh, j): _dot_nt(k_ref[0, j * B:(j + 1) * B, :], qs[hh])
               for hh in range(2) for j in range(c + 1)}
        maxima = []
        for hh in range(2):
            if c > 0:
                gate_t = _dot_nt(jnp.where(kmlane == hh, km_ref[0], 0.0), qh[hh], precision=lax.Precision.HIGHEST)
                sel = jnp.where(_top_blocks(gate_t, c, axis=0) > 0.0, 0.0, NEG_INF)
            m = None
            for j in range(c + 1):
                s = jnp.where(causal, raw[hh, j], NEG_INF) if j == c else raw[hh, j] + sel[j:j + 1, :]
                s_ref[2 * (c % (PROMPT_PIPE + 1)) + hh, j] = s
                mj = _col_max(s)
                m = mj if m is None else jnp.maximum(m, mj)
            maxima.append(m)
        return maxima

    def values_stage(c, maxima):
        outs = []
        for hh in range(2):
            probs = [jnp.exp2(s_ref[2 * (c % (PROMPT_PIPE + 1)) + hh, j] - maxima[hh]).astype(BF16)
                     for j in range(c + 1)]
            acc = jnp.dot(vtb_ref[:, :(c + 1) * B], jnp.concatenate(probs, axis=0), preferred_element_type=F32)
            outs.append(acc[:128] / acc[128:129])
        y_ref[0, c * B:(c + 1) * B, :] = jnp.where(drow == 0, outs[0], outs[1]).T.astype(BF16)

    pending = []
    for c in range(n_blocks):
        pending.append((c, scores_stage(c)))
        if len(pending) > PROMPT_PIPE:
            values_stage(*pending.pop(0))
    for item in pending:
        values_stage(*item)


def _moba_prompt(q, km, kb, vt_all, layer):
    Bn, T, _ = q.shape
    n_blocks = T // MB_BLOCK
    seq = pl.BlockSpec((1, T, 128), lambda b, hp: (b, 0, hp))
    return pl.pallas_call(
        _moba_prompt_kernel,
        grid=(Bn, MB_W // 128),
        in_specs=[seq,
                  pl.BlockSpec((1, n_blocks, 128), lambda b, hp: (b, 0, hp)),
                  seq,
                  pl.BlockSpec((1, 1, 128, T), lambda b, hp: (layer, b, hp, 0))],
        out_specs=seq,
        out_shape=jax.ShapeDtypeStruct((Bn, T, MB_W), BF16),
        scratch_shapes=[pltpu.VMEM((2 * (PROMPT_PIPE + 1), n_blocks, MB_BLOCK, MB_BLOCK), F32), pltpu.VMEM((128 + 16, T), BF16)],
        compiler_params=_cparams(("arbitrary", "arbitrary")),
        name="moba_prompt",
    )(q, km, kb, vt_all)


def _moba_sample_kernel(pt_ref, q_ref, kn_ref, vn_ref, *refs, n_blocks, ts, bps):
    ppb = MB_BLOCK // PAGE_SIZE
    k_refs = refs[:bps * ppb]
    v_refs = refs[bps * ppb:2 * bps * ppb]
    y_ref, qbd_ref, oblk_ref, km_ref, m_ref, l_ref = refs[2 * bps * ppb:]
    step = pl.program_id(1)
    R = MB_HEADS * ts
    own = (lax.broadcasted_iota(jnp.int32, (R, MB_W), 0) // ts
           == lax.broadcasted_iota(jnp.int32, (R, MB_W), 1) // MB_HD)
    blane = lax.broadcasted_iota(jnp.int32, (R, n_blocks), 1)
    kml = lax.broadcasted_iota(jnp.int32, (MB_W, n_blocks), 1)

    @pl.when(step == 0)
    def _():
        for ref in (m_ref, l_ref, km_ref):
            ref[...] = jnp.zeros(ref.shape, F32)
        qbd_ref[...] = jnp.where(own, jnp.concatenate([q_ref[0]] * MB_HEADS, axis=0), 0.0)

    qbd = qbd_ref[...]
    qb = (qbd * (MB_HD ** -0.5)).astype(BF16)
    page = lambda ref: ref[0, 0].reshape(MB_W, PAGE_SIZE)
    m_all, l_all, km_all = m_ref[...], l_ref[...], km_ref[...]
    blocks = lambda refs, i: jnp.concatenate([page(refs[ppb * i + j]) for j in range(ppb)], axis=1)
    scores, probs = [], []
    for i in range(bps):
        kblk = blocks(k_refs, i)
        scores.append(jnp.dot(qb, kblk.astype(BF16), preferred_element_type=F32))
        kmean = jnp.sum(kblk, axis=1, keepdims=True) * (1.0 / MB_BLOCK)
        km_all = jnp.where(kml == step * bps + i, kmean, km_all)
    for i, s in enumerate(scores):
        n = step * bps + i
        m_n = jnp.max(s, axis=1, keepdims=True)
        p = jnp.exp(s - m_n)
        probs.append(p.astype(BF16))
        m_all = jnp.where(blane == n, m_n, m_all)
        l_all = jnp.where(blane == n, jnp.sum(p, axis=1, keepdims=True), l_all)
    for i, p in enumerate(probs):
        oblk_ref[step * bps + i] = _dot_nt(p, blocks(v_refs, i).astype(BF16))
    km_ref[...] = km_all
    m_ref[...] = m_all
    l_ref[...] = l_all

    @pl.when(step == n_blocks // bps - 1)
    def _():
        gate = jnp.dot(qbd, km_ref[...], preferred_element_type=F32, precision=lax.Precision.HIGHEST)
        sel = _top_blocks(gate, n_blocks, axis=1) > 0.0
        s_own = _dot_nt(qb, kn_ref[0].astype(BF16))
        tq = lax.broadcasted_iota(jnp.int32, (R, ts), 0) % ts
        tk = lax.broadcasted_iota(jnp.int32, (R, ts), 1)
        s_own = jnp.where(tk <= tq, s_own, NEG_INF)
        mb = jnp.where(sel, m_ref[...], NEG_INF)
        m_all = jnp.maximum(jnp.max(s_own, axis=1, keepdims=True), jnp.max(mb, axis=1, keepdims=True))
        w = jnp.exp(mb - m_all)
        p_own = jnp.exp(s_own - m_all)
        l = jnp.sum(p_own, axis=1, keepdims=True) + jnp.sum(w * l_ref[...], axis=1, keepdims=True)
        o = jnp.dot(p_own.astype(BF16), vn_ref[0].astype(BF16), preferred_element_type=F32)
        for nb in range(n_blocks):
            o = o + w[:, nb:nb + 1] * oblk_ref[nb]
        o = jnp.where(own, o / l, 0.0)
        y = o[0:ts]
        for h in range(1, MB_HEADS):
            y = y + o[h * ts:(h + 1) * ts]
        y_ref[0] = y.astype(BF16)


def _moba_sample(q, kn, vn, cache_kt, cache_vt, page_table, layer):
    DB, ts, _ = q.shape
    n_pages = page_table.shape[1]
    ppb = MB_BLOCK // PAGE_SIZE
    n_blocks = n_pages // ppb
    R = MB_HEADS * ts
    tok = pl.BlockSpec((1, ts, MB_W), lambda b, n, pt: (b, 0, 0))
    bps = math.gcd(n_blocks, SAMPLE_BLOCKS_PER_STEP)
    pps = bps * ppb

    def page(i):
        return pl.BlockSpec((1, 1, MB_HEADS, MB_HD, PAGE_SIZE),
                            lambda b, n, pt: (layer, pt[b * n_pages + n * pps + i], 0, 0, 0))

    pages = [page(i) for i in range(pps)]
    grid_spec = pltpu.PrefetchScalarGridSpec(
        num_scalar_prefetch=1,
        grid=(DB, n_blocks // bps),
        in_specs=[tok, tok, tok] + pages + pages,
        out_specs=tok,
        scratch_shapes=[pltpu.VMEM((R, MB_W), F32),
                        pltpu.VMEM((n_blocks, R, MB_W), F32),
                        pltpu.VMEM((MB_W, n_blocks), F32),
                        pltpu.VMEM((R, n_blocks), F32),
                        pltpu.VMEM((R, n_blocks), F32)],
    )
    return pl.pallas_call(
        functools.partial(_moba_sample_kernel, n_blocks=n_blocks, ts=ts, bps=bps),
        grid_spec=grid_spec,
        out_shape=jax.ShapeDtypeStruct((DB, ts, MB_W), BF16),
        compiler_params=_cparams(("arbitrary", "arbitrary")),
        name="moba_sample",
    )(page_table.reshape(-1), q, kn, vn, *([cache_kt] * pps), *([cache_vt] * pps))


def _gla_kernel(qg_ref, kg_ref, vg_ref, rg_ref, lg_ref, s0_ref, gn_ref, y_ref, sfin_ref, st_ref, *,
                n_chunks, valid):
    t = pl.program_id(1)
    C = GLA_CHUNK

    @pl.when(t == 0)
    def _():
        for h in range(GLA_HEADS):
            st_ref[:, h * GLA_DK:(h + 1) * GLA_DK] = s0_ref[0, h].T

    ri = lax.broadcasted_iota(jnp.int32, (C, C), 0)
    ci = lax.broadcasted_iota(jnp.int32, (C, C), 1)
    tri = (ci <= ri).astype(BF16)
    m1 = (ri >= 32) & (ci < 32)
    m2 = (ri // 32 == ci // 32) & (ri % 32 >= 16) & (ci % 32 < 16)
    m3 = (ri // 16 == ci // 16) & (ci <= ri)
    grp = lax.broadcasted_iota(jnp.int32, (C, GLA_KW), 0)

    heads = [(slice(h * GLA_DK, (h + 1) * GLA_DK), slice(h * GLA_DV, (h + 1) * GLA_DV)) for h in range(GLA_HEADS)]
    sts = [st_ref[:, kl] for kl, _ in heads]
    for g0 in range(0, n_chunks, GLA_GROUP):
        cs = range(g0, min(g0 + GLA_GROUP, n_chunks))
        rows = {c: slice(c * C, c * C + valid) for c in cs}

        def chunk(ref, c, lanes=slice(None)):
            x = ref[0, rows[c], lanes]
            return x if valid == C else jnp.concatenate([x, jnp.zeros((C - valid, x.shape[1]), x.dtype)], axis=0)

        bcum = {}
        for c in cs:
            hi, mid, lo = _split3(chunk(lg_ref, c))
            bcum[c] = (jnp.dot(tri, hi, preferred_element_type=F32) + jnp.dot(tri, mid, preferred_element_type=F32)
                       + jnp.dot(tri, lo, preferred_element_type=F32))
        scaled = {}
        for c in cs:
            b = bcum[c]
            q = chunk(qg_ref, c) * (GLA_DK ** -0.5)
            k = chunk(kg_ref, c)
            brow = lambda i: jnp.broadcast_to(b[i:i + 1, :], (C, GLA_KW))
            ref1 = brow(31)
            ref2 = jnp.where(grp < 32, brow(15), brow(47))
            ref3 = jnp.where(grp < 16, brow(7),
                             jnp.where(grp < 32, brow(23), jnp.where(grp < 48, brow(39), brow(55))))
            b_last = b[C - 1:C, :]
            scaled[c] = dict(
                qs=(q * jnp.exp(b)).astype(BF16), ks=(k * jnp.exp(b_last - b)).astype(BF16),
                q1=(q * jnp.exp(jnp.minimum(b - ref1, 0.0))).astype(BF16),
                k1=(k * jnp.exp(jnp.minimum(ref1 - b, 0.0))).astype(BF16),
                q2=(q * jnp.exp(jnp.minimum(b - ref2, 0.0))).astype(BF16),
                k2=(k * jnp.exp(jnp.minimum(ref2 - b, 0.0))).astype(BF16),
                q3=(q * jnp.exp(b - ref3)).astype(BF16), k3=(k * jnp.exp(ref3 - b)).astype(BF16),
                decay=jnp.exp(b_last))
        levels = {}
        for c in cs:
            x = scaled[c]
            for h, (kl, _) in enumerate(heads):
                levels[c, h] = (_dot_nt(x["q1"][:, kl], x["k1"][:, kl]), _dot_nt(x["q2"][:, kl], x["k2"][:, kl]),
                                _dot_nt(x["q3"][:, kl], x["k3"][:, kl]))
        intra, update = {}, {}
        for c in cs:
            for h, (kl, vl) in enumerate(heads):
                l1, l2, l3 = levels[c, h]
                a = jnp.where(m1, l1, 0.0) + jnp.where(m2, l2, 0.0) + jnp.where(m3, l3, 0.0)
                vf = chunk(vg_ref, c, vl)
                intra[c, h] = jnp.dot(a.astype(BF16), vf.astype(BF16), preferred_element_type=F32)
                update[c, h] = jnp.dot(vf.T.astype(BF16), scaled[c]["ks"][:, kl], preferred_element_type=F32)
        for c in cs:
            for h, (kl, vl) in enumerate(heads):
                o = _dot_nt(scaled[c]["qs"][:, kl], sts[h].astype(BF16)) + intra[c, h]
                sts[h] = sts[h] * scaled[c]["decay"][:, kl] + update[c, h]
                y_ref[0, rows[c], vl] = (_rms(o[:valid], gn_ref[:, vl]) * rg_ref[0, rows[c], vl]).astype(BF16)
    for (kl, _), st in zip(heads, sts):
        st_ref[:, kl] = st

    @pl.when(t == pl.num_programs(1) - 1)
    def _():
        for h in range(GLA_HEADS):
            sfin_ref[0, h] = st_ref[:, h * GLA_DK:(h + 1) * GLA_DK].T


def _gla(qg, kg, vg, rg, lg, s0, gn):
    Bn, T, _ = qg.shape
    assert T < GLA_CHUNK or T % GLA_CHUNK == 0
    TT = min(T, GLA_SEQ_TILE)
    kspec = pl.BlockSpec((1, TT, GLA_KW), lambda b, t: (b, t, 0))
    vspec = pl.BlockSpec((1, TT, GLA_VW), lambda b, t: (b, t, 0))
    sspec = pl.BlockSpec((1, GLA_HEADS, GLA_DK, GLA_DV), lambda b, t: (b, 0, 0, 0))
    return pl.pallas_call(
        functools.partial(_gla_kernel, n_chunks=max(1, TT // GLA_CHUNK), valid=min(TT, GLA_CHUNK)),
        grid=(Bn, T // TT),
        in_specs=[kspec, kspec, vspec, vspec, kspec, sspec, _const_spec((1, GLA_VW))],
        out_specs=[vspec, sspec],
        out_shape=[jax.ShapeDtypeStruct((Bn, T, GLA_VW), BF16),
                   jax.ShapeDtypeStruct((Bn, GLA_HEADS, GLA_DK, GLA_DV), F32)],
        scratch_shapes=[pltpu.VMEM((GLA_DV, GLA_KW), F32)],
        compiler_params=_cparams(("arbitrary", "arbitrary")),
        name="gla",
    )(qg, kg, vg, rg, lg, s0, gn)


def _merge_ffn_kernel(x_ref, yp_ref, ym_ref, yg_ref, sg_ref, wp_ref, wmb_ref, wgl_ref, wo_ref,
                      gpost_ref, gfpre_ref, gfpost_ref, w1_ref, w2_ref, o_ref):
    tm = x_ref.shape[0]
    subs = [slice(r, r + FFN_SUB_ROWS) for r in range(0, tm, FFN_SUB_ROWS)] if tm > FFN_SUB_ROWS else [slice(0, tm)]

    def dot(a, w):
        return jnp.dot(a.astype(BF16), w, preferred_element_type=F32)

    branches = [[dot(y_ref[r, :], w_ref[...]) for y_ref, w_ref in
                 ((yp_ref, wp_ref), (ym_ref, wmb_ref), (yg_ref, wgl_ref))] for r in subs]
    merged = [sum(sg_ref[r, 1024 * j:1024 * (j + 1)] * b for j, b in enumerate(bs))
              for r, bs in zip(subs, branches)]
    z = [dot(m, wo_ref[...]) for m in merged]
    x1 = [x_ref[r, :] + _rms(zi, gpost_ref[...]) for r, zi in zip(subs, z)]
    hb = [_rms(xi, gfpre_ref[...]).astype(BF16) for xi in x1]
    FC = 1024
    f = [jnp.zeros(xi.shape, F32) for xi in x1]

    def second(i, c, a):
        a = jnp.maximum(a, 0.0)
        f[i] = f[i] + dot(a * a, w2_ref[c * FC:(c + 1) * FC, :])

    pending = None
    for c in range(D_FF // FC):
        for i in range(len(subs)):
            a = jnp.dot(hb[i], w1_ref[:, c * FC:(c + 1) * FC], preferred_element_type=F32)
            if pending is not None:
                second(*pending)
            pending = (i, c, a)
    second(*pending)
    for i, r in enumerate(subs):
        o_ref[r, :] = x1[i] + _rms(f[i], gfpost_ref[...])


def _merge_ffn(x, yp, ym, yg, sg, layer, wp, wmb, wgl, wo, gpost, gfpre, gfpost, w1, w2):
    M = x.shape[0]
    tm = min(FFN_ROW_TILE, M)
    row = lambda w: pl.BlockSpec((tm, w), lambda i: (i, 0))
    vec = _const_spec((1, D_MODEL))
    return pl.pallas_call(
        _merge_ffn_kernel,
        grid=(M // tm,),
        in_specs=[row(D_MODEL), row(POOL_W), row(MB_W), row(GLA_VW), row(3 * D_MODEL),
                  *(_layer_spec(w, layer) for w in (wp, wmb, wgl, wo)),
                  vec, vec, vec, _layer_spec(w1, layer), _layer_spec(w2, layer)],
        out_specs=row(D_MODEL),
        out_shape=jax.ShapeDtypeStruct((M, D_MODEL), F32),
        compiler_params=_cparams(("arbitrary",)),
        name="merge_ffn",
    )(x, yp, ym, yg, sg, wp, wmb, wgl, wo, gpost, gfpre, gfpost, w1, w2)


def _layer(x, Bn, T, tabs, hist16, pos0, s0, sample_attend, layer, lw, prompt_state=None):
    (w_in, wg, wa2, ba, gn, wgrp, pscale, wp, wmb, wgl, wo, gpre, gpost, gfpre, gfpost, w1, w2) = lw
    seq = lambda a: a.reshape(Bn, T, a.shape[-1])
    if sample_attend is None:
        u, q, kb, km, k, v, qg, kg, vg, rg, lg, sg = _in_proj(x, gpre, w_in, layer, wg, wa2, ba, tabs,
                                                             prompt_state=(T,) + tuple(prompt_state))
        ym = _moba_prompt(seq(q), km.reshape(Bn, T // MB_BLOCK, MB_W), seq(kb), v, layer)
    else:
        u, q, k, v, qg, kg, vg, rg, lg, sg = _in_proj(x, gpre, w_in, layer, wg, wa2, ba, tabs)
        ym = sample_attend(seq(q), seq(k), seq(v))
        k, v = (a.reshape(Bn, T, MB_HEADS, MB_HD) for a in (k, v))
    u3 = seq(u)
    yp = _pool(u3, hist16, wgrp, pscale, pos0)
    yg, s_new = _gla(seq(qg), seq(kg), seq(vg), seq(rg), seq(lg), s0, gn)
    yg = yg.reshape(Bn * T, GLA_VW)
    x = _merge_ffn(x, yp.reshape(Bn * T, POOL_W), ym.reshape(Bn * T, MB_W), yg, sg, layer,
                   wp, wmb, wgl, wo, gpost, gfpre, gfpost, w1, w2)
    new_hist = jnp.concatenate([hist16[:, 1:], u3], axis=1)[:, -POOL_HIST:]
    return x, k, v, new_hist, s_new


def kernel(x_prompt, x_sample, cache_k, cache_v, state_pool, state_gla, page_table, w_in, w_gla_a2, b_gla_a,
           gla_norm, w_pool_grp, pool_scale, w_pool_out, w_mb_out, w_gla_out, w_o, norm_mix_pre,
           norm_mix_post, norm_ffn_pre, norm_ffn_post, w_ff1, w_ff2):
    B, T, D = x_prompt.shape
    DB, TS, _ = x_sample.shape
    depth = w_in.shape[0]
    n_pages = page_table.shape[1]
    past_len = n_pages * PAGE_SIZE
    assert D == D_MODEL and T % MB_BLOCK == 0 and past_len % MB_BLOCK == 0
    assert past_len // MB_BLOCK >= MB_TOPK and (B * T) % ROW_TILE == 0 and MB_BLOCK == 2 * PAGE_SIZE

    tabs_p = _rope_tables(jnp.arange(T))
    reps = max(1, min(ROW_TILE, DB * TS) // TS)
    tabs_s = tuple(jnp.tile(t, (reps, 1)) for t in _rope_tables(past_len + jnp.arange(TS)))
    ckt = jnp.transpose(cache_k, (0, 1, 3, 4, 2))
    cvt = jnp.transpose(cache_v, (0, 1, 3, 4, 2))
    zrow = lambda n: jnp.zeros((n, 1, POOL_W), F32)
    hist_p = jnp.zeros((B, 16, POOL_W), F32)
    s0_p = jnp.zeros((B, GLA_HEADS, GLA_DK, GLA_DV), F32)

    hp = x_prompt.reshape(B * T, D)
    hs = x_sample.reshape(DB * TS, D)
    outs_p, outs_s = [], []
    kt_all = vt_all = None
    bf = lambda a: a.astype(BF16)
    vec = lambda a: a.reshape(1, -1)
    w_in_b, wp_b, wmb_b, wgl_b, wo_b, w1_b, w2_b = (bf(w) for w in (w_in, w_pool_out, w_mb_out, w_gla_out,
                                                                    w_o, w_ff1, w_ff2))
    for l in range(depth):
        lw = (w_in_b, w_in_b[l][:, W_MAIN + GLA_RANK:],
              bf(w_gla_a2[l]), vec(b_gla_a[l]), vec(gla_norm[l]), bf(w_pool_grp[l]), vec(pool_scale[l]),
              wp_b, wmb_b, wgl_b, wo_b,
              vec(norm_mix_pre[l]), vec(norm_mix_post[l]), vec(norm_ffn_pre[l]), vec(norm_ffn_post[l]),
              w1_b, w2_b)
        wvt = w_in_b[l][:, COL_V:COL_QG].T
        hp, kt_all, vt_all, *op = _layer(hp, B, T, tabs_p, hist_p, 0, s0_p, None, l, lw,
                                         prompt_state=(depth, l, wvt, kt_all, vt_all))
        hist_s = jnp.concatenate([zrow(DB), state_pool[l]], axis=1)
        attend_s = functools.partial(_moba_sample, cache_kt=ckt, cache_vt=cvt, page_table=page_table, layer=l)
        hs, *os_ = _layer(hs, DB, TS, tabs_s, hist_s, past_len, state_gla[l], attend_s, l, lw)
        outs_p.append(op)
        outs_s.append(os_)
    stack = lambda outs, i: jnp.stack([o[i] for o in outs])
    untr = lambda a: jnp.transpose(a.reshape(depth, B, MB_HEADS, MB_HD, T), (0, 1, 4, 2, 3))
    return (hp.reshape(B, T, D), hs.reshape(DB, TS, D),
            untr(kt_all), untr(vt_all), stack(outs_p, 0), stack(outs_p, 1),
            stack(outs_s, 0), stack(outs_s, 1), stack(outs_s, 2), stack(outs_s, 3))
```
